```python
import math
import jax, jax.numpy as jnp
from jax import lax
import numpy as np

D_MODEL = 1024
BATCH = 4
SEQ = 8192
DEPTH = 2

CHUNK = 64
Q_BLOCK = 128
HEAD_DIM = 64
N_FOX_HEADS = D_MODEL // HEAD_DIM
FOX_WIDTH = N_FOX_HEADS * HEAD_DIM
N_DIFF_HEADS = D_MODEL // (2 * HEAD_DIM)
DIFF_QK_WIDTH = N_DIFF_HEADS * HEAD_DIM
DIFF_V_DIM = 2 * HEAD_DIM
DIFF_WIDTH = N_DIFF_HEADS * DIFF_V_DIM
N_A_LAYERS = DEPTH // 2
N_B_LAYERS = DEPTH - N_A_LAYERS
NUM_BUCKETS = 32
MAX_DISTANCE = 128
RMS_EPS = 1e-6
NEG_INF = -1e30

kernel_name = "yoco_fox_diffattn_hybrid"


def _rms_norm(t, g):
    tf = t.astype(jnp.float32)
    y = tf * lax.rsqrt(jnp.mean(tf * tf, axis=-1, keepdims=True) + RMS_EPS)
    return (y * g.astype(jnp.float32)).astype(t.dtype)


def _split_heads(t, n_heads, dim):
    b, s, _ = t.shape
    return t.reshape(b, s, n_heads, dim).transpose(0, 2, 1, 3)


def _to_blocks(t):
    b, h, s = t.shape[:3]
    nblk = s // Q_BLOCK
    t = t.reshape((b, h, nblk, Q_BLOCK) + t.shape[3:])
    return jnp.moveaxis(t, 2, 0)


def _merge_blocks(o):
    nblk, b, h, qb, dh = o.shape
    return o.transpose(1, 0, 3, 2, 4).reshape(b, nblk * qb, h * dh)


def _t5_bucket(rel):
    half = NUM_BUCKETS // 2
    max_exact = half // 2
    ret = jnp.where(rel > 0, half, 0)
    n = jnp.abs(rel)
    n_f = jnp.maximum(n, 1).astype(jnp.float32)
    large = max_exact + (jnp.log(n_f / max_exact) / math.log(MAX_DISTANCE / max_exact)
                         * (half - max_exact)).astype(jnp.int32)
    large = jnp.minimum(large, half - 1)
    return ret + jnp.where(n < max_exact, n, large)


def _fox_layer(h, g_norm, w_in, b_f, g_q, g_k, w_out):
    b, s, _ = h.shape
    u = _rms_norm(h, g_norm)
    proj = jnp.einsum('bsd,de->bse', u, w_in)
    q, k, v, gate, f_logit = jnp.split(
        proj, [FOX_WIDTH, 2 * FOX_WIDTH, 3 * FOX_WIDTH, 4 * FOX_WIDTH], axis=-1)
    q = _rms_norm(_split_heads(q, N_FOX_HEADS, HEAD_DIM), g_q)
    k = _rms_norm(_split_heads(k, N_FOX_HEADS, HEAD_DIM), g_k)
    v = _split_heads(v, N_FOX_HEADS, HEAD_DIM)
    log_f = jax.nn.log_sigmoid((f_logit + b_f).astype(jnp.float32))
    c = jnp.cumsum(log_f, axis=1).transpose(0, 2, 1)
    key_pos = jnp.arange(s)
    scale = HEAD_DIM ** -0.5
    nblk = s // Q_BLOCK

    def block(args):
        qb, cb, i = args
        qpos = i * Q_BLOCK + jnp.arange(Q_BLOCK)
        sc = jnp.einsum('bhqd,bhkd->bhqk', qb, k).astype(jnp.float32) * scale
        sc = sc + cb[..., :, None] - c[..., None, :]
        sc = jnp.where(key_pos[None, :] <= qpos[:, None], sc, NEG_INF)
        p = jax.nn.softmax(sc, axis=-1)
        return jnp.einsum('bhqk,bhkd->bhqd', p.astype(v.dtype), v)

    o = lax.map(block, (_to_blocks(q), _to_blocks(c), jnp.arange(nblk)))
    o = _merge_blocks(o) * jax.nn.silu(gate)
    return h + jnp.einsum('bse,ed->bsd', o, w_out)


def _shared_kv(h, g_norm, w_kv, g_k):
    u = _rms_norm(h, g_norm)
    kv = jnp.einsum('bsd,de->bse', u, w_kv)
    k1, k2, v = jnp.split(kv, [DIFF_QK_WIDTH, 2 * DIFF_QK_WIDTH], axis=-1)
    k1 = _rms_norm(_split_heads(k1, N_DIFF_HEADS, HEAD_DIM), g_k[0])
    k2 = _rms_norm(_split_heads(k2, N_DIFF_HEADS, HEAD_DIM), g_k[1])
    v = _split_heads(v, N_DIFF_HEADS, DIFF_V_DIM)
    return k1, k2, v


def _diff_layer(h, k1, k2, v, rel_bias, g_norm, w_in, g_q, lam_q1, lam_k1,
                lam_q2, lam_k2, g_sub, w_out, lambda_init):
    b, s, _ = h.shape
    u = _rms_norm(h, g_norm)
    proj = jnp.einsum('bsd,de->bse', u, w_in)
    q1, q2, gate = jnp.split(proj, [DIFF_QK_WIDTH, 2 * DIFF_QK_WIDTH], axis=-1)
    q1 = _rms_norm(_split_heads(q1, N_DIFF_HEADS, HEAD_DIM), g_q[0])
    q2 = _rms_norm(_split_heads(q2, N_DIFF_HEADS, HEAD_DIM), g_q[1])
    lam = (jnp.exp(jnp.sum(lam_q1.astype(jnp.float32) * lam_k1.astype(jnp.float32)))
           - jnp.exp(jnp.sum(lam_q2.astype(jnp.float32) * lam_k2.astype(jnp.float32)))
           + lambda_init)
    key_pos = jnp.arange(s)
    scale = HEAD_DIM ** -0.5
    nblk = s // Q_BLOCK

    def block(args):
        q1b, q2b, i = args
        qpos = i * Q_BLOCK + jnp.arange(Q_BLOCK)
        bucket = _t5_bucket(key_pos[None, :] - qpos[:, None])
        bias = rel_bias[bucket].astype(jnp.float32).transpose(2, 3, 0, 1)
        mask = (key_pos[None, :] // CHUNK) <= (qpos[:, None] // CHUNK)
        s1 = jnp.einsum('bhqd,bhkd->bhqk', q1b, k1).astype(jnp.float32) * scale + bias[0]
        s2 = jnp.einsum('bhqd,bhkd->bhqk', q2b, k2).astype(jnp.float32) * scale + bias[1]
        p = (jax.nn.softmax(jnp.where(mask, s1, NEG_INF), axis=-1)
             - lam * jax.nn.softmax(jnp.where(mask, s2, NEG_INF), axis=-1))
        return jnp.einsum('bhqk,bhkv->bhqv', p.astype(v.dtype), v)

    o = lax.map(block, (_to_blocks(q1), _to_blocks(q2), jnp.arange(nblk)))
    o = _rms_norm(o, g_sub) * (1.0 - lambda_init)
    o = _merge_blocks(o) * jax.nn.silu(gate)
    return h + jnp.einsum('bse,ed->bsd', o, w_out)


def setup_inputs(seed: int = 0) -> dict:
    key = jax.random.key(seed)
    ks = jax.random.split(key, 24)
    f32 = jnp.float32
    D = D_MODEL
    a_in_cols = 4 * FOX_WIDTH + N_FOX_HEADS
    b_in_cols = 2 * DIFF_QK_WIDTH + DIFF_WIDTH
    kv_cols = 2 * DIFF_QK_WIDTH + DIFF_WIDTH

    def nrm(k, shape, scale):
        return jax.random.normal(k, shape, f32) * scale

    def gain(k, shape):
        return 1.0 + 0.05 * jax.random.normal(k, shape, f32)

    return {
        "x": jax.random.normal(ks[0], (BATCH, SEQ, D), f32),
        "a_norm": gain(ks[1], (N_A_LAYERS, D)),
        "a_w_in": nrm(ks[2], (N_A_LAYERS, D, a_in_cols), D ** -0.5),
        "a_b_f": jax.random.uniform(ks[3], (N_A_LAYERS, N_FOX_HEADS), f32, 1.0, 4.0),
        "a_q_norm": gain(ks[4], (N_A_LAYERS, HEAD_DIM)),
        "a_k_norm": gain(ks[5], (N_A_LAYERS, HEAD_DIM)),
        "a_w_out": nrm(ks[6], (N_A_LAYERS, FOX_WIDTH, D), FOX_WIDTH ** -0.5),
        "kv_norm": gain(ks[7], (D,)),
        "kv_w": nrm(ks[8], (D, kv_cols), D ** -0.5),
        "kv_k_norm": gain(ks[9], (2, HEAD_DIM)),
        "rel_bias": nrm(ks[10], (NUM_BUCKETS, 2, N_DIFF_HEADS), 0.5),
        "b_norm": gain(ks[11], (N_B_LAYERS, D)),
        "b_w_in": nrm(ks[12], (N_B_LAYERS, D, b_in_cols), D ** -0.5),
        "b_q_norm": gain(ks[13], (N_B_LAYERS, 2, HEAD_DIM)),
        "b_lam_q1": nrm(ks[14], (N_B_LAYERS, HEAD_DIM), 0.1),
        "b_lam_k1": nrm(ks[15], (N_B_LAYERS, HEAD_DIM), 0.1),
        "b_lam_q2": nrm(ks[16], (N_B_LAYERS, HEAD_DIM), 0.1),
        "b_lam_k2": nrm(ks[17], (N_B_LAYERS, HEAD_DIM), 0.1),
        "b_sub_norm": gain(ks[18], (N_B_LAYERS, DIFF_V_DIM)),
        "b_w_out": nrm(ks[19], (N_B_LAYERS, DIFF_WIDTH, D), DIFF_WIDTH ** -0.5),
    }


def reference(x, a_norm, a_w_in, a_b_f, a_q_norm, a_k_norm, a_w_out, kv_norm, kv_w,
              kv_k_norm, rel_bias, b_norm, b_w_in, b_q_norm, b_lam_q1, b_lam_k1,
              b_lam_q2, b_lam_k2, b_sub_norm, b_w_out):
    h = x
    k1 = k2 = v = None
    for layer in range(DEPTH):
        if layer < N_A_LAYERS:
            h = _fox_layer(h, a_norm[layer], a_w_in[layer], a_b_f[layer],
                           a_q_norm[layer], a_k_norm[layer], a_w_out[layer])
            if layer == N_A_LAYERS - 1:
                k1, k2, v = _shared_kv(h, kv_norm, kv_w, kv_k_norm)
        else:
            j = layer - N_A_LAYERS
            lambda_init = 0.8 - 0.6 * math.exp(-0.3 * layer)
            h = _diff_layer(h, k1, k2, v, rel_bias, b_norm[j], b_w_in[j], b_q_norm[j],
                            b_lam_q1[j], b_lam_k1[j], b_lam_q2[j], b_lam_k2[j],
                            b_sub_norm[j], b_w_out[j], lambda_init)
    return h
```

```python
import functools
import math

import jax
import jax.numpy as jnp
from jax import lax
from jax.experimental import pallas as pl
from jax.experimental.pallas import tpu as pltpu

HEAD_DIM = 64
DIFF_V_DIM = 2 * HEAD_DIM
CHUNK = 64
NUM_BUCKETS = 32
MAX_DISTANCE = 128
RMS_EPS = 1e-6
NEG_INF = -1e30
QK_SCALE = HEAD_DIM ** -0.5
LAMBDA_INIT = 0.8 - 0.6 * math.exp(-0.3 * 1)

ROW_TILE = 512
ATTN_TILE = 512
FOX_HEADS_PER_STEP = 2
V7X_VMEM_LIMIT_BYTES = 56 * 1024 * 1024

F32 = jnp.float32
BF16 = jnp.bfloat16
_NT = (((1,), (1,)), ((), ()))
_TN = (((0,), (0,)), ((), ()))


def _params(n_axes):
    return pltpu.CompilerParams(
        dimension_semantics=("arbitrary",) * n_axes,
        vmem_limit_bytes=V7X_VMEM_LIMIT_BYTES)


def _head_norm(p_t, g_col, n_heads, scale):
    t = p_t.shape[-1]
    p3 = p_t.reshape(n_heads, HEAD_DIM, t)
    ms = jnp.mean(p3 * p3, axis=1, keepdims=True)
    y = p3 * lax.rsqrt(ms + RMS_EPS) * (g_col * scale)[None]
    return y.reshape(n_heads * HEAD_DIM, t)


def _split3(c):
    hi = c.astype(BF16).astype(F32)
    r = c - hi
    mid = r.astype(BF16).astype(F32)
    lo = (r - mid).astype(BF16).astype(F32)
    return hi, mid, lo


def _silu(g):
    return g / (1.0 + jnp.exp(-g))


def _fox_proj_kernel(x_ref, gn_ref, wq_ref, wk_ref, wv_ref, wg_ref, wf_ref, bf_ref,
                     gq_ref, gk_ref, q_ref, k_ref, v_ref, g_ref, c_ref, carry_ref,
                     *, n_heads):
    @pl.when(pl.program_id(1) == 0)
    def _():
        carry_ref[...] = jnp.zeros_like(carry_ref)

    x = x_ref[0]
    tm = x.shape[0]
    ms = jnp.mean(x * x, axis=-1, keepdims=True)
    u = (x * lax.rsqrt(ms + RMS_EPS) * gn_ref[...]).astype(BF16)

    def proj(w_ref):
        return lax.dot_general(w_ref[...], u, _NT, preferred_element_type=F32)

    q_ref[0] = _head_norm(proj(wq_ref), gq_ref[...], n_heads, QK_SCALE).astype(BF16)
    k_ref[0, 0] = _head_norm(proj(wk_ref), gk_ref[...], n_heads, 1.0).astype(BF16)
    v_ref[0, 0] = proj(wv_ref).astype(BF16)
    g_ref[0] = proj(wg_ref).astype(BF16)

    z = proj(wf_ref) + bf_ref[...]
    log_f = jnp.minimum(z, 0.0) - jnp.log1p(jnp.exp(-jnp.abs(z)))
    lane = lax.broadcasted_iota(jnp.int32, log_f.shape, 1)
    cs = log_f
    shift = 1
    while shift < tm:
        cs = cs + jnp.where(lane >= shift, pltpu.roll(cs, shift, axis=1), 0.0)
        shift *= 2
    c = cs + carry_ref[:, 0:1]
    c_ref[0, 0] = c
    carry_ref[...] = jnp.broadcast_to(c[:, tm - 1:tm], carry_ref.shape)


def _fox_proj(x, a_norm, w_in, b_f, g_q, g_k):
    b, s, d = x.shape
    n_heads = b_f.shape[0]
    width = n_heads * HEAD_DIM
    tm = ROW_TILE
    w_t = w_in.T.astype(BF16)
    wq, wk, wv, wg, wf = (w_t[0:width], w_t[width:2 * width], w_t[2 * width:3 * width],
                          w_t[3 * width:4 * width], w_t[4 * width:])
    const = lambda *_: (0, 0)
    w_spec = pl.BlockSpec((width, d), const)
    col = pl.BlockSpec((HEAD_DIM, 1), const)
    return pl.pallas_call(
        functools.partial(_fox_proj_kernel, n_heads=n_heads),
        grid=(b, s // tm),
        in_specs=[
            pl.BlockSpec((1, tm, d), lambda bi, i: (bi, i, 0)),
            pl.BlockSpec((1, d), const),
            w_spec, w_spec, w_spec, w_spec,
            pl.BlockSpec((n_heads, d), const),
            pl.BlockSpec((n_heads, 1), const),
            col, col,
        ],
        out_specs=[
            pl.BlockSpec((1, width, tm), lambda bi, i: (bi, 0, i)),
            pl.BlockSpec((1, 1, width, tm), lambda bi, i: (bi, i, 0, 0)),
            pl.BlockSpec((1, 1, width, tm), lambda bi, i: (bi, i, 0, 0)),
            pl.BlockSpec((1, width, tm), lambda bi, i: (bi, 0, i)),
            pl.BlockSpec((1, 1, n_heads, tm), lambda bi, i: (bi, i, 0, 0)),
        ],
        out_shape=[
            jax.ShapeDtypeStruct((b, width, s), BF16),
            jax.ShapeDtypeStruct((b, s // tm, width, tm), BF16),
            jax.ShapeDtypeStruct((b, s // tm, width, tm), BF16),
            jax.ShapeDtypeStruct((b, width, s), BF16),
            jax.ShapeDtypeStruct((b, s // tm, n_heads, tm), F32),
        ],
        scratch_shapes=[pltpu.VMEM((n_heads, 128), F32)],
        compiler_params=_params(2),
        name="fox_proj",
    )(x, a_norm.reshape(1, d), wq, wk, wv, wg, wf, b_f.reshape(n_heads, 1),
      g_q.reshape(HEAD_DIM, 1), g_k.reshape(HEAD_DIM, 1))


def _online_update(s, v, m_ref, l_ref, acc_ref, idx):
    m_prev = m_ref[idx]
    m_new = jnp.maximum(m_prev, jnp.max(s, axis=0, keepdims=True))
    alpha = jnp.exp(m_prev - m_new)
    p = jnp.exp(s - m_new)
    l_ref[idx] = alpha * l_ref[idx] + jnp.sum(p, axis=0, keepdims=True)
    pv = jnp.dot(v, p.astype(BF16), preferred_element_type=F32)
    acc_ref[idx] = alpha * acc_ref[idx] + pv
    m_ref[idx] = m_new


def _aug_rows(rows8, shape):
    row = lax.broadcasted_iota(jnp.int32, shape, 0)
    out = jnp.zeros(shape, F32)
    for r, val in enumerate(rows8):
        out = jnp.where(row == r, val, out)
    return out


def _fox_attn_kernel(q_ref, k_ref, v_ref, c_ref, g_ref, o_ref,
                     kaug_ref, m_ref, l_ref, acc_ref, *, hp, n_chunks):
    qi = pl.program_id(2)
    t = q_ref.shape[-1]
    pad = jnp.zeros((2 * HEAD_DIM - HEAD_DIM - 8, t), F32)

    @pl.when(qi == 0)
    def _():
        def prep(j, carry):
            for h in range(hp):
                kt = k_ref[0, j, h * HEAD_DIM:(h + 1) * HEAD_DIM, :].astype(F32)
                hi, mid, lo = _split3(c_ref[0, j, 0, h:h + 1, :])
                aug = _aug_rows([-hi, -mid, -lo, 1.0, 1.0, 1.0], (8, t))
                kaug_ref[h, j] = jnp.concatenate([kt, aug, pad], axis=0).T.astype(BF16)
            return carry
        lax.fori_loop(0, n_chunks, prep, 0)

    qa = []
    for h in range(hp):
        qf = q_ref[0, h * HEAD_DIM:(h + 1) * HEAD_DIM, :].astype(F32)
        hi, mid, lo = _split3(c_ref[0, qi, 0, h:h + 1, :])
        aug = _aug_rows([1.0, 1.0, 1.0, hi, mid, lo], (8, t))
        qa.append(jnp.concatenate([qf, aug, pad], axis=0).astype(BF16))

    m_ref[...] = jnp.full(m_ref.shape, NEG_INF, F32)
    l_ref[...] = jnp.zeros_like(l_ref)
    acc_ref[...] = jnp.zeros_like(acc_ref)

    def scores(h, j):
        return jnp.dot(kaug_ref[h, j], qa[h], preferred_element_type=F32)

    def values(h, j):
        return v_ref[0, j, h * HEAD_DIM:(h + 1) * HEAD_DIM, :]

    def body(j, carry):
        for h in range(hp):
            _online_update(scores(h, j), values(h, j), m_ref, l_ref, acc_ref, h)
        return carry
    lax.fori_loop(0, qi, body, 0)

    kpos = lax.broadcasted_iota(jnp.int32, (t, t), 0)
    qpos = lax.broadcasted_iota(jnp.int32, (t, t), 1)
    causal = kpos <= qpos
    for h in range(hp):
        s = jnp.where(causal, scores(h, qi), NEG_INF)
        _online_update(s, values(h, qi), m_ref, l_ref, acc_ref, h)

    for h in range(hp):
        o = acc_ref[h] * (1.0 / l_ref[h])
        g = g_ref[0, h * HEAD_DIM:(h + 1) * HEAD_DIM, :].astype(F32)
        o_ref[0, h * HEAD_DIM:(h + 1) * HEAD_DIM, :] = (o * _silu(g)).astype(BF16)


def _fox_attn(q_t, k_c, v_c, c_c, g_t):
    b, width, s = q_t.shape
    t = ATTN_TILE
    hp = FOX_HEADS_PER_STEP
    n_heads = width // HEAD_DIM
    n_chunks = s // t
    rows = hp * HEAD_DIM
    c_c = c_c.reshape(b, n_chunks, n_heads // hp, hp, t)
    tile = pl.BlockSpec((1, rows, t), lambda bi, hg, qi: (bi, hg, qi))
    full = pl.BlockSpec((1, n_chunks, rows, t), lambda bi, hg, qi: (bi, 0, hg, 0))
    return pl.pallas_call(
        functools.partial(_fox_attn_kernel, hp=hp, n_chunks=n_chunks),
        grid=(b, n_heads // hp, n_chunks),
        in_specs=[
            tile, full, full,
            pl.BlockSpec((1, n_chunks, 1, hp, t), lambda bi, hg, qi: (bi, 0, hg, 0, 0)),
            tile,
        ],
        out_specs=tile,
        out_shape=jax.ShapeDtypeStruct((b, width, s), BF16),
        scratch_shapes=[
            pltpu.VMEM((hp, n_chunks, t, 2 * HEAD_DIM), BF16),
            pltpu.VMEM((hp, 1, t), F32),
            pltpu.VMEM((hp, 1, t), F32),
            pltpu.VMEM((hp, HEAD_DIM, t), F32),
        ],
        compiler_params=_params(3),
        name="fox_attn",
    )(q_t, k_c, v_c, c_c, g_t)


def _mid_kernel(o_ref, x_ref, wo_ref, gkv_ref, gb_ref, wkv_ref, wb_ref, gk_ref, gq_ref,
                h_ref, k_ref, v_ref, q_ref, g_ref, *, n_heads):
    half = n_heads * HEAD_DIM
    h1 = x_ref[0] + lax.dot_general(o_ref[0], wo_ref[...], _TN, preferred_element_type=F32)
    h_ref[0] = h1
    ms = jnp.mean(h1 * h1, axis=-1, keepdims=True)
    hn = h1 * lax.rsqrt(ms + RMS_EPS)
    u_kv = (hn * gkv_ref[...]).astype(BF16)
    u_b = (hn * gb_ref[...]).astype(BF16)

    def pack_heads(ref, a, b2):
        for h in range(n_heads):
            sl = slice(h * HEAD_DIM, (h + 1) * HEAD_DIM)
            ref[..., h * 2 * HEAD_DIM:h * 2 * HEAD_DIM + HEAD_DIM, :] = a[sl]
            ref[..., h * 2 * HEAD_DIM + HEAD_DIM:(h + 1) * 2 * HEAD_DIM, :] = b2[sl]

    kv = lax.dot_general(wkv_ref[...], u_kv, _NT, preferred_element_type=F32)
    k1 = _head_norm(kv[0:half], gk_ref[0], n_heads, 1.0).astype(BF16)
    k2 = _head_norm(kv[half:2 * half], gk_ref[1], n_heads, 1.0).astype(BF16)
    pack_heads(k_ref.at[0, 0], k1, k2)
    v_ref[0, 0] = kv[2 * half:].astype(BF16)

    qg = lax.dot_general(wb_ref[...], u_b, _NT, preferred_element_type=F32)
    q1 = _head_norm(qg[0:half], gq_ref[0], n_heads, QK_SCALE).astype(BF16)
    q2 = _head_norm(qg[half:2 * half], gq_ref[1], n_heads, QK_SCALE).astype(BF16)
    pack_heads(q_ref.at[0], q1, q2)
    g_ref[0] = qg[2 * half:].astype(BF16)


def _mid(o_t, x, w_out, kv_norm, kv_w, kv_k_norm, b_norm, b_w_in, b_q_norm):
    b, s, d = x.shape
    tm = ROW_TILE
    n_heads = kv_w.shape[1] // (4 * HEAD_DIM)
    width = 2 * n_heads * HEAD_DIM
    const2 = lambda *_: (0, 0)
    const3 = lambda *_: (0, 0, 0)
    fm = pl.BlockSpec((1, width, tm), lambda bi, i: (bi, 0, i))
    tok = pl.BlockSpec((1, tm, d), lambda bi, i: (bi, i, 0))
    chunked = pl.BlockSpec((1, 1, width, tm), lambda bi, i: (bi, i, 0, 0))
    return pl.pallas_call(
        functools.partial(_mid_kernel, n_heads=n_heads),
        grid=(b, s // tm),
        in_specs=[
            pl.BlockSpec((1, o_t.shape[1], tm), lambda bi, i: (bi, 0, i)),
            tok,
            pl.BlockSpec(w_out.shape, const2),
            pl.BlockSpec((1, d), const2),
            pl.BlockSpec((1, d), const2),
            pl.BlockSpec((2 * width, d), const2),
            pl.BlockSpec((2 * width, d), const2),
            pl.BlockSpec((2, HEAD_DIM, 1), const3),
            pl.BlockSpec((2, HEAD_DIM, 1), const3),
        ],
        out_specs=[tok, chunked, chunked, fm, fm],
        out_shape=[
            jax.ShapeDtypeStruct((b, s, d), F32),
            jax.ShapeDtypeStruct((b, s // tm, width, tm), BF16),
            jax.ShapeDtypeStruct((b, s // tm, width, tm), BF16),
            jax.ShapeDtypeStruct((b, width, s), BF16),
            jax.ShapeDtypeStruct((b, width, s), BF16),
        ],
        compiler_params=_params(2),
        name="mid_proj",
    )(o_t, x, w_out.astype(BF16), kv_norm.reshape(1, d), b_norm.reshape(1, d),
      kv_w.T.astype(BF16), b_w_in.T.astype(BF16),
      kv_k_norm.reshape(2, HEAD_DIM, 1), b_q_norm.reshape(2, HEAD_DIM, 1))


def _diff_attn_kernel(q_ref, k_ref, v_ref, bias_ref, lam_ref, gs_ref, g_ref, o_ref,
                      ktok_ref, m_ref, l_ref, acc_ref, *, n_chunks):
    qi = pl.program_id(2)
    t = q_ref.shape[-1]

    @pl.when(qi == 0)
    def _():
        def prep(j, carry):
            ktok_ref[j] = k_ref[0, j].astype(F32).T.astype(BF16)
            return carry
        lax.fori_loop(0, n_chunks, prep, 0)

    q = q_ref[0]
    row = lax.broadcasted_iota(jnp.int32, q.shape, 0)
    zero = jnp.zeros_like(q)
    qa = (jnp.where(row < HEAD_DIM, q, zero), jnp.where(row >= HEAD_DIM, q, zero))

    m_ref[...] = jnp.full(m_ref.shape, NEG_INF, F32)
    l_ref[...] = jnp.zeros_like(l_ref)
    acc_ref[...] = jnp.zeros_like(acc_ref)

    def scores(mp, j):
        return jnp.dot(ktok_ref[j], qa[mp], preferred_element_type=F32)

    def body(j, carry):
        for mp in range(2):
            _online_update(scores(mp, j), v_ref[0, j], m_ref, l_ref, acc_ref, mp)
        return carry
    lax.fori_loop(0, jnp.maximum(qi - 1, 0), body, 0)

    @pl.when(qi > 0)
    def _():
        for mp in range(2):
            s = scores(mp, qi - 1) + bias_ref[0, mp, 1]
            _online_update(s, v_ref[0, qi - 1], m_ref, l_ref, acc_ref, mp)

    kchunk = lax.broadcasted_iota(jnp.int32, (t, t), 0) // CHUNK
    qchunk = lax.broadcasted_iota(jnp.int32, (t, t), 1) // CHUNK
    visible = kchunk <= qchunk
    for mp in range(2):
        s = jnp.where(visible, scores(mp, qi) + bias_ref[0, mp, 0], NEG_INF)
        _online_update(s, v_ref[0, qi], m_ref, l_ref, acc_ref, mp)

    lam_p = lam_ref[...]
    e1 = jnp.exp(jnp.sum(lam_p[0:1] * lam_p[1:2], axis=1, keepdims=True))
    e2 = jnp.exp(jnp.sum(lam_p[2:3] * lam_p[3:4], axis=1, keepdims=True))
    lam = e1 - e2 + LAMBDA_INIT
    o = acc_ref[0] * (1.0 / l_ref[0]) - lam * (acc_ref[1] * (1.0 / l_ref[1]))
    ms = jnp.mean(o * o, axis=0, keepdims=True)
    y = o * lax.rsqrt(ms + RMS_EPS) * gs_ref[...] * (1.0 - LAMBDA_INIT)
    o_ref[0] = (y * _silu(g_ref[0].astype(F32))).astype(BF16)


def _t5_bucket(rel):
    half = NUM_BUCKETS // 2
    max_exact = half // 2
    ret = jnp.where(rel > 0, half, 0)
    n = jnp.abs(rel)
    n_f = jnp.maximum(n, 1).astype(jnp.float32)
    large = max_exact + (jnp.log(n_f / max_exact) / math.log(MAX_DISTANCE / max_exact)
                         * (half - max_exact)).astype(jnp.int32)
    large = jnp.minimum(large, half - 1)
    return ret + jnp.where(n < max_exact, n, large)


def _bias_tiles(rel_bias, t):
    kk = jnp.arange(t)[:, None]
    qq = jnp.arange(t)[None, :]
    rel = jnp.stack([kk - qq, kk - qq - t])
    tiles = rel_bias[_t5_bucket(rel)].astype(F32)
    far = rel_bias[_t5_bucket(jnp.int32(-MAX_DISTANCE))].astype(F32)
    return (tiles - far).transpose(4, 3, 0, 1, 2)


def _diff_attn(q_t, k_c, v_c, g_t, rel_bias, lam_params, g_sub):
    b, width, s = q_t.shape
    t = ATTN_TILE
    n_heads = width // DIFF_V_DIM
    n_chunks = s // t
    tile = pl.BlockSpec((1, DIFF_V_DIM, t), lambda bi, h, qi: (bi, h, qi))
    full = pl.BlockSpec((1, n_chunks, DIFF_V_DIM, t), lambda bi, h, qi: (bi, 0, h, 0))
    return pl.pallas_call(
        functools.partial(_diff_attn_kernel, n_chunks=n_chunks),
        grid=(b, n_heads, n_chunks),
        in_specs=[
            tile, full, full,
            pl.BlockSpec((1, 2, 2, t, t), lambda bi, h, qi: (h, 0, 0, 0, 0)),
            pl.BlockSpec((4, HEAD_DIM), lambda *_: (0, 0)),
            pl.BlockSpec((DIFF_V_DIM, 1), lambda *_: (0, 0)),
            tile,
        ],
        out_specs=tile,
        out_shape=jax.ShapeDtypeStruct((b, width, s), BF16),
        scratch_shapes=[
            pltpu.VMEM((n_chunks, t, DIFF_V_DIM), BF16),
            pltpu.VMEM((2, 1, t), F32),
            pltpu.VMEM((2, 1, t), F32),
            pltpu.VMEM((2, DIFF_V_DIM, t), F32),
        ],
        compiler_params=_params(3),
        name="diff_attn",
    )(q_t, k_c, v_c, _bias_tiles(rel_bias, t), lam_params, g_sub.reshape(DIFF_V_DIM, 1), g_t)


def _out_kernel(o_ref, h_ref, w_ref, y_ref):
    y_ref[0] = h_ref[0] + lax.dot_general(o_ref[0], w_ref[...], _TN, preferred_element_type=F32)


def _out_proj(o_t, h1, w_out):
    b, s, d = h1.shape
    tm = ROW_TILE
    tok = pl.BlockSpec((1, tm, d), lambda bi, i: (bi, i, 0))
    return pl.pallas_call(
        _out_kernel,
        grid=(b, s // tm),
        in_specs=[
            pl.BlockSpec((1, o_t.shape[1], tm), lambda bi, i: (bi, 0, i)),
            tok,
            pl.BlockSpec(w_out.shape, lambda *_: (0, 0)),
        ],
        out_specs=tok,
        out_shape=jax.ShapeDtypeStruct((b, s, d), F32),
        compiler_params=_params(2),
        name="out_proj",
    )(o_t, h1, w_out.astype(BF16))


def kernel(x, a_norm, a_w_in, a_b_f, a_q_norm, a_k_norm, a_w_out, kv_norm, kv_w, kv_k_norm,
           rel_bias, b_norm, b_w_in, b_q_norm, b_lam_q1, b_lam_k1, b_lam_q2, b_lam_k2,
           b_sub_norm, b_w_out):
    assert a_norm.shape[0] == 1 and b_norm.shape[0] == 1
    assert x.shape[1] % ATTN_TILE == 0 and ATTN_TILE == ROW_TILE
    q_t, k_c, v_c, g_t, c_c = _fox_proj(x, a_norm[0], a_w_in[0], a_b_f[0], a_q_norm[0], a_k_norm[0])
    o_t = _fox_attn(q_t, k_c, v_c, c_c, g_t)
    h1, k2_c, v2_c, q2_t, g2_t = _mid(o_t, x, a_w_out[0], kv_norm, kv_w, kv_k_norm,
                                      b_norm[0], b_w_in[0], b_q_norm[0])
    lam_params = jnp.stack([b_lam_q1[0], b_lam_k1[0], b_lam_q2[0], b_lam_k2[0]])
    o2_t = _diff_attn(q2_t, k2_c, v2_c, g2_t, rel_bias, lam_params, b_sub_norm[0])
    return _out_proj(o2_t, h1, b_w_out[0])
```

```python
import functools
import math

import jax
import jax.numpy as jnp
from jax import lax
from jax.experimental import pallas as pl
from jax.experimental.pallas import tpu as pltpu

HEAD_DIM = 64
DIFF_V_DIM = 2 * HEAD_DIM
CHUNK = 64
NUM_BUCKETS = 32
MAX_DISTANCE = 128
RMS_EPS = 1e-6
NEG_INF = -1e30
QK_SCALE = HEAD_DIM ** -0.5
LAMBDA_INIT = 0.8 - 0.6 * math.exp(-0.3 * 1)

ROW_TILE = 512
ATTN_TILE = 512
FOX_HEADS_PER_STEP = 2
V7X_VMEM_LIMIT_BYTES = 56 * 1024 * 1024

F32 = jnp.float32
BF16 = jnp.bfloat16
_NT = (((1,), (1,)), ((), ()))
_TN = (((0,), (0,)), ((), ()))


def _params(n_axes):
    return pltpu.CompilerParams(
        dimension_semantics=("arbitrary",) * n_axes,
        vmem_limit_bytes=V7X_VMEM_LIMIT_BYTES)


def _head_norm(p_t, g_col, n_heads, scale):
    t = p_t.shape[-1]
    p3 = p_t.reshape(n_heads, HEAD_DIM, t)
    ms = jnp.mean(p3 * p3, axis=1, keepdims=True)
    y = p3 * lax.rsqrt(ms + RMS_EPS) * (g_col * scale)[None]
    return y.reshape(n_heads * HEAD_DIM, t)


def _split3(c):
    hi = c.astype(BF16).astype(F32)
    r = c - hi
    mid = r.astype(BF16).astype(F32)
    lo = (r - mid).astype(BF16).astype(F32)
    return hi, mid, lo


def _silu(g):
    return g / (1.0 + jnp.exp(-g))


def _fox_proj_kernel(x_ref, gn_ref, wq_ref, wk_ref, wv_ref, wg_ref, wf_ref, bf_ref,
                     gq_ref, gk_ref, q_ref, k_ref, v_ref, g_ref, c_ref, carry_ref,
                     *, n_heads):
    @pl.when(pl.program_id(1) == 0)
    def _():
        carry_ref[...] = jnp.zeros_like(carry_ref)

    x = x_ref[0]
    tm = x.shape[0]
    ms = jnp.mean(x * x, axis=-1, keepdims=True)
    u = (x * lax.rsqrt(ms + RMS_EPS) * gn_ref[...]).astype(BF16)

    def proj(w_ref):
        return lax.dot_general(w_ref[...], u, _NT, preferred_element_type=F32)

    q_ref[0] = _head_norm(proj(wq_ref), gq_ref[...], n_heads, QK_SCALE).astype(BF16)
    k_ref[0, 0] = _head_norm(proj(wk_ref), gk_ref[...], n_heads, 1.0).astype(BF16)
    v_ref[0, 0] = proj(wv_ref).astype(BF16)
    g_ref[0] = proj(wg_ref).astype(BF16)

    z = proj(wf_ref) + bf_ref[...]
    log_f = jnp.minimum(z, 0.0) - jnp.log1p(jnp.exp(-jnp.abs(z)))
    lane = lax.broadcasted_iota(jnp.int32, log_f.shape, 1)
    cs = log_f
    shift = 1
    while shift < tm:
        cs = cs + jnp.where(lane >= shift, pltpu.roll(cs, shift, axis=1), 0.0)
        shift *= 2
    c = cs + carry_ref[:, 0:1]
    c_ref[0, 0] = c
    carry_ref[...] = jnp.broadcast_to(c[:, tm - 1:tm], carry_ref.shape)


def _fox_proj(x, a_norm, w_in, b_f, g_q, g_k):
    b, s, d = x.shape
    n_heads = b_f.shape[0]
    width = n_heads * HEAD_DIM
    tm = ROW_TILE
    w_t = w_in.T.astype(BF16)
    wq, wk, wv, wg, wf = (w_t[0:width], w_t[width:2 * width], w_t[2 * width:3 * width],
                          w_t[3 * width:4 * width], w_t[4 * width:])
    const = lambda *_: (0, 0)
    w_spec = pl.BlockSpec((width, d), const)
    col = pl.BlockSpec((HEAD_DIM, 1), const)
    return pl.pallas_call(
        functools.partial(_fox_proj_kernel, n_heads=n_heads),
        grid=(b, s // tm),
        in_specs=[
            pl.BlockSpec((1, tm, d), lambda bi, i: (bi, i, 0)),
            pl.BlockSpec((1, d), const),
            w_spec, w_spec, w_spec, w_spec,
            pl.BlockSpec((n_heads, d), const),
            pl.BlockSpec((n_heads, 1), const),
            col, col,
        ],
        out_specs=[
            pl.BlockSpec((1, width, tm), lambda bi, i: (bi, 0, i)),
            pl.BlockSpec((1, 1, width, tm), lambda bi, i: (bi, i, 0, 0)),
            pl.BlockSpec((1, 1, width, tm), lambda bi, i: (bi, i, 0, 0)),
            pl.BlockSpec((1, width, tm), lambda bi, i: (bi, 0, i)),
            pl.BlockSpec((1, 1, n_heads, tm), lambda bi, i: (bi, i, 0, 0)),
        ],
        out_shape=[
            jax.ShapeDtypeStruct((b, width, s), BF16),
            jax.ShapeDtypeStruct((b, s // tm, width, tm), BF16),
            jax.ShapeDtypeStruct((b, s // tm, width, tm), BF16),
            jax.ShapeDtypeStruct((b, width, s), BF16),
            jax.ShapeDtypeStruct((b, s // tm, n_heads, tm), F32),
        ],
        scratch_shapes=[pltpu.VMEM((n_heads, 128), F32)],
        compiler_params=_params(2),
        name="fox_proj",
    )(x, a_norm.reshape(1, d), wq, wk, wv, wg, wf, b_f.reshape(n_heads, 1),
      g_q.reshape(HEAD_DIM, 1), g_k.reshape(HEAD_DIM, 1))


def _online_update(s, v, m_ref, l_ref, acc_ref, idx):
    m_prev = m_ref[idx]
    m_new = jnp.maximum(m_prev, jnp.max(s, axis=0, keepdims=True))
    alpha = jnp.exp(m_prev - m_new)
    p = jnp.exp(s - m_new)
    l_ref[idx] = alpha * l_ref[idx] + jnp.sum(p, axis=0, keepdims=True)
    pv = jnp.dot(v, p.astype(BF16), preferred_element_type=F32)
    acc_ref[idx] = alpha * acc_ref[idx] + pv
    m_ref[idx] = m_new


def _aug_rows(rows8, shape):
    row = lax.broadcasted_iota(jnp.int32, shape, 0)
    out = jnp.zeros(shape, F32)
    for r, val in enumerate(rows8):
        out = jnp.where(row == r, val, out)
    return out


def _fox_attn_kernel(q_ref, k_ref, v_ref, c_ref, g_ref, o_ref,
                     kaug_ref, m_ref, l_ref, acc_ref, *, hp, n_chunks):
    qi = pl.program_id(2)
    t = q_ref.shape[-1]
    pad = jnp.zeros((2 * HEAD_DIM - HEAD_DIM - 8, t), F32)

    @pl.when(qi == 0)
    def _():
        def prep(j, carry):
            for h in range(hp):
                kt = k_ref[0, j, h * HEAD_DIM:(h + 1) * HEAD_DIM, :].astype(F32)
                hi, mid, lo = _split3(c_ref[0, j, 0, h:h + 1, :])
                aug = _aug_rows([-hi, -mid, -lo, 1.0, 1.0, 1.0], (8, t))
                kaug_ref[h, j] = jnp.concatenate([kt, aug, pad], axis=0).T.astype(BF16)
            return carry
        lax.fori_loop(0, n_chunks, prep, 0)

    qa = []
    for h in range(hp):
        qf = q_ref[0, h * HEAD_DIM:(h + 1) * HEAD_DIM, :].astype(F32)
        hi, mid, lo = _split3(c_ref[0, qi, 0, h:h + 1, :])
        aug = _aug_rows([1.0, 1.0, 1.0, hi, mid, lo], (8, t))
        qa.append(jnp.concatenate([qf, aug, pad], axis=0).astype(BF16))

    m_ref[...] = jnp.full(m_ref.shape, NEG_INF, F32)
    l_ref[...] = jnp.zeros_like(l_ref)
    acc_ref[...] = jnp.zeros_like(acc_ref)

    def scores(h, j):
        return jnp.dot(kaug_ref[h, j], qa[h], preferred_element_type=F32)

    def values(h, j):
        return v_ref[0, j, h * HEAD_DIM:(h + 1) * HEAD_DIM, :]

    def body(j, carry):
        for h in range(hp):
            _online_update(scores(h, j), values(h, j), m_ref, l_ref, acc_ref, h)
        return carry
    lax.fori_loop(0, qi, body, 0)

    kpos = lax.broadcasted_iota(jnp.int32, (t, t), 0)
    qpos = lax.broadcasted_iota(jnp.int32, (t, t), 1)
    causal = kpos <= qpos
    for h in range(hp):
        s = jnp.where(causal, scores(h, qi), NEG_INF)
        _online_update(s, values(h, qi), m_ref, l_ref, acc_ref, h)

    for h in range(hp):
        o = acc_ref[h] * (1.0 / l_ref[h])
        g = g_ref[0, h * HEAD_DIM:(h + 1) * HEAD_DIM, :].astype(F32)
        o_ref[0, h * HEAD_DIM:(h + 1) * HEAD_DIM, :] = (o * _silu(g)).astype(BF16)


def _fox_attn(q_t, k_c, v_c, c_c, g_t):
    b, width, s = q_t.shape
    t = ATTN_TILE
    hp = FOX_HEADS_PER_STEP
    n_heads = width // HEAD_DIM
    n_chunks = s // t
    rows = hp * HEAD_DIM
    c_c = c_c.reshape(b, n_chunks, n_heads // hp, hp, t)
    tile = pl.BlockSpec((1, rows, t), lambda bi, hg, qi: (bi, hg, qi))
    full = pl.BlockSpec((1, n_chunks, rows, t), lambda bi, hg, qi: (bi, 0, hg, 0))
    return pl.pallas_call(
        functools.partial(_fox_attn_kernel, hp=hp, n_chunks=n_chunks),
        grid=(b, n_heads // hp, n_chunks),
        in_specs=[
            tile, full, full,
            pl.BlockSpec((1, n_chunks, 1, hp, t), lambda bi, hg, qi: (bi, 0, hg, 0, 0)),
            tile,
        ],
        out_specs=tile,
        out_shape=jax.ShapeDtypeStruct((b, width, s), BF16),
        scratch_shapes=[
            pltpu.VMEM((hp, n_chunks, t, 2 * HEAD_DIM), BF16),
            pltpu.VMEM((hp, 1, t), F32),
            pltpu.VMEM((hp, 1, t), F32),
            pltpu.VMEM((hp, HEAD_DIM, t), F32),
        ],
        compiler_params=_params(3),
        name="fox_attn",
    )(q_t, k_c, v_c, c_c, g_t)


def _mid_kernel(o_ref, x_ref, wo_ref, gkv_ref, gb_ref, wkv_ref, wb_ref, gk_ref, gq_ref,
                h_ref, k_ref, v_ref, q_ref, g_ref, *, n_heads):
    half = n_heads * HEAD_DIM
    h1 = x_ref[0] + lax.dot_general(o_ref[0], wo_ref[...], _TN, preferred_element_type=F32)
    h_ref[0] = h1
    ms = jnp.mean(h1 * h1, axis=-1, keepdims=True)
    hn = h1 * lax.rsqrt(ms + RMS_EPS)
    u_kv = (hn * gkv_ref[...]).astype(BF16)
    u_b = (hn * gb_ref[...]).astype(BF16)

    def pack_heads(ref, a, b2):
        for h in range(n_heads):
            sl = slice(h * HEAD_DIM, (h + 1) * HEAD_DIM)
            ref[..., h * 2 * HEAD_DIM:h * 2 * HEAD_DIM + HEAD_DIM, :] = a[sl]
            ref[..., h * 2 * HEAD_DIM + HEAD_DIM:(h + 1) * 2 * HEAD_DIM, :] = b2[sl]

    kv = lax.dot_general(wkv_ref[...], u_kv, _NT, preferred_element_type=F32)
    k1 = _head_norm(kv[0:half], gk_ref[0], n_heads, 1.0).astype(BF16)
    k2 = _head_norm(kv[half:2 * half], gk_ref[1], n_heads, 1.0).astype(BF16)
    pack_heads(k_ref.at[0, 0], k1, k2)
    v_ref[0, 0] = kv[2 * half:].astype(BF16)

    qg = lax.dot_general(wb_ref[...], u_b, _NT, preferred_element_type=F32)
    q1 = _head_norm(qg[0:half], gq_ref[0], n_heads, QK_SCALE).astype(BF16)
    q2 = _head_norm(qg[half:2 * half], gq_ref[1], n_heads, QK_SCALE).astype(BF16)
    pack_heads(q_ref.at[0], q1, q2)
    g_ref[0] = qg[2 * half:].astype(BF16)


def _mid(o_t, x, w_out, kv_norm, kv_w, kv_k_norm, b_norm, b_w_in, b_q_norm):
    b, s, d = x.shape
    tm = ROW_TILE
    n_heads = kv_w.shape[1] // (4 * HEAD_DIM)
    width = 2 * n_heads * HEAD_DIM
    const2 = lambda *_: (0, 0)
    const3 = lambda *_: (0, 0, 0)
    fm = pl.BlockSpec((1, width, tm), lambda bi, i: (bi, 0, i))
    tok = pl.BlockSpec((1, tm, d), lambda bi, i: (bi, i, 0))
    chunked = pl.BlockSpec((1, 1, width, tm), lambda bi, i: (bi, i, 0, 0))
    return pl.pallas_call(
        functools.partial(_mid_kernel, n_heads=n_heads),
        grid=(b, s // tm),
        in_specs=[
            pl.BlockSpec((1, o_t.shape[1], tm), lambda bi, i: (bi, 0, i)),
            tok,
            pl.BlockSpec(w_out.shape, const2),
            pl.BlockSpec((1, d), const2),
            pl.BlockSpec((1, d), const2),
            pl.BlockSpec((2 * width, d), const2),
            pl.BlockSpec((2 * width, d), const2),
            pl.BlockSpec((2, HEAD_DIM, 1), const3),
            pl.BlockSpec((2, HEAD_DIM, 1), const3),
        ],
        out_specs=[tok, chunked, chunked, fm, fm],
        out_shape=[
            jax.ShapeDtypeStruct((b, s, d), F32),
            jax.ShapeDtypeStruct((b, s // tm, width, tm), BF16),
            jax.ShapeDtypeStruct((b, s // tm, width, tm), BF16),
            jax.ShapeDtypeStruct((b, width, s), BF16),
            jax.ShapeDtypeStruct((b, width, s), BF16),
        ],
        compiler_params=_params(2),
        name="mid_proj",
    )(o_t, x, w_out.astype(BF16), kv_norm.reshape(1, d), b_norm.reshape(1, d),
      kv_w.T.astype(BF16), b_w_in.T.astype(BF16),
      kv_k_norm.reshape(2, HEAD_DIM, 1), b_q_norm.reshape(2, HEAD_DIM, 1))


def _diff_attn_kernel(q_ref, k_ref, v_ref, brow_ref, lam_ref, gs_ref, g_ref, o_ref,
                      ktok_ref, bias_ref, m_ref, l_ref, acc_ref, *, n_chunks):
    qi = pl.program_id(2)
    t = q_ref.shape[-1]

    @pl.when(qi == 0)
    def _():
        def prep(j, carry):
            ktok_ref[j] = k_ref[0, j].astype(F32).T.astype(BF16)
            return carry
        lax.fori_loop(0, n_chunks, prep, 0)
        for mp in range(2):
            for off in range(2):
                rows = jnp.broadcast_to(brow_ref[0, mp, off], (t, 2 * t))
                bias_ref[mp, off] = pltpu.roll(rows, 0, 1, stride=1, stride_axis=0)[:, 0:t]

    q = q_ref[0]
    row = lax.broadcasted_iota(jnp.int32, q.shape, 0)
    zero = jnp.zeros_like(q)
    qa = (jnp.where(row < HEAD_DIM, q, zero), jnp.where(row >= HEAD_DIM, q, zero))

    m_ref[...] = jnp.full(m_ref.shape, NEG_INF, F32)
    l_ref[...] = jnp.zeros_like(l_ref)
    acc_ref[...] = jnp.zeros_like(acc_ref)

    def scores(mp, j):
        return jnp.dot(ktok_ref[j], qa[mp], preferred_element_type=F32)

    def body(j, carry):
        for mp in range(2):
            _online_update(scores(mp, j), v_ref[0, j], m_ref, l_ref, acc_ref, mp)
        return carry
    lax.fori_loop(0, jnp.maximum(qi - 1, 0), body, 0)

    @pl.when(qi > 0)
    def _():
        for mp in range(2):
            s = scores(mp, qi - 1) + bias_ref[mp, 1]
            _online_update(s, v_ref[0, qi - 1], m_ref, l_ref, acc_ref, mp)

    kchunk = lax.broadcasted_iota(jnp.int32, (t, t), 0) // CHUNK
    qchunk = lax.broadcasted_iota(jnp.int32, (t, t), 1) // CHUNK
    visible = kchunk <= qchunk
    for mp in range(2):
        s = jnp.where(visible, scores(mp, qi) + bias_ref[mp, 0], NEG_INF)
        _online_update(s, v_ref[0, qi], m_ref, l_ref, acc_ref, mp)

    lam_p = lam_ref[...]
    e1 = jnp.exp(jnp.sum(lam_p[0:1] * lam_p[1:2], axis=1, keepdims=True))
    e2 = jnp.exp(jnp.sum(lam_p[2:3] * lam_p[3:4], axis=1, keepdims=True))
    lam = e1 - e2 + LAMBDA_INIT
    o = acc_ref[0] * (1.0 / l_ref[0]) - lam * (acc_ref[1] * (1.0 / l_ref[1]))
    ms = jnp.mean(o * o, axis=0, keepdims=True)
    y = o * lax.rsqrt(ms + RMS_EPS) * gs_ref[...] * (1.0 - LAMBDA_INIT)
    o_ref[0] = (y * _silu(g_ref[0].astype(F32))).astype(BF16)


def _t5_bucket(rel):
    half = NUM_BUCKETS // 2
    max_exact = half // 2
    ret = jnp.where(rel > 0, half, 0)
    n = jnp.abs(rel)
    n_f = jnp.maximum(n, 1).astype(jnp.float32)
    large = max_exact + (jnp.log(n_f / max_exact) / math.log(MAX_DISTANCE / max_exact)
                         * (half - max_exact)).astype(jnp.int32)
    large = jnp.minimum(large, half - 1)
    return ret + jnp.where(n < max_exact, n, large)


def _bias_rows(rel_bias, t):
    dw = jnp.arange(2 * t)
    d = jnp.where(dw < t, dw, dw - 2 * t)
    rel = jnp.stack([-d, -d - t])
    rows = rel_bias[_t5_bucket(rel)].astype(F32)
    far = rel_bias[_t5_bucket(jnp.int32(-MAX_DISTANCE))].astype(F32)
    return (rows - far).transpose(3, 2, 0, 1)[:, :, :, None, :]


def _diff_attn(q_t, k_c, v_c, g_t, rel_bias, lam_params, g_sub):
    b, width, s = q_t.shape
    t = ATTN_TILE
    n_heads = width // DIFF_V_DIM
    n_chunks = s // t
    tile = pl.BlockSpec((1, DIFF_V_DIM, t), lambda bi, h, qi: (bi, h, qi))
    full = pl.BlockSpec((1, n_chunks, DIFF_V_DIM, t), lambda bi, h, qi: (bi, 0, h, 0))
    return pl.pallas_call(
        functools.partial(_diff_attn_kernel, n_chunks=n_chunks),
        grid=(b, n_heads, n_chunks),
        in_specs=[
            tile, full, full,
            pl.BlockSpec((1, 2, 2, 1, 2 * t), lambda bi, h, qi: (h, 0, 0, 0, 0)),
            pl.BlockSpec((4, HEAD_DIM), lambda *_: (0, 0)),
            pl.BlockSpec((DIFF_V_DIM, 1), lambda *_: (0, 0)),
            tile,
        ],
        out_specs=tile,
        out_shape=jax.ShapeDtypeStruct((b, width, s), BF16),
        scratch_shapes=[
            pltpu.VMEM((n_chunks, t, DIFF_V_DIM), BF16),
            pltpu.VMEM((2, 2, t, t), F32),
            pltpu.VMEM((2, 1, t), F32),
            pltpu.VMEM((2, 1, t), F32),
            pltpu.VMEM((2, DIFF_V_DIM, t), F32),
        ],
        compiler_params=_params(3),
        name="diff_attn",
    )(q_t, k_c, v_c, _bias_rows(rel_bias, t), lam_params, g_sub.reshape(DIFF_V_DIM, 1), g_t)


def _out_kernel(o_ref, h_ref, w_ref, y_ref):
    y_ref[0] = h_ref[0] + lax.dot_general(o_ref[0], w_ref[...], _TN, preferred_element_type=F32)


def _out_proj(o_t, h1, w_out):
    b, s, d = h1.shape
    tm = ROW_TILE
    tok = pl.BlockSpec((1, tm, d), lambda bi, i: (bi, i, 0))
    return pl.pallas_call(
        _out_kernel,
        grid=(b, s // tm),
        in_specs=[
            pl.BlockSpec((1, o_t.shape[1], tm), lambda bi, i: (bi, 0, i)),
            tok,
            pl.BlockSpec(w_out.shape, lambda *_: (0, 0)),
        ],
        out_specs=tok,
        out_shape=jax.ShapeDtypeStruct((b, s, d), F32),
        compiler_params=_params(2),
        name="out_proj",
    )(o_t, h1, w_out.astype(BF16))


def kernel(x, a_norm, a_w_in, a_b_f, a_q_norm, a_k_norm, a_w_out, kv_norm, kv_w, kv_k_norm,
           rel_bias, b_norm, b_w_in, b_q_norm, b_lam_q1, b_lam_k1, b_lam_q2, b_lam_k2,
           b_sub_norm, b_w_out):
    assert a_norm.shape[0] == 1 and b_norm.shape[0] == 1
    assert x.shape[1] % ATTN_TILE == 0 and ATTN_TILE == ROW_TILE
    q_t, k_c, v_c, g_t, c_c = _fox_proj(x, a_norm[0], a_w_in[0], a_b_f[0], a_q_norm[0], a_k_norm[0])
    o_t = _fox_attn(q_t, k_c, v_c, c_c, g_t)
    h1, k2_c, v2_c, q2_t, g2_t = _mid(o_t, x, a_w_out[0], kv_norm, kv_w, kv_k_norm,
                                      b_norm[0], b_w_in[0], b_q_norm[0])
    lam_params = jnp.stack([b_lam_q1[0], b_lam_k1[0], b_lam_q2[0], b_lam_k2[0]])
    o2_t = _diff_attn(q2_t, k2_c, v2_c, g2_t, rel_bias, lam_params, b_sub_norm[0])
    return _out_proj(o2_t, h1, b_w_out[0])
```

```python
import functools
import math

import jax
import jax.numpy as jnp
from jax import lax
from jax.experimental import pallas as pl
from jax.experimental.pallas import tpu as pltpu

HEAD_DIM = 64
DIFF_V_DIM = 2 * HEAD_DIM
CHUNK = 64
NUM_BUCKETS = 32
MAX_DISTANCE = 128
RMS_EPS = 1e-6
NEG_INF = -1e30
LOG2E = math.log2(math.e)
QK_SCALE = HEAD_DIM ** -0.5 * LOG2E
LAMBDA_INIT = 0.8 - 0.6 * math.exp(-0.3 * 1)

ROW_TILE = 512
ATTN_TILE = 512
FOX_HEADS_PER_STEP = 2
ONES_ROWS = 16
V7X_VMEM_LIMIT_BYTES = 56 * 1024 * 1024

F32 = jnp.float32
BF16 = jnp.bfloat16
_NT = (((1,), (1,)), ((), ()))
_TN = (((0,), (0,)), ((), ()))


def _params(n_axes):
    return pltpu.CompilerParams(
        dimension_semantics=("arbitrary",) * n_axes,
        vmem_limit_bytes=V7X_VMEM_LIMIT_BYTES)


def _head_norm(p_t, g_col, n_heads, scale):
    t = p_t.shape[-1]
    p3 = p_t.reshape(n_heads, HEAD_DIM, t)
    ms = jnp.mean(p3 * p3, axis=1, keepdims=True)
    y = p3 * lax.rsqrt(ms + RMS_EPS) * (g_col * scale)[None]
    return y.reshape(n_heads * HEAD_DIM, t)


def _split3(c):
    hi = c.astype(BF16).astype(F32)
    r = c - hi
    mid = r.astype(BF16).astype(F32)
    lo = (r - mid).astype(BF16).astype(F32)
    return hi, mid, lo


def _silu(g):
    return g / (1.0 + jnp.exp(-g))


def _fox_proj_kernel(x_ref, gn_ref, wq_ref, wk_ref, wv_ref, wg_ref, wf_ref, bf_ref,
                     gq_ref, gk_ref, q_ref, k_ref, v_ref, g_ref, c_ref, carry_ref,
                     *, n_heads):
    @pl.when(pl.program_id(1) == 0)
    def _():
        carry_ref[...] = jnp.zeros_like(carry_ref)

    x = x_ref[0]
    tm = x.shape[0]
    ms = jnp.mean(x * x, axis=-1, keepdims=True)
    u = (x * lax.rsqrt(ms + RMS_EPS) * gn_ref[...]).astype(BF16)

    def proj(w_ref):
        return lax.dot_general(w_ref[...], u, _NT, preferred_element_type=F32)

    q_ref[0] = _head_norm(proj(wq_ref), gq_ref[...], n_heads, QK_SCALE).astype(BF16)
    k_ref[0, 0] = _head_norm(proj(wk_ref), gk_ref[...], n_heads, 1.0).astype(BF16)
    v_ref[0, 0] = proj(wv_ref).astype(BF16)
    g_ref[0] = proj(wg_ref).astype(BF16)

    z = proj(wf_ref) + bf_ref[...]
    log_f = jnp.minimum(z, 0.0) - jnp.log1p(jnp.exp(-jnp.abs(z)))
    lane = lax.broadcasted_iota(jnp.int32, log_f.shape, 1)
    cs = log_f
    shift = 1
    while shift < tm:
        cs = cs + jnp.where(lane >= shift, pltpu.roll(cs, shift, axis=1), 0.0)
        shift *= 2
    c = cs + carry_ref[:, 0:1]
    c_ref[0, 0] = c * LOG2E
    carry_ref[...] = jnp.broadcast_to(c[:, tm - 1:tm], carry_ref.shape)


def _fox_proj(x, a_norm, w_in, b_f, g_q, g_k):
    b, s, d = x.shape
    n_heads = b_f.shape[0]
    width = n_heads * HEAD_DIM
    tm = ROW_TILE
    w_t = w_in.T.astype(BF16)
    wq, wk, wv, wg, wf = (w_t[0:width], w_t[width:2 * width], w_t[2 * width:3 * width],
                          w_t[3 * width:4 * width], w_t[4 * width:])
    const = lambda *_: (0, 0)
    w_spec = pl.BlockSpec((width, d), const)
    col = pl.BlockSpec((HEAD_DIM, 1), const)
    return pl.pallas_call(
        functools.partial(_fox_proj_kernel, n_heads=n_heads),
        grid=(b, s // tm),
        in_specs=[
            pl.BlockSpec((1, tm, d), lambda bi, i: (bi, i, 0)),
            pl.BlockSpec((1, d), const),
            w_spec, w_spec, w_spec, w_spec,
            pl.BlockSpec((n_heads, d), const),
            pl.BlockSpec((n_heads, 1), const),
            col, col,
        ],
        out_specs=[
            pl.BlockSpec((1, width, tm), lambda bi, i: (bi, 0, i)),
            pl.BlockSpec((1, 1, width, tm), lambda bi, i: (bi, i, 0, 0)),
            pl.BlockSpec((1, 1, width, tm), lambda bi, i: (bi, i, 0, 0)),
            pl.BlockSpec((1, width, tm), lambda bi, i: (bi, 0, i)),
            pl.BlockSpec((1, 1, n_heads, tm), lambda bi, i: (bi, i, 0, 0)),
        ],
        out_shape=[
            jax.ShapeDtypeStruct((b, width, s), BF16),
            jax.ShapeDtypeStruct((b, s // tm, width, tm), BF16),
            jax.ShapeDtypeStruct((b, s // tm, width, tm), BF16),
            jax.ShapeDtypeStruct((b, width, s), BF16),
            jax.ShapeDtypeStruct((b, s // tm, n_heads, tm), F32),
        ],
        scratch_shapes=[pltpu.VMEM((n_heads, 128), F32)],
        compiler_params=_params(2),
        name="fox_proj",
    )(x, a_norm.reshape(1, d), wq, wk, wv, wg, wf, b_f.reshape(n_heads, 1),
      g_q.reshape(HEAD_DIM, 1), g_k.reshape(HEAD_DIM, 1))


def _store_scores(s, s_ref, mx_ref, idx):
    s_ref[idx] = s
    mx_ref[idx] = jnp.max(s, axis=0, keepdims=True)


def _softmax_step(s_ref, mx_ref, m_ref, alpha_ref, p_ref, idx):
    m_prev = m_ref[idx]
    m_new = jnp.maximum(m_prev, mx_ref[idx])
    p_ref[idx] = jnp.exp2(s_ref[idx] - m_new).astype(BF16)
    alpha_ref[idx] = jnp.exp2(m_prev - m_new)
    m_ref[idx] = m_new


def _with_ones_rows(v):
    return jnp.concatenate([v, jnp.ones((ONES_ROWS, v.shape[1]), v.dtype)], axis=0)


def _run_pipeline(qi, tail, produce, consume, pv):
    n_tail = len(tail)

    def step(j, cur, next_kind):
        pv(jnp.maximum(j - 1, 0), 1 - cur)
        if next_kind is not None:
            produce(j + 1, 1 - cur, next_kind)
        consume(j, cur)

    def finish(first_tile, first_buf, kinds):
        j, buf = first_tile, first_buf
        for kind in kinds:
            step(j, buf, kind)
            j, buf = j + 1, 1 - buf
        step(j, buf, None)
        pv(j, buf)

    for small in range(n_tail):
        @pl.when(qi == small)
        def _(small=small):
            kinds = tail[n_tail - 1 - small:]
            produce(0, 0, kinds[0])
            finish(0, 0, kinds[1:])

    n_loop = qi - n_tail

    @pl.when(qi >= n_tail)
    def _():
        produce(0, 0, "far")

    def pair(i, carry):
        step(2 * i, 0, "far")
        step(2 * i + 1, 1, "far")
        return carry
    lax.fori_loop(0, jnp.maximum(n_loop, 0) // 2, pair, 0)

    @pl.when(jnp.logical_and(qi >= n_tail, n_loop % 2 == 0))
    def _():
        finish(n_loop, 0, tail)

    @pl.when(jnp.logical_and(qi >= n_tail, n_loop % 2 == 1))
    def _():
        step(n_loop - 1, 0, "far")
        finish(n_loop, 1, tail)


def _pv_step(v, p_ref, alpha_ref, acc_ref, idx):
    acc_ref[idx] = alpha_ref[idx] * acc_ref[idx] + jnp.dot(
        v, p_ref[idx], preferred_element_type=F32)


def _init_softmax_state(m_ref, alpha_ref, acc_ref, p_prev_ref):
    m_ref[...] = jnp.full(m_ref.shape, NEG_INF, F32)
    alpha_ref[...] = jnp.ones_like(alpha_ref)
    acc_ref[...] = jnp.zeros_like(acc_ref)
    p_prev_ref[...] = jnp.zeros_like(p_prev_ref)


def _aug_rows(rows8, shape):
    row = lax.broadcasted_iota(jnp.int32, shape, 0)
    out = jnp.zeros(shape, F32)
    for r, val in enumerate(rows8):
        out = jnp.where(row == r, val, out)
    return out


def _fox_attn_kernel(q_ref, k_ref, v_ref, c_ref, g_ref, o_ref,
                     kaug_ref, vaug_ref, m_ref, alpha_ref, mxa_ref, mxb_ref, acc_ref,
                     sa_ref, sb_ref, pa_ref, pb_ref, *, hp, n_chunks):
    qi = pl.program_id(2)
    t = q_ref.shape[-1]
    pad = jnp.zeros((2 * HEAD_DIM - HEAD_DIM - 8, t), F32)

    @pl.when(qi == 0)
    def _():
        def prep(j, carry):
            for h in range(hp):
                kt = k_ref[0, j, h * HEAD_DIM:(h + 1) * HEAD_DIM, :].astype(F32)
                hi, mid, lo = _split3(c_ref[0, j, 0, h:h + 1, :])
                aug = _aug_rows([-hi, -mid, -lo, 1.0, 1.0, 1.0], (8, t))
                kaug_ref[h, j] = jnp.concatenate([kt, aug, pad], axis=0).T.astype(BF16)
                vaug_ref[h, j] = _with_ones_rows(v_ref[0, j, h * HEAD_DIM:(h + 1) * HEAD_DIM, :])
            return carry
        lax.fori_loop(0, n_chunks, prep, 0)

    qa = []
    for h in range(hp):
        qf = q_ref[0, h * HEAD_DIM:(h + 1) * HEAD_DIM, :].astype(F32)
        hi, mid, lo = _split3(c_ref[0, qi, 0, h:h + 1, :])
        aug = _aug_rows([1.0, 1.0, 1.0, hi, mid, lo], (8, t))
        qa.append(jnp.concatenate([qf, aug, pad], axis=0).astype(BF16))

    bufs = ((sa_ref, pa_ref, mxa_ref), (sb_ref, pb_ref, mxb_ref))
    _init_softmax_state(m_ref, alpha_ref, acc_ref, pb_ref)

    def produce(j, buf, kind):
        for h in range(hp):
            s = jnp.dot(kaug_ref[h, j], qa[h], preferred_element_type=F32)
            if kind == "diag":
                kpos = lax.broadcasted_iota(jnp.int32, (t, t), 0)
                qpos = lax.broadcasted_iota(jnp.int32, (t, t), 1)
                s = jnp.where(kpos <= qpos, s, NEG_INF)
            _store_scores(s, bufs[buf][0], bufs[buf][2], h)

    def consume(j, buf):
        for h in range(hp):
            _softmax_step(bufs[buf][0], bufs[buf][2], m_ref, alpha_ref, bufs[buf][1], h)

    def pv(j, buf):
        for h in range(hp):
            _pv_step(vaug_ref[h, j], bufs[buf][1], alpha_ref, acc_ref, h)

    _run_pipeline(qi, ("diag",), produce, consume, pv)

    for h in range(hp):
        o = acc_ref[h, 0:HEAD_DIM] * (1.0 / acc_ref[h, HEAD_DIM:HEAD_DIM + 1])
        g = g_ref[0, h * HEAD_DIM:(h + 1) * HEAD_DIM, :].astype(F32)
        o_ref[0, h * HEAD_DIM:(h + 1) * HEAD_DIM, :] = (o * _silu(g)).astype(BF16)


def _fox_attn(q_t, k_c, v_c, c_c, g_t):
    b, width, s = q_t.shape
    t = ATTN_TILE
    hp = FOX_HEADS_PER_STEP
    n_heads = width // HEAD_DIM
    n_chunks = s // t
    rows = hp * HEAD_DIM
    c_c = c_c.reshape(b, n_chunks, n_heads // hp, hp, t)
    tile = pl.BlockSpec((1, rows, t), lambda bi, hg, qi: (bi, hg, qi))
    full = pl.BlockSpec((1, n_chunks, rows, t), lambda bi, hg, qi: (bi, 0, hg, 0))
    return pl.pallas_call(
        functools.partial(_fox_attn_kernel, hp=hp, n_chunks=n_chunks),
        grid=(b, n_heads // hp, n_chunks),
        in_specs=[
            tile, full, full,
            pl.BlockSpec((1, n_chunks, 1, hp, t), lambda bi, hg, qi: (bi, 0, hg, 0, 0)),
            tile,
        ],
        out_specs=tile,
        out_shape=jax.ShapeDtypeStruct((b, width, s), BF16),
        scratch_shapes=[
            pltpu.VMEM((hp, n_chunks, t, 2 * HEAD_DIM), BF16),
            pltpu.VMEM((hp, n_chunks, HEAD_DIM + ONES_ROWS, t), BF16),
            pltpu.VMEM((hp, 1, t), F32),
            pltpu.VMEM((hp, 1, t), F32),
            pltpu.VMEM((hp, 1, t), F32),
            pltpu.VMEM((hp, 1, t), F32),
            pltpu.VMEM((hp, HEAD_DIM + ONES_ROWS, t), F32),
            pltpu.VMEM((hp, t, t), F32),
            pltpu.VMEM((hp, t, t), F32),
            pltpu.VMEM((hp, t, t), BF16),
            pltpu.VMEM((hp, t, t), BF16),
        ],
        compiler_params=_params(3),
        name="fox_attn",
    )(q_t, k_c, v_c, c_c, g_t)


def _mid_kernel(o_ref, x_ref, wo_ref, gkv_ref, gb_ref, wkv_ref, wb_ref, gk_ref, gq_ref,
                h_ref, k_ref, v_ref, q_ref, g_ref, *, n_heads):
    half = n_heads * HEAD_DIM
    h1 = x_ref[0] + lax.dot_general(o_ref[0], wo_ref[...], _TN, preferred_element_type=F32)
    h_ref[0] = h1
    ms = jnp.mean(h1 * h1, axis=-1, keepdims=True)
    hn = h1 * lax.rsqrt(ms + RMS_EPS)
    u_kv = (hn * gkv_ref[...]).astype(BF16)
    u_b = (hn * gb_ref[...]).astype(BF16)

    def pack_heads(ref, a, b2):
        for h in range(n_heads):
            sl = slice(h * HEAD_DIM, (h + 1) * HEAD_DIM)
            ref[..., h * 2 * HEAD_DIM:h * 2 * HEAD_DIM + HEAD_DIM, :] = a[sl]
            ref[..., h * 2 * HEAD_DIM + HEAD_DIM:(h + 1) * 2 * HEAD_DIM, :] = b2[sl]

    kv = lax.dot_general(wkv_ref[...], u_kv, _NT, preferred_element_type=F32)
    k1 = _head_norm(kv[0:half], gk_ref[0], n_heads, 1.0).astype(BF16)
    k2 = _head_norm(kv[half:2 * half], gk_ref[1], n_heads, 1.0).astype(BF16)
    pack_heads(k_ref.at[0, 0], k1, k2)
    v_ref[0, 0] = kv[2 * half:].astype(BF16)

    qg = lax.dot_general(wb_ref[...], u_b, _NT, preferred_element_type=F32)
    q1 = _head_norm(qg[0:half], gq_ref[0], n_heads, QK_SCALE).astype(BF16)
    q2 = _head_norm(qg[half:2 * half], gq_ref[1], n_heads, QK_SCALE).astype(BF16)
    pack_heads(q_ref.at[0], q1, q2)
    g_ref[0] = qg[2 * half:].astype(BF16)


def _mid(o_t, x, w_out, kv_norm, kv_w, kv_k_norm, b_norm, b_w_in, b_q_norm):
    b, s, d = x.shape
    tm = ROW_TILE
    n_heads = kv_w.shape[1] // (4 * HEAD_DIM)
    width = 2 * n_heads * HEAD_DIM
    const2 = lambda *_: (0, 0)
    const3 = lambda *_: (0, 0, 0)
    fm = pl.BlockSpec((1, width, tm), lambda bi, i: (bi, 0, i))
    tok = pl.BlockSpec((1, tm, d), lambda bi, i: (bi, i, 0))
    chunked = pl.BlockSpec((1, 1, width, tm), lambda bi, i: (bi, i, 0, 0))
    return pl.pallas_call(
        functools.partial(_mid_kernel, n_heads=n_heads),
        grid=(b, s // tm),
        in_specs=[
            pl.BlockSpec((1, o_t.shape[1], tm), lambda bi, i: (bi, 0, i)),
            tok,
            pl.BlockSpec(w_out.shape, const2),
            pl.BlockSpec((1, d), const2),
            pl.BlockSpec((1, d), const2),
            pl.BlockSpec((2 * width, d), const2),
            pl.BlockSpec((2 * width, d), const2),
            pl.BlockSpec((2, HEAD_DIM, 1), const3),
            pl.BlockSpec((2, HEAD_DIM, 1), const3),
        ],
        out_specs=[tok, chunked, chunked, fm, fm],
        out_shape=[
            jax.ShapeDtypeStruct((b, s, d), F32),
            jax.ShapeDtypeStruct((b, s // tm, width, tm), BF16),
            jax.ShapeDtypeStruct((b, s // tm, width, tm), BF16),
            jax.ShapeDtypeStruct((b, width, s), BF16),
            jax.ShapeDtypeStruct((b, width, s), BF16),
        ],
        compiler_params=_params(2),
        name="mid_proj",
    )(o_t, x, w_out.astype(BF16), kv_norm.reshape(1, d), b_norm.reshape(1, d),
      kv_w.T.astype(BF16), b_w_in.T.astype(BF16),
      kv_k_norm.reshape(2, HEAD_DIM, 1), b_q_norm.reshape(2, HEAD_DIM, 1))


def _diff_attn_kernel(q_ref, k_ref, v_ref, brow_ref, lam_ref, gs_ref, g_ref, o_ref,
                      ktok_ref, vaug_ref, bias_ref, m_ref, alpha_ref, mxa_ref, mxb_ref, acc_ref,
                      sa_ref, sb_ref, pa_ref, pb_ref, *, n_chunks):
    qi = pl.program_id(2)
    t = q_ref.shape[-1]

    @pl.when(qi == 0)
    def _():
        def prep(j, carry):
            ktok_ref[j] = k_ref[0, j].astype(F32).T.astype(BF16)
            vaug_ref[j] = _with_ones_rows(v_ref[0, j])
            return carry
        lax.fori_loop(0, n_chunks, prep, 0)
        for mp in range(2):
            for off in range(2):
                rows = jnp.broadcast_to(brow_ref[0, mp, off], (t, 2 * t))
                bias_ref[mp, off] = pltpu.roll(rows, 0, 1, stride=1, stride_axis=0)[:, 0:t]

    q = q_ref[0]
    row = lax.broadcasted_iota(jnp.int32, q.shape, 0)
    zero = jnp.zeros_like(q)
    qa = (jnp.where(row < HEAD_DIM, q, zero), jnp.where(row >= HEAD_DIM, q, zero))

    bufs = ((sa_ref, pa_ref, mxa_ref), (sb_ref, pb_ref, mxb_ref))
    _init_softmax_state(m_ref, alpha_ref, acc_ref, pb_ref)

    def produce(j, buf, kind):
        for mp in range(2):
            s = jnp.dot(ktok_ref[j], qa[mp], preferred_element_type=F32)
            if kind == "near":
                s = s + bias_ref[mp, 1]
            elif kind == "diag":
                kchunk = lax.broadcasted_iota(jnp.int32, (t, t), 0) // CHUNK
                qchunk = lax.broadcasted_iota(jnp.int32, (t, t), 1) // CHUNK
                s = jnp.where(kchunk <= qchunk, s + bias_ref[mp, 0], NEG_INF)
            _store_scores(s, bufs[buf][0], bufs[buf][2], mp)

    def consume(j, buf):
        for mp in range(2):
            _softmax_step(bufs[buf][0], bufs[buf][2], m_ref, alpha_ref, bufs[buf][1], mp)

    def pv(j, buf):
        for mp in range(2):
            _pv_step(vaug_ref[j], bufs[buf][1], alpha_ref, acc_ref, mp)

    _run_pipeline(qi, ("near", "diag"), produce, consume, pv)

    lam_p = lam_ref[...]
    e1 = jnp.exp(jnp.sum(lam_p[0:1] * lam_p[1:2], axis=1, keepdims=True))
    e2 = jnp.exp(jnp.sum(lam_p[2:3] * lam_p[3:4], axis=1, keepdims=True))
    lam = e1 - e2 + LAMBDA_INIT
    dv = DIFF_V_DIM
    o = (acc_ref[0, 0:dv] * (1.0 / acc_ref[0, dv:dv + 1])
         - lam * (acc_ref[1, 0:dv] * (1.0 / acc_ref[1, dv:dv + 1])))
    ms = jnp.mean(o * o, axis=0, keepdims=True)
    y = o * lax.rsqrt(ms + RMS_EPS) * gs_ref[...] * (1.0 - LAMBDA_INIT)
    o_ref[0] = (y * _silu(g_ref[0].astype(F32))).astype(BF16)


def _t5_bucket(rel):
    half = NUM_BUCKETS // 2
    max_exact = half // 2
    ret = jnp.where(rel > 0, half, 0)
    n = jnp.abs(rel)
    n_f = jnp.maximum(n, 1).astype(jnp.float32)
    large = max_exact + (jnp.log(n_f / max_exact) / math.log(MAX_DISTANCE / max_exact)
                         * (half - max_exact)).astype(jnp.int32)
    large = jnp.minimum(large, half - 1)
    return ret + jnp.where(n < max_exact, n, large)


def _bias_rows(rel_bias, t):
    dw = jnp.arange(2 * t)
    d = jnp.where(dw < t, dw, dw - 2 * t)
    rel = jnp.stack([-d, -d - t])
    rows = rel_bias[_t5_bucket(rel)].astype(F32)
    far = rel_bias[_t5_bucket(jnp.int32(-MAX_DISTANCE))].astype(F32)
    return ((rows - far) * LOG2E).transpose(3, 2, 0, 1)[:, :, :, None, :]


def _diff_attn(q_t, k_c, v_c, g_t, rel_bias, lam_params, g_sub):
    b, width, s = q_t.shape
    t = ATTN_TILE
    n_heads = width // DIFF_V_DIM
    n_chunks = s // t
    tile = pl.BlockSpec((1, DIFF_V_DIM, t), lambda bi, h, qi: (bi, h, qi))
    full = pl.BlockSpec((1, n_chunks, DIFF_V_DIM, t), lambda bi, h, qi: (bi, 0, h, 0))
    return pl.pallas_call(
        functools.partial(_diff_attn_kernel, n_chunks=n_chunks),
        grid=(b, n_heads, n_chunks),
        in_specs=[
            tile, full, full,
            pl.BlockSpec((1, 2, 2, 1, 2 * t), lambda bi, h, qi: (h, 0, 0, 0, 0)),
            pl.BlockSpec((4, HEAD_DIM), lambda *_: (0, 0)),
            pl.BlockSpec((DIFF_V_DIM, 1), lambda *_: (0, 0)),
            tile,
        ],
        out_specs=tile,
        out_shape=jax.ShapeDtypeStruct((b, width, s), BF16),
        scratch_shapes=[
            pltpu.VMEM((n_chunks, t, DIFF_V_DIM), BF16),
            pltpu.VMEM((n_chunks, DIFF_V_DIM + ONES_ROWS, t), BF16),
            pltpu.VMEM((2, 2, t, t), F32),
            pltpu.VMEM((2, 1, t), F32),
            pltpu.VMEM((2, 1, t), F32),
            pltpu.VMEM((2, 1, t), F32),
            pltpu.VMEM((2, 1, t), F32),
            pltpu.VMEM((2, DIFF_V_DIM + ONES_ROWS, t), F32),
            pltpu.VMEM((2, t, t), F32),
            pltpu.VMEM((2, t, t), F32),
            pltpu.VMEM((2, t, t), BF16),
            pltpu.VMEM((2, t, t), BF16),
        ],
        compiler_params=_params(3),
        name="diff_attn",
    )(q_t, k_c, v_c, _bias_rows(rel_bias, t), lam_params, g_sub.reshape(DIFF_V_DIM, 1), g_t)


def _out_kernel(o_ref, h_ref, w_ref, y_ref):
    y_ref[0] = h_ref[0] + lax.dot_general(o_ref[0], w_ref[...], _TN, preferred_element_type=F32)


def _out_proj(o_t, h1, w_out):
    b, s, d = h1.shape
    tm = ROW_TILE
    tok = pl.BlockSpec((1, tm, d), lambda bi, i: (bi, i, 0))
    return pl.pallas_call(
        _out_kernel,
        grid=(b, s // tm),
        in_specs=[
            pl.BlockSpec((1, o_t.shape[1], tm), lambda bi, i: (bi, 0, i)),
            tok,
            pl.BlockSpec(w_out.shape, lambda *_: (0, 0)),
        ],
        out_specs=tok,
        out_shape=jax.ShapeDtypeStruct((b, s, d), F32),
        compiler_params=_params(2),
        name="out_proj",
    )(o_t, h1, w_out.astype(BF16))


def kernel(x, a_norm, a_w_in, a_b_f, a_q_norm, a_k_norm, a_w_out, kv_norm, kv_w, kv_k_norm,
           rel_bias, b_norm, b_w_in, b_q_norm, b_lam_q1, b_lam_k1, b_lam_q2, b_lam_k2,
           b_sub_norm, b_w_out):
    assert a_norm.shape[0] == 1 and b_norm.shape[0] == 1
    assert x.shape[1] % ATTN_TILE == 0 and ATTN_TILE == ROW_TILE
    q_t, k_c, v_c, g_t, c_c = _fox_proj(x, a_norm[0], a_w_in[0], a_b_f[0], a_q_norm[0], a_k_norm[0])
    o_t = _fox_attn(q_t, k_c, v_c, c_c, g_t)
    h1, k2_c, v2_c, q2_t, g2_t = _mid(o_t, x, a_w_out[0], kv_norm, kv_w, kv_k_norm,
                                      b_norm[0], b_w_in[0], b_q_norm[0])
    lam_params = jnp.stack([b_lam_q1[0], b_lam_k1[0], b_lam_q2[0], b_lam_k2[0]])
    o2_t = _diff_attn(q2_t, k2_c, v2_c, g2_t, rel_bias, lam_params, b_sub_norm[0])
    return _out_proj(o2_t, h1, b_w_out[0])
```

```python
import functools
import math

import jax
import jax.numpy as jnp
from jax import lax
from jax.experimental import pallas as pl
from jax.experimental.pallas import tpu as pltpu

HEAD_DIM = 64
DIFF_V_DIM = 2 * HEAD_DIM
CHUNK = 64
NUM_BUCKETS = 32
MAX_DISTANCE = 128
RMS_EPS = 1e-6
NEG_INF = -1e30
LOG2E = math.log2(math.e)
QK_SCALE = HEAD_DIM ** -0.5 * LOG2E
LAMBDA_INIT = 0.8 - 0.6 * math.exp(-0.3 * 1)

ROW_TILE = 512
ATTN_TILE = 512
FOX_HEADS_PER_STEP = 2
DIFF_HEADS_PER_STEP = 1
ONES_ROWS = 16
V7X_VMEM_LIMIT_BYTES = 56 * 1024 * 1024

F32 = jnp.float32
BF16 = jnp.bfloat16
_NT = (((1,), (1,)), ((), ()))
_TN = (((0,), (0,)), ((), ()))


def _params(n_axes, flags=None):
    return pltpu.CompilerParams(
        dimension_semantics=("arbitrary",) * n_axes,
        vmem_limit_bytes=V7X_VMEM_LIMIT_BYTES, flags=flags)


_ATTN_FLAGS = None


def _head_norm(p_t, g_col, n_heads, scale):
    t = p_t.shape[-1]
    p3 = p_t.reshape(n_heads, HEAD_DIM, t)
    ms = jnp.mean(p3 * p3, axis=1, keepdims=True)
    y = p3 * lax.rsqrt(ms + RMS_EPS) * (g_col * scale)[None]
    return y.reshape(n_heads * HEAD_DIM, t)


def _split3(c):
    hi = c.astype(BF16).astype(F32)
    r = c - hi
    mid = r.astype(BF16).astype(F32)
    lo = (r - mid).astype(BF16).astype(F32)
    return hi, mid, lo


def _silu(g):
    return g / (1.0 + jnp.exp(-g))


def _fox_proj_kernel(x_ref, gn_ref, wq_ref, wk_ref, wv_ref, wg_ref, wf_ref, bf_ref,
                     gq_ref, gk_ref, q_ref, k_ref, v_ref, g_ref, c_ref, carry_ref,
                     *, n_heads):
    @pl.when(pl.program_id(1) == 0)
    def _():
        carry_ref[...] = jnp.zeros_like(carry_ref)

    x = x_ref[0]
    tm = x.shape[0]
    ms = jnp.mean(x * x, axis=-1, keepdims=True)
    u = (x * lax.rsqrt(ms + RMS_EPS) * gn_ref[...]).astype(BF16)

    def proj(w_ref):
        return lax.dot_general(w_ref[...], u, _NT, preferred_element_type=F32)

    q_ref[0] = _head_norm(proj(wq_ref), gq_ref[...], n_heads, QK_SCALE).astype(BF16)
    k_ref[0, 0] = _head_norm(proj(wk_ref), gk_ref[...], n_heads, 1.0).astype(BF16)
    v_ref[0, 0] = proj(wv_ref).astype(BF16)
    g_ref[0] = proj(wg_ref).astype(BF16)

    z = proj(wf_ref) + bf_ref[...]
    log_f = jnp.minimum(z, 0.0) - jnp.log1p(jnp.exp(-jnp.abs(z)))
    lane = lax.broadcasted_iota(jnp.int32, log_f.shape, 1)
    cs = log_f
    shift = 1
    while shift < tm:
        cs = cs + jnp.where(lane >= shift, pltpu.roll(cs, shift, axis=1), 0.0)
        shift *= 2
    c = cs + carry_ref[:, 0:1]
    c_ref[0, 0] = c * LOG2E
    carry_ref[...] = jnp.broadcast_to(c[:, tm - 1:tm], carry_ref.shape)


def _fox_proj(x, a_norm, w_in, b_f, g_q, g_k):
    b, s, d = x.shape
    n_heads = b_f.shape[0]
    width = n_heads * HEAD_DIM
    tm = ROW_TILE
    w_t = w_in.T.astype(BF16)
    wq, wk, wv, wg, wf = (w_t[0:width], w_t[width:2 * width], w_t[2 * width:3 * width],
                          w_t[3 * width:4 * width], w_t[4 * width:])
    const = lambda *_: (0, 0)
    w_spec = pl.BlockSpec((width, d), const)
    col = pl.BlockSpec((HEAD_DIM, 1), const)
    return pl.pallas_call(
        functools.partial(_fox_proj_kernel, n_heads=n_heads),
        grid=(b, s // tm),
        in_specs=[
            pl.BlockSpec((1, tm, d), lambda bi, i: (bi, i, 0)),
            pl.BlockSpec((1, d), const),
            w_spec, w_spec, w_spec, w_spec,
            pl.BlockSpec((n_heads, d), const),
            pl.BlockSpec((n_heads, 1), const),
            col, col,
        ],
        out_specs=[
            pl.BlockSpec((1, width, tm), lambda bi, i: (bi, 0, i)),
            pl.BlockSpec((1, 1, width, tm), lambda bi, i: (bi, i, 0, 0)),
            pl.BlockSpec((1, 1, width, tm), lambda bi, i: (bi, i, 0, 0)),
            pl.BlockSpec((1, width, tm), lambda bi, i: (bi, 0, i)),
            pl.BlockSpec((1, 1, n_heads, tm), lambda bi, i: (bi, i, 0, 0)),
        ],
        out_shape=[
            jax.ShapeDtypeStruct((b, width, s), BF16),
            jax.ShapeDtypeStruct((b, s // tm, width, tm), BF16),
            jax.ShapeDtypeStruct((b, s // tm, width, tm), BF16),
            jax.ShapeDtypeStruct((b, width, s), BF16),
            jax.ShapeDtypeStruct((b, s // tm, n_heads, tm), F32),
        ],
        scratch_shapes=[pltpu.VMEM((n_heads, 128), F32)],
        compiler_params=_params(2),
        name="fox_proj",
    )(x, a_norm.reshape(1, d), wq, wk, wv, wg, wf, b_f.reshape(n_heads, 1),
      g_q.reshape(HEAD_DIM, 1), g_k.reshape(HEAD_DIM, 1))


def _store_scores(s, s_ref, mx_ref, idx):
    s_ref[idx] = s
    mx_ref[idx] = jnp.max(s, axis=0, keepdims=True)


def _softmax_step(s_ref, mx_ref, m_ref, alpha_ref, p_ref, idx):
    m_prev = m_ref[idx]
    m_new = jnp.maximum(m_prev, mx_ref[idx])
    p_ref[idx] = jnp.exp2(s_ref[idx] - m_new).astype(BF16)
    alpha_ref[idx] = jnp.exp2(m_prev - m_new)
    m_ref[idx] = m_new


def _with_ones_rows(v):
    return jnp.concatenate([v, jnp.ones((ONES_ROWS, v.shape[1]), v.dtype)], axis=0)


def _run_pipeline(qi, tail, produce, consume, pv):
    n_tail = len(tail)

    def step(j, next_kind):
        pv(jnp.maximum(j - 1, 0))
        consume(j)
        if next_kind is not None:
            produce(j + 1, next_kind)

    def finish(first_tile, kinds):
        j = first_tile
        for kind in kinds:
            step(j, kind)
            j = j + 1
        step(j, None)
        pv(j)

    for small in range(n_tail):
        @pl.when(qi == small)
        def _(small=small):
            kinds = tail[n_tail - 1 - small:]
            produce(0, kinds[0])
            finish(0, kinds[1:])

    @pl.when(qi >= n_tail)
    def _():
        produce(0, "far")

        def far_step(j, carry):
            step(j, "far")
            return carry
        lax.fori_loop(0, qi - n_tail, far_step, 0)
        finish(qi - n_tail, tail)


def _pv_step(v, p_ref, alpha_ref, acc_ref, idx):
    acc_ref[idx] = alpha_ref[idx] * acc_ref[idx] + jnp.dot(
        v, p_ref[idx], preferred_element_type=F32)


def _init_softmax_state(m_ref, alpha_ref, acc_ref, p_prev_ref):
    m_ref[...] = jnp.full(m_ref.shape, NEG_INF, F32)
    alpha_ref[...] = jnp.ones_like(alpha_ref)
    acc_ref[...] = jnp.zeros_like(acc_ref)
    p_prev_ref[...] = jnp.zeros_like(p_prev_ref)


def _aug_rows(rows8, shape):
    row = lax.broadcasted_iota(jnp.int32, shape, 0)
    out = jnp.zeros(shape, F32)
    for r, val in enumerate(rows8):
        out = jnp.where(row == r, val, out)
    return out


def _fox_attn_kernel(q_ref, k_ref, v_ref, c_ref, g_ref, o_ref,
                     kaug_ref, vaug_ref, m_ref, alpha_ref, mx_ref, acc_ref, s_ref, p_ref,
                     *, hp, n_chunks):
    qi = pl.program_id(2)
    t = q_ref.shape[-1]
    pad = jnp.zeros((2 * HEAD_DIM - HEAD_DIM - 8, t), F32)

    @pl.when(qi == 0)
    def _():
        def prep(j, carry):
            for h in range(hp):
                kt = k_ref[0, j, h * HEAD_DIM:(h + 1) * HEAD_DIM, :].astype(F32)
                hi, mid, lo = _split3(c_ref[0, j, 0, h:h + 1, :])
                aug = _aug_rows([-hi, -mid, -lo, 1.0, 1.0, 1.0], (8, t))
                kaug_ref[h, j] = jnp.concatenate([kt, aug, pad], axis=0).T.astype(BF16)
                vaug_ref[h, j] = _with_ones_rows(v_ref[0, j, h * HEAD_DIM:(h + 1) * HEAD_DIM, :])
            return carry
        lax.fori_loop(0, n_chunks, prep, 0)

    qa = []
    for h in range(hp):
        qf = q_ref[0, h * HEAD_DIM:(h + 1) * HEAD_DIM, :].astype(F32)
        hi, mid, lo = _split3(c_ref[0, qi, 0, h:h + 1, :])
        aug = _aug_rows([1.0, 1.0, 1.0, hi, mid, lo], (8, t))
        qa.append(jnp.concatenate([qf, aug, pad], axis=0).astype(BF16))

    _init_softmax_state(m_ref, alpha_ref, acc_ref, p_ref)

    def produce(j, kind):
        for h in range(hp):
            s = jnp.dot(kaug_ref[h, j], qa[h], preferred_element_type=F32)
            if kind == "diag":
                kpos = lax.broadcasted_iota(jnp.int32, (t, t), 0)
                qpos = lax.broadcasted_iota(jnp.int32, (t, t), 1)
                s = jnp.where(kpos <= qpos, s, NEG_INF)
            _store_scores(s, s_ref, mx_ref, h)

    def consume(j):
        for h in range(hp):
            _softmax_step(s_ref, mx_ref, m_ref, alpha_ref, p_ref, h)

    def pv(j):
        for h in range(hp):
            _pv_step(vaug_ref[h, j], p_ref, alpha_ref, acc_ref, h)

    _run_pipeline(qi, ("diag",), produce, consume, pv)

    for h in range(hp):
        o = acc_ref[h, 0:HEAD_DIM] * (1.0 / acc_ref[h, HEAD_DIM:HEAD_DIM + 1])
        g = g_ref[0, h * HEAD_DIM:(h + 1) * HEAD_DIM, :].astype(F32)
        o_ref[0, h * HEAD_DIM:(h + 1) * HEAD_DIM, :] = (o * _silu(g)).astype(BF16)


def _fox_attn(q_t, k_c, v_c, c_c, g_t):
    b, width, s = q_t.shape
    t = ATTN_TILE
    hp = FOX_HEADS_PER_STEP
    n_heads = width // HEAD_DIM
    n_chunks = s // t
    rows = hp * HEAD_DIM
    c_c = c_c.reshape(b, n_chunks, n_heads // hp, hp, t)
    tile = pl.BlockSpec((1, rows, t), lambda bi, hg, qi: (bi, hg, qi))
    full = pl.BlockSpec((1, n_chunks, rows, t), lambda bi, hg, qi: (bi, 0, hg, 0))
    return pl.pallas_call(
        functools.partial(_fox_attn_kernel, hp=hp, n_chunks=n_chunks),
        grid=(b, n_heads // hp, n_chunks),
        in_specs=[
            tile, full, full,
            pl.BlockSpec((1, n_chunks, 1, hp, t), lambda bi, hg, qi: (bi, 0, hg, 0, 0)),
            tile,
        ],
        out_specs=tile,
        out_shape=jax.ShapeDtypeStruct((b, width, s), BF16),
        scratch_shapes=[
            pltpu.VMEM((hp, n_chunks, t, 2 * HEAD_DIM), BF16),
            pltpu.VMEM((hp, n_chunks, HEAD_DIM + ONES_ROWS, t), BF16),
            pltpu.VMEM((hp, 1, t), F32),
            pltpu.VMEM((hp, 1, t), F32),
            pltpu.VMEM((hp, 1, t), F32),
            pltpu.VMEM((hp, HEAD_DIM + ONES_ROWS, t), F32),
            pltpu.VMEM((hp, t, t), F32),
            pltpu.VMEM((hp, t, t), BF16),
        ],
        compiler_params=_params(3, _ATTN_FLAGS),
        name="fox_attn",
    )(q_t, k_c, v_c, c_c, g_t)


def _mid_kernel(o_ref, x_ref, wo_ref, gkv_ref, gb_ref, wkv_ref, wb_ref, gk_ref, gq_ref,
                h_ref, k_ref, v_ref, q_ref, g_ref, *, n_heads):
    half = n_heads * HEAD_DIM
    h1 = x_ref[0] + lax.dot_general(o_ref[0], wo_ref[...], _TN, preferred_element_type=F32)
    h_ref[0] = h1
    ms = jnp.mean(h1 * h1, axis=-1, keepdims=True)
    hn = h1 * lax.rsqrt(ms + RMS_EPS)
    u_kv = (hn * gkv_ref[...]).astype(BF16)
    u_b = (hn * gb_ref[...]).astype(BF16)

    def pack_heads(ref, a, b2):
        for h in range(n_heads):
            sl = slice(h * HEAD_DIM, (h + 1) * HEAD_DIM)
            ref[..., h * 2 * HEAD_DIM:h * 2 * HEAD_DIM + HEAD_DIM, :] = a[sl]
            ref[..., h * 2 * HEAD_DIM + HEAD_DIM:(h + 1) * 2 * HEAD_DIM, :] = b2[sl]

    kv = lax.dot_general(wkv_ref[...], u_kv, _NT, preferred_element_type=F32)
    k1 = _head_norm(kv[0:half], gk_ref[0], n_heads, 1.0).astype(BF16)
    k2 = _head_norm(kv[half:2 * half], gk_ref[1], n_heads, 1.0).astype(BF16)
    pack_heads(k_ref.at[0, 0], k1, k2)
    v_ref[0, 0] = kv[2 * half:].astype(BF16)

    qg = lax.dot_general(wb_ref[...], u_b, _NT, preferred_element_type=F32)
    q1 = _head_norm(qg[0:half], gq_ref[0], n_heads, QK_SCALE).astype(BF16)
    q2 = _head_norm(qg[half:2 * half], gq_ref[1], n_heads, QK_SCALE).astype(BF16)
    pack_heads(q_ref.at[0], q1, q2)
    g_ref[0] = qg[2 * half:].astype(BF16)


def _mid(o_t, x, w_out, kv_norm, kv_w, kv_k_norm, b_norm, b_w_in, b_q_norm):
    b, s, d = x.shape
    tm = ROW_TILE
    n_heads = kv_w.shape[1] // (4 * HEAD_DIM)
    width = 2 * n_heads * HEAD_DIM
    const2 = lambda *_: (0, 0)
    const3 = lambda *_: (0, 0, 0)
    fm = pl.BlockSpec((1, width, tm), lambda bi, i: (bi, 0, i))
    tok = pl.BlockSpec((1, tm, d), lambda bi, i: (bi, i, 0))
    chunked = pl.BlockSpec((1, 1, width, tm), lambda bi, i: (bi, i, 0, 0))
    return pl.pallas_call(
        functools.partial(_mid_kernel, n_heads=n_heads),
        grid=(b, s // tm),
        in_specs=[
            pl.BlockSpec((1, o_t.shape[1], tm), lambda bi, i: (bi, 0, i)),
            tok,
            pl.BlockSpec(w_out.shape, const2),
            pl.BlockSpec((1, d), const2),
            pl.BlockSpec((1, d), const2),
            pl.BlockSpec((2 * width, d), const2),
            pl.BlockSpec((2 * width, d), const2),
            pl.BlockSpec((2, HEAD_DIM, 1), const3),
            pl.BlockSpec((2, HEAD_DIM, 1), const3),
        ],
        out_specs=[tok, chunked, chunked, fm, fm],
        out_shape=[
            jax.ShapeDtypeStruct((b, s, d), F32),
            jax.ShapeDtypeStruct((b, s // tm, width, tm), BF16),
            jax.ShapeDtypeStruct((b, s // tm, width, tm), BF16),
            jax.ShapeDtypeStruct((b, width, s), BF16),
            jax.ShapeDtypeStruct((b, width, s), BF16),
        ],
        compiler_params=_params(2),
        name="mid_proj",
    )(o_t, x, w_out.astype(BF16), kv_norm.reshape(1, d), b_norm.reshape(1, d),
      kv_w.T.astype(BF16), b_w_in.T.astype(BF16),
      kv_k_norm.reshape(2, HEAD_DIM, 1), b_q_norm.reshape(2, HEAD_DIM, 1))


def _diff_attn_kernel(q_ref, k_ref, v_ref, brow_ref, lam_ref, gs_ref, g_ref, o_ref,
                      ktok_ref, vaug_ref, bias_ref, m_ref, alpha_ref, mx_ref, acc_ref, s_ref, p_ref,
                      *, hd, n_chunks):
    qi = pl.program_id(2)
    t = q_ref.shape[-1]
    dv = DIFF_V_DIM
    streams = [(h, mp) for h in range(hd) for mp in range(2)]

    @pl.when(qi == 0)
    def _():
        def prep(j, carry):
            for h in range(hd):
                ktok_ref[h, j] = k_ref[0, j, h * dv:(h + 1) * dv, :].astype(F32).T.astype(BF16)
                vaug_ref[h, j] = _with_ones_rows(v_ref[0, j, h * dv:(h + 1) * dv, :])
            return carry
        lax.fori_loop(0, n_chunks, prep, 0)
        for h, mp in streams:
            for off in range(2):
                rows = jnp.broadcast_to(brow_ref[h, mp, off], (t, 2 * t))
                bias_ref[h, mp, off] = pltpu.roll(rows, 0, 1, stride=1, stride_axis=0)[:, 0:t]

    qa = []
    for h in range(hd):
        q = q_ref[0, h * dv:(h + 1) * dv, :]
        row = lax.broadcasted_iota(jnp.int32, q.shape, 0)
        zero = jnp.zeros_like(q)
        qa += [jnp.where(row < HEAD_DIM, q, zero), jnp.where(row >= HEAD_DIM, q, zero)]

    _init_softmax_state(m_ref, alpha_ref, acc_ref, p_ref)

    def produce(j, kind):
        for i, (h, mp) in enumerate(streams):
            s = jnp.dot(ktok_ref[h, j], qa[i], preferred_element_type=F32)
            if kind == "near":
                s = s + bias_ref[h, mp, 1]
            elif kind == "diag":
                kchunk = lax.broadcasted_iota(jnp.int32, (t, t), 0) // CHUNK
                qchunk = lax.broadcasted_iota(jnp.int32, (t, t), 1) // CHUNK
                s = jnp.where(kchunk <= qchunk, s + bias_ref[h, mp, 0], NEG_INF)
            _store_scores(s, s_ref, mx_ref, i)

    def consume(j):
        for i in range(len(streams)):
            _softmax_step(s_ref, mx_ref, m_ref, alpha_ref, p_ref, i)

    def pv(j):
        for i, (h, mp) in enumerate(streams):
            _pv_step(vaug_ref[h, j], p_ref, alpha_ref, acc_ref, i)

    _run_pipeline(qi, ("near", "diag"), produce, consume, pv)

    lam_p = lam_ref[...]
    e1 = jnp.exp(jnp.sum(lam_p[0:1] * lam_p[1:2], axis=1, keepdims=True))
    e2 = jnp.exp(jnp.sum(lam_p[2:3] * lam_p[3:4], axis=1, keepdims=True))
    lam = e1 - e2 + LAMBDA_INIT
    for h in range(hd):
        a1, a2 = acc_ref[2 * h], acc_ref[2 * h + 1]
        o = a1[0:dv] * (1.0 / a1[dv:dv + 1]) - lam * (a2[0:dv] * (1.0 / a2[dv:dv + 1]))
        ms = jnp.mean(o * o, axis=0, keepdims=True)
        y = o * lax.rsqrt(ms + RMS_EPS) * gs_ref[...] * (1.0 - LAMBDA_INIT)
        g = g_ref[0, h * dv:(h + 1) * dv, :].astype(F32)
        o_ref[0, h * dv:(h + 1) * dv, :] = (y * _silu(g)).astype(BF16)


def _t5_bucket(rel):
    half = NUM_BUCKETS // 2
    max_exact = half // 2
    ret = jnp.where(rel > 0, half, 0)
    n = jnp.abs(rel)
    n_f = jnp.maximum(n, 1).astype(jnp.float32)
    large = max_exact + (jnp.log(n_f / max_exact) / math.log(MAX_DISTANCE / max_exact)
                         * (half - max_exact)).astype(jnp.int32)
    large = jnp.minimum(large, half - 1)
    return ret + jnp.where(n < max_exact, n, large)


def _bias_rows(rel_bias, t):
    dw = jnp.arange(2 * t)
    d = jnp.where(dw < t, dw, dw - 2 * t)
    rel = jnp.stack([-d, -d - t])
    rows = rel_bias[_t5_bucket(rel)].astype(F32)
    far = rel_bias[_t5_bucket(jnp.int32(-MAX_DISTANCE))].astype(F32)
    return ((rows - far) * LOG2E).transpose(3, 2, 0, 1)[:, :, :, None, :]


def _diff_attn(q_t, k_c, v_c, g_t, rel_bias, lam_params, g_sub):
    b, width, s = q_t.shape
    t = ATTN_TILE
    n_heads = width // DIFF_V_DIM
    n_chunks = s // t
    hd = DIFF_HEADS_PER_STEP
    rows = hd * DIFF_V_DIM
    n_streams = 2 * hd
    tile = pl.BlockSpec((1, rows, t), lambda bi, hg, qi: (bi, hg, qi))
    full = pl.BlockSpec((1, n_chunks, rows, t), lambda bi, hg, qi: (bi, 0, hg, 0))
    return pl.pallas_call(
        functools.partial(_diff_attn_kernel, hd=hd, n_chunks=n_chunks),
        grid=(b, n_heads // hd, n_chunks),
        in_specs=[
            tile, full, full,
            pl.BlockSpec((hd, 2, 2, 1, 2 * t), lambda bi, hg, qi: (hg, 0, 0, 0, 0)),
            pl.BlockSpec((4, HEAD_DIM), lambda *_: (0, 0)),
            pl.BlockSpec((DIFF_V_DIM, 1), lambda *_: (0, 0)),
            tile,
        ],
        out_specs=tile,
        out_shape=jax.ShapeDtypeStruct((b, width, s), BF16),
        scratch_shapes=[
            pltpu.VMEM((hd, n_chunks, t, DIFF_V_DIM), BF16),
            pltpu.VMEM((hd, n_chunks, DIFF_V_DIM + ONES_ROWS, t), BF16),
            pltpu.VMEM((hd, 2, 2, t, t), F32),
            pltpu.VMEM((n_streams, 1, t), F32),
            pltpu.VMEM((n_streams, 1, t), F32),
            pltpu.VMEM((n_streams, 1, t), F32),
            pltpu.VMEM((n_streams, DIFF_V_DIM + ONES_ROWS, t), F32),
            pltpu.VMEM((n_streams, t, t), F32),
            pltpu.VMEM((n_streams, t, t), BF16),
        ],
        compiler_params=_params(3, _ATTN_FLAGS),
        name="diff_attn",
    )(q_t, k_c, v_c, _bias_rows(rel_bias, t), lam_params, g_sub.reshape(DIFF_V_DIM, 1), g_t)


def _out_kernel(o_ref, h_ref, w_ref, y_ref):
    y_ref[0] = h_ref[0] + lax.dot_general(o_ref[0], w_ref[...], _TN, preferred_element_type=F32)


def _out_proj(o_t, h1, w_out):
    b, s, d = h1.shape
    tm = ROW_TILE
    tok = pl.BlockSpec((1, tm, d), lambda bi, i: (bi, i, 0))
    return pl.pallas_call(
        _out_kernel,
        grid=(b, s // tm),
        in_specs=[
            pl.BlockSpec((1, o_t.shape[1], tm), lambda bi, i: (bi, 0, i)),
            tok,
            pl.BlockSpec(w_out.shape, lambda *_: (0, 0)),
        ],
        out_specs=tok,
        out_shape=jax.ShapeDtypeStruct((b, s, d), F32),
        compiler_params=_params(2),
        name="out_proj",
    )(o_t, h1, w_out.astype(BF16))


def kernel(x, a_norm, a_w_in, a_b_f, a_q_norm, a_k_norm, a_w_out, kv_norm, kv_w, kv_k_norm,
           rel_bias, b_norm, b_w_in, b_q_norm, b_lam_q1, b_lam_k1, b_lam_q2, b_lam_k2,
           b_sub_norm, b_w_out):
    assert a_norm.shape[0] == 1 and b_norm.shape[0] == 1
    assert x.shape[1] % ATTN_TILE == 0 and ATTN_TILE == ROW_TILE
    q_t, k_c, v_c, g_t, c_c = _fox_proj(x, a_norm[0], a_w_in[0], a_b_f[0], a_q_norm[0], a_k_norm[0])
    o_t = _fox_attn(q_t, k_c, v_c, c_c, g_t)
    h1, k2_c, v2_c, q2_t, g2_t = _mid(o_t, x, a_w_out[0], kv_norm, kv_w, kv_k_norm,
                                      b_norm[0], b_w_in[0], b_q_norm[0])
    lam_params = jnp.stack([b_lam_q1[0], b_lam_k1[0], b_lam_q2[0], b_lam_k2[0]])
    o2_t = _diff_attn(q2_t, k2_c, v2_c, g2_t, rel_bias, lam_params, b_sub_norm[0])
    return _out_proj(o2_t, h1, b_w_out[0])
```

```python
import functools
import math

import jax
import jax.numpy as jnp
from jax import lax
from jax.experimental import pallas as pl
from jax.experimental.pallas import tpu as pltpu

HEAD_DIM = 64
DIFF_V_DIM = 2 * HEAD_DIM
CHUNK = 64
NUM_BUCKETS = 32
MAX_DISTANCE = 128
RMS_EPS = 1e-6
NEG_INF = -1e30
LOG2E = math.log2(math.e)
QK_SCALE = HEAD_DIM ** -0.5 * LOG2E
LAMBDA_INIT = 0.8 - 0.6 * math.exp(-0.3 * 1)

ROW_TILE = 512
ATTN_TILE = 512
FOX_HEADS_PER_STEP = 2
DIFF_HEADS_PER_STEP = 1
ONES_ROWS = 16
V7X_VMEM_LIMIT_BYTES = 56 * 1024 * 1024

F32 = jnp.float32
BF16 = jnp.bfloat16
_NT = (((1,), (1,)), ((), ()))
_TN = (((0,), (0,)), ((), ()))


def _params(n_axes, flags=None):
    return pltpu.CompilerParams(
        dimension_semantics=("arbitrary",) * n_axes,
        vmem_limit_bytes=V7X_VMEM_LIMIT_BYTES, flags=flags)


_ATTN_FLAGS = None


def _head_norm(p_t, g_col, n_heads, scale):
    t = p_t.shape[-1]
    p3 = p_t.reshape(n_heads, HEAD_DIM, t)
    ms = jnp.mean(p3 * p3, axis=1, keepdims=True)
    y = p3 * lax.rsqrt(ms + RMS_EPS) * (g_col * scale)[None]
    return y.reshape(n_heads * HEAD_DIM, t)


def _split3(c):
    hi = c.astype(BF16).astype(F32)
    r = c - hi
    mid = r.astype(BF16).astype(F32)
    lo = (r - mid).astype(BF16).astype(F32)
    return hi, mid, lo


def _silu(g):
    return g / (1.0 + jnp.exp(-g))


def _fox_proj_kernel(x_ref, gn_ref, wq_ref, wk_ref, wv_ref, wg_ref, wf_ref, bf_ref,
                     gq_ref, gk_ref, q_ref, k_ref, v_ref, g_ref, c_ref, carry_ref,
                     *, n_heads):
    @pl.when(pl.program_id(1) == 0)
    def _():
        carry_ref[...] = jnp.zeros_like(carry_ref)

    x = x_ref[0]
    tm = x.shape[0]
    ms = jnp.mean(x * x, axis=-1, keepdims=True)
    u = (x * lax.rsqrt(ms + RMS_EPS) * gn_ref[...]).astype(BF16)

    def proj(w_ref):
        return lax.dot_general(w_ref[...], u, _NT, preferred_element_type=F32)

    q_ref[0] = _head_norm(proj(wq_ref), gq_ref[...], n_heads, QK_SCALE).astype(BF16)
    k_ref[0, 0] = _head_norm(proj(wk_ref), gk_ref[...], n_heads, 1.0).astype(BF16)
    v_ref[0, 0] = proj(wv_ref).astype(BF16)
    g_ref[0] = proj(wg_ref).astype(BF16)

    z = proj(wf_ref) + bf_ref[...]
    log_f = jnp.minimum(z, 0.0) - jnp.log1p(jnp.exp(-jnp.abs(z)))
    lane = lax.broadcasted_iota(jnp.int32, log_f.shape, 1)
    cs = log_f
    shift = 1
    while shift < tm:
        cs = cs + jnp.where(lane >= shift, pltpu.roll(cs, shift, axis=1), 0.0)
        shift *= 2
    c = cs + carry_ref[:, 0:1]
    c_ref[0, 0] = c * LOG2E
    carry_ref[...] = jnp.broadcast_to(c[:, tm - 1:tm], carry_ref.shape)


def _fox_proj(x, a_norm, w_in, b_f, g_q, g_k):
    b, s, d = x.shape
    n_heads = b_f.shape[0]
    width = n_heads * HEAD_DIM
    tm = ROW_TILE
    w_t = w_in.T.astype(BF16)
    wq, wk, wv, wg, wf = (w_t[0:width], w_t[width:2 * width], w_t[2 * width:3 * width],
                          w_t[3 * width:4 * width], w_t[4 * width:])
    const = lambda *_: (0, 0)
    w_spec = pl.BlockSpec((width, d), const)
    col = pl.BlockSpec((HEAD_DIM, 1), const)
    return pl.pallas_call(
        functools.partial(_fox_proj_kernel, n_heads=n_heads),
        grid=(b, s // tm),
        in_specs=[
            pl.BlockSpec((1, tm, d), lambda bi, i: (bi, i, 0)),
            pl.BlockSpec((1, d), const),
            w_spec, w_spec, w_spec, w_spec,
            pl.BlockSpec((n_heads, d), const),
            pl.BlockSpec((n_heads, 1), const),
            col, col,
        ],
        out_specs=[
            pl.BlockSpec((1, width, tm), lambda bi, i: (bi, 0, i)),
            pl.BlockSpec((1, 1, width, tm), lambda bi, i: (bi, i, 0, 0)),
            pl.BlockSpec((1, 1, width, tm), lambda bi, i: (bi, i, 0, 0)),
            pl.BlockSpec((1, width, tm), lambda bi, i: (bi, 0, i)),
            pl.BlockSpec((1, 1, n_heads, tm), lambda bi, i: (bi, i, 0, 0)),
        ],
        out_shape=[
            jax.ShapeDtypeStruct((b, width, s), BF16),
            jax.ShapeDtypeStruct((b, s // tm, width, tm), BF16),
            jax.ShapeDtypeStruct((b, s // tm, width, tm), BF16),
            jax.ShapeDtypeStruct((b, width, s), BF16),
            jax.ShapeDtypeStruct((b, s // tm, n_heads, tm), F32),
        ],
        scratch_shapes=[pltpu.VMEM((n_heads, 128), F32)],
        compiler_params=_params(2),
        name="fox_proj",
    )(x, a_norm.reshape(1, d), wq, wk, wv, wg, wf, b_f.reshape(n_heads, 1),
      g_q.reshape(HEAD_DIM, 1), g_k.reshape(HEAD_DIM, 1))


def _store_scores(s, s_ref, mx_ref, idx):
    s_ref[idx] = s
    mx_ref[idx] = jnp.max(s, axis=0, keepdims=True)


def _softmax_step(s_ref, mx_ref, m_ref, alpha_ref, p_ref, idx):
    m_prev = m_ref[idx]
    m_new = jnp.maximum(m_prev, mx_ref[idx])
    p_ref[idx] = jnp.exp2(s_ref[idx] - m_new).astype(BF16)
    alpha_ref[idx] = jnp.exp2(m_prev - m_new)
    m_ref[idx] = m_new


def _with_ones_rows(v):
    return jnp.concatenate([v, jnp.ones((ONES_ROWS, v.shape[1]), v.dtype)], axis=0)


def _run_pipeline(qi, tail, produce, produce_next, consume, pv):
    n_tail = len(tail)

    def first_kind(q_tile):
        return "far" if q_tile >= n_tail else tail[n_tail - 1 - q_tile]

    def step(j, produce_following):
        pv(jnp.maximum(j - 1, 0))
        consume(j)
        produce_following()

    def finish(first_tile, kinds, following_first_kind):
        j = first_tile
        for kind in kinds:
            step(j, functools.partial(produce, j + 1, kind))
            j = j + 1
        step(j, functools.partial(produce_next, following_first_kind))
        pv(j)

    for small in range(n_tail):
        @pl.when(qi == small)
        def _(small=small):
            kinds = tail[n_tail - 1 - small:]
            if small == 0:
                produce(0, kinds[0])
            finish(0, kinds[1:], first_kind(small + 1))

    @pl.when(qi >= n_tail)
    def _():
        def far_step(j, carry):
            step(j, functools.partial(produce, j + 1, "far"))
            return carry
        lax.fori_loop(0, qi - n_tail, far_step, 0)
        finish(qi - n_tail, tail, "far")


def _pv_step(v, p_ref, alpha_ref, acc_ref, idx):
    acc_ref[idx] = alpha_ref[idx] * acc_ref[idx] + jnp.dot(
        v, p_ref[idx], preferred_element_type=F32)


def _init_softmax_state(m_ref, alpha_ref, acc_ref, p_prev_ref):
    m_ref[...] = jnp.full(m_ref.shape, NEG_INF, F32)
    alpha_ref[...] = jnp.ones_like(alpha_ref)
    acc_ref[...] = jnp.zeros_like(acc_ref)
    p_prev_ref[...] = jnp.zeros_like(p_prev_ref)


def _aug_rows(rows8, shape):
    row = lax.broadcasted_iota(jnp.int32, shape, 0)
    out = jnp.zeros(shape, F32)
    for r, val in enumerate(rows8):
        out = jnp.where(row == r, val, out)
    return out


def _fox_attn_kernel(q_ref, qn_ref, k_ref, v_ref, c_ref, g_ref, o_ref,
                     kaug_ref, vaug_ref, m_ref, alpha_ref, mx_ref, acc_ref, s_ref, p_ref,
                     *, hp, n_chunks):
    qi = pl.program_id(2)
    t = q_ref.shape[-1]
    pad = jnp.zeros((2 * HEAD_DIM - HEAD_DIM - 8, t), F32)

    @pl.when(qi == 0)
    def _():
        def prep(j, carry):
            for h in range(hp):
                kt = k_ref[0, j, h * HEAD_DIM:(h + 1) * HEAD_DIM, :].astype(F32)
                hi, mid, lo = _split3(c_ref[0, j, 0, h:h + 1, :])
                aug = _aug_rows([-hi, -mid, -lo, 1.0, 1.0, 1.0], (8, t))
                kaug_ref[h, j] = jnp.concatenate([kt, aug, pad], axis=0).T.astype(BF16)
                vaug_ref[h, j] = _with_ones_rows(v_ref[0, j, h * HEAD_DIM:(h + 1) * HEAD_DIM, :])
            return carry
        lax.fori_loop(0, n_chunks, prep, 0)

    def augmented_queries(queries_ref, q_tile):
        qa = []
        for h in range(hp):
            qf = queries_ref[0, h * HEAD_DIM:(h + 1) * HEAD_DIM, :].astype(F32)
            hi, mid, lo = _split3(c_ref[0, q_tile, 0, h:h + 1, :])
            aug = _aug_rows([1.0, 1.0, 1.0, hi, mid, lo], (8, t))
            qa.append(jnp.concatenate([qf, aug, pad], axis=0).astype(BF16))
        return qa

    qa_this = augmented_queries(q_ref, qi)

    _init_softmax_state(m_ref, alpha_ref, acc_ref, p_ref)

    def produce(j, kind, qa=qa_this):
        for h in range(hp):
            s = jnp.dot(kaug_ref[h, j], qa[h], preferred_element_type=F32)
            if kind == "diag":
                kpos = lax.broadcasted_iota(jnp.int32, (t, t), 0)
                qpos = lax.broadcasted_iota(jnp.int32, (t, t), 1)
                s = jnp.where(kpos <= qpos, s, NEG_INF)
            _store_scores(s, s_ref, mx_ref, h)

    def produce_next(kind):
        produce(0, kind, augmented_queries(qn_ref, jnp.minimum(qi + 1, n_chunks - 1)))

    def consume(j):
        for h in range(hp):
            _softmax_step(s_ref, mx_ref, m_ref, alpha_ref, p_ref, h)

    def pv(j):
        for h in range(hp):
            _pv_step(vaug_ref[h, j], p_ref, alpha_ref, acc_ref, h)

    _run_pipeline(qi, ("diag",), produce, produce_next, consume, pv)

    for h in range(hp):
        o = acc_ref[h, 0:HEAD_DIM] * (1.0 / acc_ref[h, HEAD_DIM:HEAD_DIM + 1])
        g = g_ref[0, h * HEAD_DIM:(h + 1) * HEAD_DIM, :].astype(F32)
        o_ref[0, h * HEAD_DIM:(h + 1) * HEAD_DIM, :] = (o * _silu(g)).astype(BF16)


def _fox_attn(q_t, k_c, v_c, c_c, g_t):
    b, width, s = q_t.shape
    t = ATTN_TILE
    hp = FOX_HEADS_PER_STEP
    n_heads = width // HEAD_DIM
    n_chunks = s // t
    rows = hp * HEAD_DIM
    c_c = c_c.reshape(b, n_chunks, n_heads // hp, hp, t)
    tile = pl.BlockSpec((1, rows, t), lambda bi, hg, qi: (bi, hg, qi))
    next_tile = pl.BlockSpec((1, rows, t), lambda bi, hg, qi: (bi, hg, jnp.minimum(qi + 1, n_chunks - 1)))
    full = pl.BlockSpec((1, n_chunks, rows, t), lambda bi, hg, qi: (bi, 0, hg, 0))
    return pl.pallas_call(
        functools.partial(_fox_attn_kernel, hp=hp, n_chunks=n_chunks),
        grid=(b, n_heads // hp, n_chunks),
        in_specs=[
            tile, next_tile, full, full,
            pl.BlockSpec((1, n_chunks, 1, hp, t), lambda bi, hg, qi: (bi, 0, hg, 0, 0)),
            tile,
        ],
        out_specs=tile,
        out_shape=jax.ShapeDtypeStruct((b, width, s), BF16),
        scratch_shapes=[
            pltpu.VMEM((hp, n_chunks, t, 2 * HEAD_DIM), BF16),
            pltpu.VMEM((hp, n_chunks, HEAD_DIM + ONES_ROWS, t), BF16),
            pltpu.VMEM((hp, 1, t), F32),
            pltpu.VMEM((hp, 1, t), F32),
            pltpu.VMEM((hp, 1, t), F32),
            pltpu.VMEM((hp, HEAD_DIM + ONES_ROWS, t), F32),
            pltpu.VMEM((hp, t, t), F32),
            pltpu.VMEM((hp, t, t), BF16),
        ],
        compiler_params=_params(3, _ATTN_FLAGS),
        name="fox_attn",
    )(q_t, q_t, k_c, v_c, c_c, g_t)


def _mid_kernel(o_ref, x_ref, wo_ref, gkv_ref, gb_ref, wkv_ref, wb_ref, gk_ref, gq_ref,
                h_ref, k_ref, v_ref, q_ref, g_ref, *, n_heads):
    half = n_heads * HEAD_DIM
    h1 = x_ref[0] + lax.dot_general(o_ref[0], wo_ref[...], _TN, preferred_element_type=F32)
    h_ref[0] = h1
    ms = jnp.mean(h1 * h1, axis=-1, keepdims=True)
    hn = h1 * lax.rsqrt(ms + RMS_EPS)
    u_kv = (hn * gkv_ref[...]).astype(BF16)
    u_b = (hn * gb_ref[...]).astype(BF16)

    def pack_heads(ref, a, b2):
        for h in range(n_heads):
            sl = slice(h * HEAD_DIM, (h + 1) * HEAD_DIM)
            ref[..., h * 2 * HEAD_DIM:h * 2 * HEAD_DIM + HEAD_DIM, :] = a[sl]
            ref[..., h * 2 * HEAD_DIM + HEAD_DIM:(h + 1) * 2 * HEAD_DIM, :] = b2[sl]

    kv = lax.dot_general(wkv_ref[...], u_kv, _NT, preferred_element_type=F32)
    k1 = _head_norm(kv[0:half], gk_ref[0], n_heads, 1.0).astype(BF16)
    k2 = _head_norm(kv[half:2 * half], gk_ref[1], n_heads, 1.0).astype(BF16)
    pack_heads(k_ref.at[0, 0], k1, k2)
    v_ref[0, 0] = kv[2 * half:].astype(BF16)

    qg = lax.dot_general(wb_ref[...], u_b, _NT, preferred_element_type=F32)
    q1 = _head_norm(qg[0:half], gq_ref[0], n_heads, QK_SCALE).astype(BF16)
    q2 = _head_norm(qg[half:2 * half], gq_ref[1], n_heads, QK_SCALE).astype(BF16)
    pack_heads(q_ref.at[0], q1, q2)
    g_ref[0] = qg[2 * half:].astype(BF16)


def _mid(o_t, x, w_out, kv_norm, kv_w, kv_k_norm, b_norm, b_w_in, b_q_norm):
    b, s, d = x.shape
    tm = ROW_TILE
    n_heads = kv_w.shape[1] // (4 * HEAD_DIM)
    width = 2 * n_heads * HEAD_DIM
    const2 = lambda *_: (0, 0)
    const3 = lambda *_: (0, 0, 0)
    fm = pl.BlockSpec((1, width, tm), lambda bi, i: (bi, 0, i))
    tok = pl.BlockSpec((1, tm, d), lambda bi, i: (bi, i, 0))
    chunked = pl.BlockSpec((1, 1, width, tm), lambda bi, i: (bi, i, 0, 0))
    return pl.pallas_call(
        functools.partial(_mid_kernel, n_heads=n_heads),
        grid=(b, s // tm),
        in_specs=[
            pl.BlockSpec((1, o_t.shape[1], tm), lambda bi, i: (bi, 0, i)),
            tok,
            pl.BlockSpec(w_out.shape, const2),
            pl.BlockSpec((1, d), const2),
            pl.BlockSpec((1, d), const2),
            pl.BlockSpec((2 * width, d), const2),
            pl.BlockSpec((2 * width, d), const2),
            pl.BlockSpec((2, HEAD_DIM, 1), const3),
            pl.BlockSpec((2, HEAD_DIM, 1), const3),
        ],
        out_specs=[tok, chunked, chunked, fm, fm],
        out_shape=[
            jax.ShapeDtypeStruct((b, s, d), F32),
            jax.ShapeDtypeStruct((b, s // tm, width, tm), BF16),
            jax.ShapeDtypeStruct((b, s // tm, width, tm), BF16),
            jax.ShapeDtypeStruct((b, width, s), BF16),
            jax.ShapeDtypeStruct((b, width, s), BF16),
        ],
        compiler_params=_params(2),
        name="mid_proj",
    )(o_t, x, w_out.astype(BF16), kv_norm.reshape(1, d), b_norm.reshape(1, d),
      kv_w.T.astype(BF16), b_w_in.T.astype(BF16),
      kv_k_norm.reshape(2, HEAD_DIM, 1), b_q_norm.reshape(2, HEAD_DIM, 1))


def _diff_attn_kernel(q_ref, qn_ref, k_ref, v_ref, brow_ref, lam_ref, gs_ref, g_ref, o_ref,
                      ktok_ref, vaug_ref, bias_ref, m_ref, alpha_ref, mx_ref, acc_ref, s_ref, p_ref,
                      *, hd, n_chunks):
    qi = pl.program_id(2)
    t = q_ref.shape[-1]
    dv = DIFF_V_DIM
    streams = [(h, mp) for h in range(hd) for mp in range(2)]

    @pl.when(qi == 0)
    def _():
        def prep(j, carry):
            for h in range(hd):
                ktok_ref[h, j] = k_ref[0, j, h * dv:(h + 1) * dv, :].astype(F32).T.astype(BF16)
                vaug_ref[h, j] = _with_ones_rows(v_ref[0, j, h * dv:(h + 1) * dv, :])
            return carry
        lax.fori_loop(0, n_chunks, prep, 0)
        for h, mp in streams:
            for off in range(2):
                rows = jnp.broadcast_to(brow_ref[h, mp, off], (t, 2 * t))
                bias_ref[h, mp, off] = pltpu.roll(rows, 0, 1, stride=1, stride_axis=0)[:, 0:t]

    def map_queries(queries_ref):
        qa = []
        for h in range(hd):
            q = queries_ref[0, h * dv:(h + 1) * dv, :]
            row = lax.broadcasted_iota(jnp.int32, q.shape, 0)
            zero = jnp.zeros_like(q)
            qa += [jnp.where(row < HEAD_DIM, q, zero), jnp.where(row >= HEAD_DIM, q, zero)]
        return qa

    qa_this = map_queries(q_ref)

    _init_softmax_state(m_ref, alpha_ref, acc_ref, p_ref)

    def produce(j, kind, qa=qa_this):
        for i, (h, mp) in enumerate(streams):
            s = jnp.dot(ktok_ref[h, j], qa[i], preferred_element_type=F32)
            if kind == "near":
                s = s + bias_ref[h, mp, 1]
            elif kind == "diag":
                kchunk = lax.broadcasted_iota(jnp.int32, (t, t), 0) // CHUNK
                qchunk = lax.broadcasted_iota(jnp.int32, (t, t), 1) // CHUNK
                s = jnp.where(kchunk <= qchunk, s + bias_ref[h, mp, 0], NEG_INF)
            _store_scores(s, s_ref, mx_ref, i)

    def consume(j):
        for i in range(len(streams)):
            _softmax_step(s_ref, mx_ref, m_ref, alpha_ref, p_ref, i)

    def pv(j):
        for i, (h, mp) in enumerate(streams):
            _pv_step(vaug_ref[h, j], p_ref, alpha_ref, acc_ref, i)

    def produce_next(kind):
        produce(0, kind, map_queries(qn_ref))

    _run_pipeline(qi, ("near", "diag"), produce, produce_next, consume, pv)

    lam_p = lam_ref[...]
    e1 = jnp.exp(jnp.sum(lam_p[0:1] * lam_p[1:2], axis=1, keepdims=True))
    e2 = jnp.exp(jnp.sum(lam_p[2:3] * lam_p[3:4], axis=1, keepdims=True))
    lam = e1 - e2 + LAMBDA_INIT
    for h in range(hd):
        a1, a2 = acc_ref[2 * h], acc_ref[2 * h + 1]
        o = a1[0:dv] * (1.0 / a1[dv:dv + 1]) - lam * (a2[0:dv] * (1.0 / a2[dv:dv + 1]))
        ms = jnp.mean(o * o, axis=0, keepdims=True)
        y = o * lax.rsqrt(ms + RMS_EPS) * gs_ref[...] * (1.0 - LAMBDA_INIT)
        g = g_ref[0, h * dv:(h + 1) * dv, :].astype(F32)
        o_ref[0, h * dv:(h + 1) * dv, :] = (y * _silu(g)).astype(BF16)


def _t5_bucket(rel):
    half = NUM_BUCKETS // 2
    max_exact = half // 2
    ret = jnp.where(rel > 0, half, 0)
    n = jnp.abs(rel)
    n_f = jnp.maximum(n, 1).astype(jnp.float32)
    large = max_exact + (jnp.log(n_f / max_exact) / math.log(MAX_DISTANCE / max_exact)
                         * (half - max_exact)).astype(jnp.int32)
    large = jnp.minimum(large, half - 1)
    return ret + jnp.where(n < max_exact, n, large)


def _bias_rows(rel_bias, t):
    dw = jnp.arange(2 * t)
    d = jnp.where(dw < t, dw, dw - 2 * t)
    rel = jnp.stack([-d, -d - t])
    rows = rel_bias[_t5_bucket(rel)].astype(F32)
    far = rel_bias[_t5_bucket(jnp.int32(-MAX_DISTANCE))].astype(F32)
    return ((rows - far) * LOG2E).transpose(3, 2, 0, 1)[:, :, :, None, :]


def _diff_attn(q_t, k_c, v_c, g_t, rel_bias, lam_params, g_sub):
    b, width, s = q_t.shape
    t = ATTN_TILE
    n_heads = width // DIFF_V_DIM
    n_chunks = s // t
    hd = DIFF_HEADS_PER_STEP
    rows = hd * DIFF_V_DIM
    n_streams = 2 * hd
    tile = pl.BlockSpec((1, rows, t), lambda bi, hg, qi: (bi, hg, qi))
    next_tile = pl.BlockSpec((1, rows, t), lambda bi, hg, qi: (bi, hg, jnp.minimum(qi + 1, n_chunks - 1)))
    full = pl.BlockSpec((1, n_chunks, rows, t), lambda bi, hg, qi: (bi, 0, hg, 0))
    return pl.pallas_call(
        functools.partial(_diff_attn_kernel, hd=hd, n_chunks=n_chunks),
        grid=(b, n_heads // hd, n_chunks),
        in_specs=[
            tile, next_tile, full, full,
            pl.BlockSpec((hd, 2, 2, 1, 2 * t), lambda bi, hg, qi: (hg, 0, 0, 0, 0)),
            pl.BlockSpec((4, HEAD_DIM), lambda *_: (0, 0)),
            pl.BlockSpec((DIFF_V_DIM, 1), lambda *_: (0, 0)),
            tile,
        ],
        out_specs=tile,
        out_shape=jax.ShapeDtypeStruct((b, width, s), BF16),
        scratch_shapes=[
            pltpu.VMEM((hd, n_chunks, t, DIFF_V_DIM), BF16),
            pltpu.VMEM((hd, n_chunks, DIFF_V_DIM + ONES_ROWS, t), BF16),
            pltpu.VMEM((hd, 2, 2, t, t), F32),
            pltpu.VMEM((n_streams, 1, t), F32),
            pltpu.VMEM((n_streams, 1, t), F32),
            pltpu.VMEM((n_streams, 1, t), F32),
            pltpu.VMEM((n_streams, DIFF_V_DIM + ONES_ROWS, t), F32),
            pltpu.VMEM((n_streams, t, t), F32),
            pltpu.VMEM((n_streams, t, t), BF16),
        ],
        compiler_params=_params(3, _ATTN_FLAGS),
        name="diff_attn",
    )(q_t, q_t, k_c, v_c, _bias_rows(rel_bias, t), lam_params, g_sub.reshape(DIFF_V_DIM, 1), g_t)


def _out_kernel(o_ref, h_ref, w_ref, y_ref):
    y_ref[0] = h_ref[0] + lax.dot_general(o_ref[0], w_ref[...], _TN, preferred_element_type=F32)


def _out_proj(o_t, h1, w_out):
    b, s, d = h1.shape
    tm = ROW_TILE
    tok = pl.BlockSpec((1, tm, d), lambda bi, i: (bi, i, 0))
    return pl.pallas_call(
        _out_kernel,
        grid=(b, s // tm),
        in_specs=[
            pl.BlockSpec((1, o_t.shape[1], tm), lambda bi, i: (bi, 0, i)),
            tok,
            pl.BlockSpec(w_out.shape, lambda *_: (0, 0)),
        ],
        out_specs=tok,
        out_shape=jax.ShapeDtypeStruct((b, s, d), F32),
        compiler_params=_params(2),
        name="out_proj",
    )(o_t, h1, w_out.astype(BF16))


def kernel(x, a_norm, a_w_in, a_b_f, a_q_norm, a_k_norm, a_w_out, kv_norm, kv_w, kv_k_norm,
           rel_bias, b_norm, b_w_in, b_q_norm, b_lam_q1, b_lam_k1, b_lam_q2, b_lam_k2,
           b_sub_norm, b_w_out):
    assert a_norm.shape[0] == 1 and b_norm.shape[0] == 1
    assert x.shape[1] % ATTN_TILE == 0 and ATTN_TILE == ROW_TILE
    q_t, k_c, v_c, g_t, c_c = _fox_proj(x, a_norm[0], a_w_in[0], a_b_f[0], a_q_norm[0], a_k_norm[0])
    o_t = _fox_attn(q_t, k_c, v_c, c_c, g_t)
    h1, k2_c, v2_c, q2_t, g2_t = _mid(o_t, x, a_w_out[0], kv_norm, kv_w, kv_k_norm,
                                      b_norm[0], b_w_in[0], b_q_norm[0])
    lam_params = jnp.stack([b_lam_q1[0], b_lam_k1[0], b_lam_q2[0], b_lam_k2[0]])
    o2_t = _diff_attn(q2_t, k2_c, v2_c, g2_t, rel_bias, lam_params, b_sub_norm[0])
    return _out_proj(o2_t, h1, b_w_out[0])
```

```python
import functools
import math

import jax
import jax.numpy as jnp
from jax import lax
from jax.experimental import pallas as pl
from jax.experimental.pallas import tpu as pltpu

HEAD_DIM = 64
DIFF_V_DIM = 2 * HEAD_DIM
CHUNK = 64
NUM_BUCKETS = 32
MAX_DISTANCE = 128
RMS_EPS = 1e-6
NEG_INF = -1e30
LOG2E = math.log2(math.e)
QK_SCALE = HEAD_DIM ** -0.5 * LOG2E
LAMBDA_INIT = 0.8 - 0.6 * math.exp(-0.3 * 1)

ROW_TILE = 512
ATTN_TILE = 512
FOX_HEADS_PER_STEP = 2
DIFF_HEADS_PER_STEP = 1
ONES_ROWS = 16
SCORE_SPLIT = 1
V7X_VMEM_LIMIT_BYTES = 56 * 1024 * 1024

F32 = jnp.float32
BF16 = jnp.bfloat16
_NT = (((1,), (1,)), ((), ()))
_TN = (((0,), (0,)), ((), ()))


def _params(n_axes, flags=None):
    return pltpu.CompilerParams(
        dimension_semantics=("arbitrary",) * n_axes,
        vmem_limit_bytes=V7X_VMEM_LIMIT_BYTES, flags=flags)


_ATTN_FLAGS = None


def _head_norm(p_t, g_col, n_heads, scale):
    t = p_t.shape[-1]
    p3 = p_t.reshape(n_heads, HEAD_DIM, t)
    ms = jnp.mean(p3 * p3, axis=1, keepdims=True)
    y = p3 * lax.rsqrt(ms + RMS_EPS) * (g_col * scale)[None]
    return y.reshape(n_heads * HEAD_DIM, t)


def _split3(c):
    hi = c.astype(BF16).astype(F32)
    r = c - hi
    mid = r.astype(BF16).astype(F32)
    lo = (r - mid).astype(BF16).astype(F32)
    return hi, mid, lo


def _silu(g):
    return g / (1.0 + jnp.exp(-g))


def _fox_proj_kernel(x_ref, gn_ref, wq_ref, wk_ref, wv_ref, wg_ref, wf_ref, bf_ref,
                     gq_ref, gk_ref, q_ref, k_ref, v_ref, g_ref, c_ref, carry_ref,
                     *, n_heads):
    @pl.when(pl.program_id(1) == 0)
    def _():
        carry_ref[...] = jnp.zeros_like(carry_ref)

    x = x_ref[0]
    tm = x.shape[0]
    ms = jnp.mean(x * x, axis=-1, keepdims=True)
    u = (x * lax.rsqrt(ms + RMS_EPS) * gn_ref[...]).astype(BF16)

    def proj(w_ref):
        return lax.dot_general(w_ref[...], u, _NT, preferred_element_type=F32)

    q_ref[0] = _head_norm(proj(wq_ref), gq_ref[...], n_heads, QK_SCALE).astype(BF16)
    k_ref[0, 0] = _head_norm(proj(wk_ref), gk_ref[...], n_heads, 1.0).astype(BF16)
    v_ref[0, 0] = proj(wv_ref).astype(BF16)
    g_ref[0] = proj(wg_ref).astype(BF16)

    z = proj(wf_ref) + bf_ref[...]
    log_f = jnp.minimum(z, 0.0) - jnp.log1p(jnp.exp(-jnp.abs(z)))
    lane = lax.broadcasted_iota(jnp.int32, log_f.shape, 1)
    cs = log_f
    shift = 1
    while shift < tm:
        cs = cs + jnp.where(lane >= shift, pltpu.roll(cs, shift, axis=1), 0.0)
        shift *= 2
    c = cs + carry_ref[:, 0:1]
    c_ref[0, 0] = c * LOG2E
    carry_ref[...] = jnp.broadcast_to(c[:, tm - 1:tm], carry_ref.shape)


def _fox_proj(x, a_norm, w_in, b_f, g_q, g_k):
    b, s, d = x.shape
    n_heads = b_f.shape[0]
    width = n_heads * HEAD_DIM
    tm = ROW_TILE
    w_t = w_in.T.astype(BF16)
    wq, wk, wv, wg, wf = (w_t[0:width], w_t[width:2 * width], w_t[2 * width:3 * width],
                          w_t[3 * width:4 * width], w_t[4 * width:])
    const = lambda *_: (0, 0)
    w_spec = pl.BlockSpec((width, d), const)
    col = pl.BlockSpec((HEAD_DIM, 1), const)
    return pl.pallas_call(
        functools.partial(_fox_proj_kernel, n_heads=n_heads),
        grid=(b, s // tm),
        in_specs=[
            pl.BlockSpec((1, tm, d), lambda bi, i: (bi, i, 0)),
            pl.BlockSpec((1, d), const),
            w_spec, w_spec, w_spec, w_spec,
            pl.BlockSpec((n_heads, d), const),
            pl.BlockSpec((n_heads, 1), const),
            col, col,
        ],
        out_specs=[
            pl.BlockSpec((1, width, tm), lambda bi, i: (bi, 0, i)),
            pl.BlockSpec((1, 1, width, tm), lambda bi, i: (bi, i, 0, 0)),
            pl.BlockSpec((1, 1, width, tm), lambda bi, i: (bi, i, 0, 0)),
            pl.BlockSpec((1, width, tm), lambda bi, i: (bi, 0, i)),
            pl.BlockSpec((1, 1, n_heads, tm), lambda bi, i: (bi, i, 0, 0)),
        ],
        out_shape=[
            jax.ShapeDtypeStruct((b, width, s), BF16),
            jax.ShapeDtypeStruct((b, s // tm, width, tm), BF16),
            jax.ShapeDtypeStruct((b, s // tm, width, tm), BF16),
            jax.ShapeDtypeStruct((b, width, s), BF16),
            jax.ShapeDtypeStruct((b, s // tm, n_heads, tm), F32),
        ],
        scratch_shapes=[pltpu.VMEM((n_heads, 128), F32)],
        compiler_params=_params(2),
        name="fox_proj",
    )(x, a_norm.reshape(1, d), wq, wk, wv, wg, wf, b_f.reshape(n_heads, 1),
      g_q.reshape(HEAD_DIM, 1), g_k.reshape(HEAD_DIM, 1))


def _store_scores(score_rows, s_ref, mx_ref, idx):
    rows = s_ref.shape[1] // SCORE_SPLIT
    mx = None
    for r in range(SCORE_SPLIT):
        s = score_rows(r * rows, rows)
        s_ref[idx, r * rows:(r + 1) * rows, :] = s
        part = jnp.max(s, axis=0, keepdims=True)
        mx = part if mx is None else jnp.maximum(mx, part)
    mx_ref[idx] = mx


def _softmax_step(s_ref, mx_ref, m_ref, alpha_ref, p_ref, idx):
    m_prev = m_ref[idx]
    m_new = jnp.maximum(m_prev, mx_ref[idx])
    p_ref[idx] = jnp.exp2(s_ref[idx] - m_new).astype(BF16)
    alpha_ref[idx] = jnp.exp2(m_prev - m_new)
    m_ref[idx] = m_new


def _with_ones_rows(v):
    return jnp.concatenate([v, jnp.ones((ONES_ROWS, v.shape[1]), v.dtype)], axis=0)


def _run_pipeline(qi, n_q, tail, produce, produce_next, consume, pv, close_previous, open_tile):
    n_tail = len(tail)

    def kinds_of(q_tile):
        return (("far",) * (q_tile + 1) + tail)[-(q_tile + 1):] if q_tile + 1 > n_tail else tail[n_tail - 1 - q_tile:]

    def first_kind(q_tile):
        return "far" if q_tile >= n_tail else tail[n_tail - 1 - q_tile]

    def first_step(produce_following, has_previous=True):
        if has_previous:
            pv(qi - 1)
            close_previous()
        open_tile()
        consume(0)
        produce_following()

    def step(j, produce_following):
        pv(j - 1)
        consume(j)
        produce_following()

    def rest(first_tile, kinds, following_first_kind):
        j = first_tile
        for kind in kinds:
            step(j, functools.partial(produce, j + 1, kind))
            j = j + 1
        step(j, functools.partial(produce_next, following_first_kind))

    @pl.when(qi == 0)
    def _():
        produce(0, first_kind(0))
        first_step(functools.partial(produce_next, first_kind(1)), has_previous=False)

    for small in range(1, n_tail + 1):
        @pl.when(qi == small)
        def _(small=small):
            kinds = kinds_of(small)
            first_step(functools.partial(produce, 1, kinds[1]))
            rest(1, kinds[2:], first_kind(small + 1))

    @pl.when(jnp.logical_and(qi > n_tail, qi < n_q))
    def _():
        first_step(functools.partial(produce, 1, "far"))

        def far_step(j, carry):
            step(j, functools.partial(produce, j + 1, "far"))
            return carry
        lax.fori_loop(1, qi - n_tail, far_step, 0)
        rest(qi - n_tail, tail, "far")

    @pl.when(qi == n_q)
    def _():
        pv(qi - 1)
        close_previous()


def _pv_step(v, p_ref, alpha_ref, acc_ref, idx):
    acc_ref[idx] = alpha_ref[idx] * acc_ref[idx] + jnp.dot(
        v, p_ref[idx], preferred_element_type=F32)


def _reset_softmax_state(m_ref, acc_ref):
    m_ref[...] = jnp.full(m_ref.shape, NEG_INF, F32)
    acc_ref[...] = jnp.zeros_like(acc_ref)


def _pipelined_tile_specs(rows, t, n_q):
    def spec(shift):
        return pl.BlockSpec((1, rows, t), lambda bi, hg, qi: (bi, hg, jnp.clip(qi + shift, 0, n_q - 1)))
    return spec(0), spec(1), spec(-1)


def _aug_rows(rows8, shape):
    row = lax.broadcasted_iota(jnp.int32, shape, 0)
    out = jnp.zeros(shape, F32)
    for r, val in enumerate(rows8):
        out = jnp.where(row == r, val, out)
    return out


def _fox_attn_kernel(q_ref, qn_ref, k_ref, v_ref, c_ref, g_ref, o_ref,
                     kaug_ref, vaug_ref, m_ref, alpha_ref, mx_ref, acc_ref, s_ref, p_ref,
                     *, hp, n_chunks):
    qi = pl.program_id(2)
    t = q_ref.shape[-1]
    pad = jnp.zeros((2 * HEAD_DIM - HEAD_DIM - 8, t), F32)

    @pl.when(qi == 0)
    def _():
        def prep(j, carry):
            for h in range(hp):
                kt = k_ref[0, j, h * HEAD_DIM:(h + 1) * HEAD_DIM, :].astype(F32)
                hi, mid, lo = _split3(c_ref[0, j, 0, h:h + 1, :])
                aug = _aug_rows([-hi, -mid, -lo, 1.0, 1.0, 1.0], (8, t))
                kaug_ref[h, j] = jnp.concatenate([kt, aug, pad], axis=0).T.astype(BF16)
                vaug_ref[h, j] = _with_ones_rows(v_ref[0, j, h * HEAD_DIM:(h + 1) * HEAD_DIM, :])
            return carry
        lax.fori_loop(0, n_chunks, prep, 0)

    def augmented_queries(queries_ref, q_tile):
        qa = []
        for h in range(hp):
            qf = queries_ref[0, h * HEAD_DIM:(h + 1) * HEAD_DIM, :].astype(F32)
            hi, mid, lo = _split3(c_ref[0, q_tile, 0, h:h + 1, :])
            aug = _aug_rows([1.0, 1.0, 1.0, hi, mid, lo], (8, t))
            qa.append(jnp.concatenate([qf, aug, pad], axis=0).astype(BF16))
        return qa

    qa_this = augmented_queries(q_ref, jnp.minimum(qi, n_chunks - 1))

    def produce(j, kind, qa=qa_this):
        for h in range(hp):
            def score_rows(r0, n, h=h):
                s = jnp.dot(kaug_ref[h, j, r0:r0 + n, :], qa[h], preferred_element_type=F32)
                if kind == "diag":
                    kpos = r0 + lax.broadcasted_iota(jnp.int32, (n, t), 0)
                    qpos = lax.broadcasted_iota(jnp.int32, (n, t), 1)
                    s = jnp.where(kpos <= qpos, s, NEG_INF)
                return s
            _store_scores(score_rows, s_ref, mx_ref, h)

    def produce_next(kind):
        produce(0, kind, augmented_queries(qn_ref, jnp.minimum(qi + 1, n_chunks - 1)))

    def consume(j):
        for h in range(hp):
            _softmax_step(s_ref, mx_ref, m_ref, alpha_ref, p_ref, h)

    def pv(j):
        for h in range(hp):
            _pv_step(vaug_ref[h, j], p_ref, alpha_ref, acc_ref, h)

    def close_previous():
        for h in range(hp):
            o = acc_ref[h, 0:HEAD_DIM] * (1.0 / acc_ref[h, HEAD_DIM:HEAD_DIM + 1])
            g = g_ref[0, h * HEAD_DIM:(h + 1) * HEAD_DIM, :].astype(F32)
            o_ref[0, h * HEAD_DIM:(h + 1) * HEAD_DIM, :] = (o * _silu(g)).astype(BF16)

    _run_pipeline(qi, n_chunks, ("diag",), produce, produce_next, consume, pv, close_previous,
                  functools.partial(_reset_softmax_state, m_ref, acc_ref))


def _fox_attn(q_t, k_c, v_c, c_c, g_t):
    b, width, s = q_t.shape
    t = ATTN_TILE
    hp = FOX_HEADS_PER_STEP
    n_heads = width // HEAD_DIM
    n_chunks = s // t
    rows = hp * HEAD_DIM
    c_c = c_c.reshape(b, n_chunks, n_heads // hp, hp, t)
    tile, next_tile, prev_tile = _pipelined_tile_specs(rows, t, n_chunks)
    full = pl.BlockSpec((1, n_chunks, rows, t), lambda bi, hg, qi: (bi, 0, hg, 0))
    return pl.pallas_call(
        functools.partial(_fox_attn_kernel, hp=hp, n_chunks=n_chunks),
        grid=(b, n_heads // hp, n_chunks + 1),
        in_specs=[
            tile, next_tile, full, full,
            pl.BlockSpec((1, n_chunks, 1, hp, t), lambda bi, hg, qi: (bi, 0, hg, 0, 0)),
            prev_tile,
        ],
        out_specs=prev_tile,
        out_shape=jax.ShapeDtypeStruct((b, width, s), BF16),
        scratch_shapes=[
            pltpu.VMEM((hp, n_chunks, t, 2 * HEAD_DIM), BF16),
            pltpu.VMEM((hp, n_chunks, HEAD_DIM + ONES_ROWS, t), BF16),
            pltpu.VMEM((hp, 1, t), F32),
            pltpu.VMEM((hp, 1, t), F32),
            pltpu.VMEM((hp, 1, t), F32),
            pltpu.VMEM((hp, HEAD_DIM + ONES_ROWS, t), F32),
            pltpu.VMEM((hp, t, t), F32),
            pltpu.VMEM((hp, t, t), BF16),
        ],
        compiler_params=_params(3, _ATTN_FLAGS),
        name="fox_attn",
    )(q_t, q_t, k_c, v_c, c_c, g_t)


def _mid_kernel(o_ref, x_ref, wo_ref, gkv_ref, gb_ref, wkv_ref, wb_ref, gk_ref, gq_ref,
                h_ref, k_ref, v_ref, q_ref, g_ref, *, n_heads):
    half = n_heads * HEAD_DIM
    h1 = x_ref[0] + lax.dot_general(o_ref[0], wo_ref[...], _TN, preferred_element_type=F32)
    h_ref[0] = h1
    ms = jnp.mean(h1 * h1, axis=-1, keepdims=True)
    hn = h1 * lax.rsqrt(ms + RMS_EPS)
    u_kv = (hn * gkv_ref[...]).astype(BF16)
    u_b = (hn * gb_ref[...]).astype(BF16)

    def pack_heads(ref, a, b2):
        for h in range(n_heads):
            sl = slice(h * HEAD_DIM, (h + 1) * HEAD_DIM)
            ref[..., h * 2 * HEAD_DIM:h * 2 * HEAD_DIM + HEAD_DIM, :] = a[sl]
            ref[..., h * 2 * HEAD_DIM + HEAD_DIM:(h + 1) * 2 * HEAD_DIM, :] = b2[sl]

    kv = lax.dot_general(wkv_ref[...], u_kv, _NT, preferred_element_type=F32)
    k1 = _head_norm(kv[0:half], gk_ref[0], n_heads, 1.0).astype(BF16)
    k2 = _head_norm(kv[half:2 * half], gk_ref[1], n_heads, 1.0).astype(BF16)
    pack_heads(k_ref.at[0, 0], k1, k2)
    v_ref[0, 0] = kv[2 * half:].astype(BF16)

    qg = lax.dot_general(wb_ref[...], u_b, _NT, preferred_element_type=F32)
    q1 = _head_norm(qg[0:half], gq_ref[0], n_heads, QK_SCALE).astype(BF16)
    q2 = _head_norm(qg[half:2 * half], gq_ref[1], n_heads, QK_SCALE).astype(BF16)
    pack_heads(q_ref.at[0], q1, q2)
    g_ref[0] = qg[2 * half:].astype(BF16)


def _mid(o_t, x, w_out, kv_norm, kv_w, kv_k_norm, b_norm, b_w_in, b_q_norm):
    b, s, d = x.shape
    tm = ROW_TILE
    n_heads = kv_w.shape[1] // (4 * HEAD_DIM)
    width = 2 * n_heads * HEAD_DIM
    const2 = lambda *_: (0, 0)
    const3 = lambda *_: (0, 0, 0)
    fm = pl.BlockSpec((1, width, tm), lambda bi, i: (bi, 0, i))
    tok = pl.BlockSpec((1, tm, d), lambda bi, i: (bi, i, 0))
    chunked = pl.BlockSpec((1, 1, width, tm), lambda bi, i: (bi, i, 0, 0))
    return pl.pallas_call(
        functools.partial(_mid_kernel, n_heads=n_heads),
        grid=(b, s // tm),
        in_specs=[
            pl.BlockSpec((1, o_t.shape[1], tm), lambda bi, i: (bi, 0, i)),
            tok,
            pl.BlockSpec(w_out.shape, const2),
            pl.BlockSpec((1, d), const2),
            pl.BlockSpec((1, d), const2),
            pl.BlockSpec((2 * width, d), const2),
            pl.BlockSpec((2 * width, d), const2),
            pl.BlockSpec((2, HEAD_DIM, 1), const3),
            pl.BlockSpec((2, HEAD_DIM, 1), const3),
        ],
        out_specs=[tok, chunked, chunked, fm, fm],
        out_shape=[
            jax.ShapeDtypeStruct((b, s, d), F32),
            jax.ShapeDtypeStruct((b, s // tm, width, tm), BF16),
            jax.ShapeDtypeStruct((b, s // tm, width, tm), BF16),
            jax.ShapeDtypeStruct((b, width, s), BF16),
            jax.ShapeDtypeStruct((b, width, s), BF16),
        ],
        compiler_params=_params(2),
        name="mid_proj",
    )(o_t, x, w_out.astype(BF16), kv_norm.reshape(1, d), b_norm.reshape(1, d),
      kv_w.T.astype(BF16), b_w_in.T.astype(BF16),
      kv_k_norm.reshape(2, HEAD_DIM, 1), b_q_norm.reshape(2, HEAD_DIM, 1))


def _diff_attn_kernel(q_ref, qn_ref, k_ref, v_ref, brow_ref, lam_ref, gs_ref, g_ref, o_ref,
                      ktok_ref, vaug_ref, bias_ref, m_ref, alpha_ref, mx_ref, acc_ref, s_ref, p_ref,
                      *, hd, n_chunks):
    qi = pl.program_id(2)
    t = q_ref.shape[-1]
    dv = DIFF_V_DIM
    streams = [(h, mp) for h in range(hd) for mp in range(2)]

    @pl.when(qi == 0)
    def _():
        def prep(j, carry):
            for h in range(hd):
                ktok_ref[h, j] = k_ref[0, j, h * dv:(h + 1) * dv, :].astype(F32).T.astype(BF16)
                vaug_ref[h, j] = _with_ones_rows(v_ref[0, j, h * dv:(h + 1) * dv, :])
            return carry
        lax.fori_loop(0, n_chunks, prep, 0)
        for h, mp in streams:
            for off in range(2):
                rows = jnp.broadcast_to(brow_ref[h, mp, off], (t, 2 * t))
                bias_ref[h, mp, off] = pltpu.roll(rows, 0, 1, stride=1, stride_axis=0)[:, 0:t]

    def map_queries(queries_ref):
        qa = []
        for h in range(hd):
            q = queries_ref[0, h * dv:(h + 1) * dv, :]
            row = lax.broadcasted_iota(jnp.int32, q.shape, 0)
            zero = jnp.zeros_like(q)
            qa += [jnp.where(row < HEAD_DIM, q, zero), jnp.where(row >= HEAD_DIM, q, zero)]
        return qa

    qa_this = map_queries(q_ref)

    def produce(j, kind, qa=qa_this):
        for i, (h, mp) in enumerate(streams):
            def score_rows(r0, n, i=i, h=h, mp=mp):
                s = jnp.dot(ktok_ref[h, j, r0:r0 + n, :], qa[i], preferred_element_type=F32)
                if kind == "near":
                    s = s + bias_ref[h, mp, 1, r0:r0 + n, :]
                elif kind == "diag":
                    kchunk = (r0 + lax.broadcasted_iota(jnp.int32, (n, t), 0)) // CHUNK
                    qchunk = lax.broadcasted_iota(jnp.int32, (n, t), 1) // CHUNK
                    s = jnp.where(kchunk <= qchunk, s + bias_ref[h, mp, 0, r0:r0 + n, :], NEG_INF)
                return s
            _store_scores(score_rows, s_ref, mx_ref, i)

    def consume(j):
        for i in range(len(streams)):
            _softmax_step(s_ref, mx_ref, m_ref, alpha_ref, p_ref, i)

    def pv(j):
        for i, (h, mp) in enumerate(streams):
            _pv_step(vaug_ref[h, j], p_ref, alpha_ref, acc_ref, i)

    def produce_next(kind):
        produce(0, kind, map_queries(qn_ref))

    def close_previous():
        lam_p = lam_ref[...]
        e1 = jnp.exp(jnp.sum(lam_p[0:1] * lam_p[1:2], axis=1, keepdims=True))
        e2 = jnp.exp(jnp.sum(lam_p[2:3] * lam_p[3:4], axis=1, keepdims=True))
        lam = e1 - e2 + LAMBDA_INIT
        for h in range(hd):
            a1, a2 = acc_ref[2 * h], acc_ref[2 * h + 1]
            o = a1[0:dv] * (1.0 / a1[dv:dv + 1]) - lam * (a2[0:dv] * (1.0 / a2[dv:dv + 1]))
            ms = jnp.mean(o * o, axis=0, keepdims=True)
            y = o * lax.rsqrt(ms + RMS_EPS) * gs_ref[...] * (1.0 - LAMBDA_INIT)
            g = g_ref[0, h * dv:(h + 1) * dv, :].astype(F32)
            o_ref[0, h * dv:(h + 1) * dv, :] = (y * _silu(g)).astype(BF16)

    _run_pipeline(qi, n_chunks, ("near", "diag"), produce, produce_next, consume, pv, close_previous,
                  functools.partial(_reset_softmax_state, m_ref, acc_ref))


def _t5_bucket(rel):
    half = NUM_BUCKETS // 2
    max_exact = half // 2
    ret = jnp.where(rel > 0, half, 0)
    n = jnp.abs(rel)
    n_f = jnp.maximum(n, 1).astype(jnp.float32)
    large = max_exact + (jnp.log(n_f / max_exact) / math.log(MAX_DISTANCE / max_exact)
                         * (half - max_exact)).astype(jnp.int32)
    large = jnp.minimum(large, half - 1)
    return ret + jnp.where(n < max_exact, n, large)


def _bias_rows(rel_bias, t):
    dw = jnp.arange(2 * t)
    d = jnp.where(dw < t, dw, dw - 2 * t)
    rel = jnp.stack([-d, -d - t])
    rows = rel_bias[_t5_bucket(rel)].astype(F32)
    far = rel_bias[_t5_bucket(jnp.int32(-MAX_DISTANCE))].astype(F32)
    return ((rows - far) * LOG2E).transpose(3, 2, 0, 1)[:, :, :, None, :]


def _diff_attn(q_t, k_c, v_c, g_t, rel_bias, lam_params, g_sub):
    b, width, s = q_t.shape
    t = ATTN_TILE
    n_heads = width // DIFF_V_DIM
    n_chunks = s // t
    hd = DIFF_HEADS_PER_STEP
    rows = hd * DIFF_V_DIM
    n_streams = 2 * hd
    tile, next_tile, prev_tile = _pipelined_tile_specs(rows, t, n_chunks)
    full = pl.BlockSpec((1, n_chunks, rows, t), lambda bi, hg, qi: (bi, 0, hg, 0))
    return pl.pallas_call(
        functools.partial(_diff_attn_kernel, hd=hd, n_chunks=n_chunks),
        grid=(b, n_heads // hd, n_chunks + 1),
        in_specs=[
            tile, next_tile, full, full,
            pl.BlockSpec((hd, 2, 2, 1, 2 * t), lambda bi, hg, qi: (hg, 0, 0, 0, 0)),
            pl.BlockSpec((4, HEAD_DIM), lambda *_: (0, 0)),
            pl.BlockSpec((DIFF_V_DIM, 1), lambda *_: (0, 0)),
            prev_tile,
        ],
        out_specs=prev_tile,
        out_shape=jax.ShapeDtypeStruct((b, width, s), BF16),
        scratch_shapes=[
            pltpu.VMEM((hd, n_chunks, t, DIFF_V_DIM), BF16),
            pltpu.VMEM((hd, n_chunks, DIFF_V_DIM + ONES_ROWS, t), BF16),
            pltpu.VMEM((hd, 2, 2, t, t), F32),
            pltpu.VMEM((n_streams, 1, t), F32),
            pltpu.VMEM((n_streams, 1, t), F32),
            pltpu.VMEM((n_streams, 1, t), F32),
            pltpu.VMEM((n_streams, DIFF_V_DIM + ONES_ROWS, t), F32),
            pltpu.VMEM((n_streams, t, t), F32),
            pltpu.VMEM((n_streams, t, t), BF16),
        ],
        compiler_params=_params(3, _ATTN_FLAGS),
        name="diff_attn",
    )(q_t, q_t, k_c, v_c, _bias_rows(rel_bias, t), lam_params, g_sub.reshape(DIFF_V_DIM, 1), g_t)


def _out_kernel(o_ref, h_ref, w_ref, y_ref):
    y_ref[0] = h_ref[0] + lax.dot_general(o_ref[0], w_ref[...], _TN, preferred_element_type=F32)


def _out_proj(o_t, h1, w_out):
    b, s, d = h1.shape
    tm = ROW_TILE
    tok = pl.BlockSpec((1, tm, d), lambda bi, i: (bi, i, 0))
    return pl.pallas_call(
        _out_kernel,
        grid=(b, s // tm),
        in_specs=[
            pl.BlockSpec((1, o_t.shape[1], tm), lambda bi, i: (bi, 0, i)),
            tok,
            pl.BlockSpec(w_out.shape, lambda *_: (0, 0)),
        ],
        out_specs=tok,
        out_shape=jax.ShapeDtypeStruct((b, s, d), F32),
        compiler_params=_params(2),
        name="out_proj",
    )(o_t, h1, w_out.astype(BF16))


def kernel(x, a_norm, a_w_in, a_b_f, a_q_norm, a_k_norm, a_w_out, kv_norm, kv_w, kv_k_norm,
           rel_bias, b_norm, b_w_in, b_q_norm, b_lam_q1, b_lam_k1, b_lam_q2, b_lam_k2,
           b_sub_norm, b_w_out):
    assert a_norm.shape[0] == 1 and b_norm.shape[0] == 1
    assert x.shape[1] % ATTN_TILE == 0 and ATTN_TILE == ROW_TILE
    q_t, k_c, v_c, g_t, c_c = _fox_proj(x, a_norm[0], a_w_in[0], a_b_f[0], a_q_norm[0], a_k_norm[0])
    o_t = _fox_attn(q_t, k_c, v_c, c_c, g_t)
    h1, k2_c, v2_c, q2_t, g2_t = _mid(o_t, x, a_w_out[0], kv_norm, kv_w, kv_k_norm,
                                      b_norm[0], b_w_in[0], b_q_norm[0])
    lam_params = jnp.stack([b_lam_q1[0], b_lam_k1[0], b_lam_q2[0], b_lam_k2[0]])
    o2_t = _diff_attn(q2_t, k2_c, v2_c, g2_t, rel_bias, lam_params, b_sub_norm[0])
    return _out_proj(o2_t, h1, b_w_out[0])
```

```python
import functools
import math

import jax
import jax.numpy as jnp
from jax import lax
from jax.experimental import pallas as pl
from jax.experimental.pallas import tpu as pltpu

HEAD_DIM = 64
DIFF_V_DIM = 2 * HEAD_DIM
CHUNK = 64
NUM_BUCKETS = 32
MAX_DISTANCE = 128
RMS_EPS = 1e-6
NEG_INF = -1e30
LOG2E = math.log2(math.e)
QK_SCALE = HEAD_DIM ** -0.5 * LOG2E
LAMBDA_INIT = 0.8 - 0.6 * math.exp(-0.3 * 1)

ROW_TILE = 512
ATTN_TILE = 512
FOX_HEADS_PER_STEP = 2
DIFF_HEADS_PER_STEP = 1
ONES_ROWS = 16
SCORE_SPLIT = 1
V7X_VMEM_LIMIT_BYTES = 56 * 1024 * 1024

F32 = jnp.float32
BF16 = jnp.bfloat16
_NT = (((1,), (1,)), ((), ()))
_TN = (((0,), (0,)), ((), ()))


def _params(n_axes, flags=None):
    return pltpu.CompilerParams(
        dimension_semantics=("arbitrary",) * n_axes,
        vmem_limit_bytes=V7X_VMEM_LIMIT_BYTES, flags=flags)


_ATTN_FLAGS = None


def _head_norm(p_t, g_col, n_heads, scale):
    t = p_t.shape[-1]
    p3 = p_t.reshape(n_heads, HEAD_DIM, t)
    ms = jnp.mean(p3 * p3, axis=1, keepdims=True)
    y = p3 * lax.rsqrt(ms + RMS_EPS) * (g_col * scale)[None]
    return y.reshape(n_heads * HEAD_DIM, t)


def _split3(c):
    hi = c.astype(BF16).astype(F32)
    r = c - hi
    mid = r.astype(BF16).astype(F32)
    lo = (r - mid).astype(BF16).astype(F32)
    return hi, mid, lo


def _silu(g):
    return g / (1.0 + jnp.exp(-g))


def _fox_proj_kernel(x_ref, gn_ref, wq_ref, wk_ref, wv_ref, wg_ref, wf_ref, bf_ref,
                     gq_ref, gk_ref, q_ref, k_ref, v_ref, g_ref, c_ref, carry_ref,
                     *, n_heads):
    @pl.when(pl.program_id(1) == 0)
    def _():
        carry_ref[...] = jnp.zeros_like(carry_ref)

    x = x_ref[0]
    tm = x.shape[0]
    ms = jnp.mean(x * x, axis=-1, keepdims=True)
    u = (x * lax.rsqrt(ms + RMS_EPS) * gn_ref[...]).astype(BF16)

    def proj(w_ref):
        return lax.dot_general(w_ref[...], u, _NT, preferred_element_type=F32)

    q_ref[0] = _head_norm(proj(wq_ref), gq_ref[...], n_heads, QK_SCALE).astype(BF16)
    k_ref[0, 0] = _head_norm(proj(wk_ref), gk_ref[...], n_heads, 1.0).astype(BF16)
    v_ref[0, 0] = proj(wv_ref).astype(BF16)
    g_ref[0] = proj(wg_ref).astype(BF16)

    z = proj(wf_ref) + bf_ref[...]
    log_f = jnp.minimum(z, 0.0) - jnp.log1p(jnp.exp(-jnp.abs(z)))
    lane = lax.broadcasted_iota(jnp.int32, log_f.shape, 1)
    cs = log_f
    shift = 1
    while shift < tm:
        cs = cs + jnp.where(lane >= shift, pltpu.roll(cs, shift, axis=1), 0.0)
        shift *= 2
    c = cs + carry_ref[:, 0:1]
    c_ref[0, 0] = c * LOG2E
    carry_ref[...] = jnp.broadcast_to(c[:, tm - 1:tm], carry_ref.shape)


def _fox_proj(x, a_norm, w_in, b_f, g_q, g_k):
    b, s, d = x.shape
    n_heads = b_f.shape[0]
    width = n_heads * HEAD_DIM
    tm = ROW_TILE
    w_t = w_in.T.astype(BF16)
    wq, wk, wv, wg, wf = (w_t[0:width], w_t[width:2 * width], w_t[2 * width:3 * width],
                          w_t[3 * width:4 * width], w_t[4 * width:])
    const = lambda *_: (0, 0)
    w_spec = pl.BlockSpec((width, d), const)
    col = pl.BlockSpec((HEAD_DIM, 1), const)
    return pl.pallas_call(
        functools.partial(_fox_proj_kernel, n_heads=n_heads),
        grid=(b, s // tm),
        in_specs=[
            pl.BlockSpec((1, tm, d), lambda bi, i: (bi, i, 0)),
            pl.BlockSpec((1, d), const),
            w_spec, w_spec, w_spec, w_spec,
            pl.BlockSpec((n_heads, d), const),
            pl.BlockSpec((n_heads, 1), const),
            col, col,
        ],
        out_specs=[
            pl.BlockSpec((1, width, tm), lambda bi, i: (bi, 0, i)),
            pl.BlockSpec((1, 1, width, tm), lambda bi, i: (bi, i, 0, 0)),
            pl.BlockSpec((1, 1, width, tm), lambda bi, i: (bi, i, 0, 0)),
            pl.BlockSpec((1, width, tm), lambda bi, i: (bi, 0, i)),
            pl.BlockSpec((1, 1, n_heads, tm), lambda bi, i: (bi, i, 0, 0)),
        ],
        out_shape=[
            jax.ShapeDtypeStruct((b, width, s), BF16),
            jax.ShapeDtypeStruct((b, s // tm, width, tm), BF16),
            jax.ShapeDtypeStruct((b, s // tm, width, tm), BF16),
            jax.ShapeDtypeStruct((b, width, s), BF16),
            jax.ShapeDtypeStruct((b, s // tm, n_heads, tm), F32),
        ],
        scratch_shapes=[pltpu.VMEM((n_heads, 128), F32)],
        compiler_params=_params(2),
        name="fox_proj",
    )(x, a_norm.reshape(1, d), wq, wk, wv, wg, wf, b_f.reshape(n_heads, 1),
      g_q.reshape(HEAD_DIM, 1), g_k.reshape(HEAD_DIM, 1))


def _store_scores(score_rows, s_ref, mx_ref, idx):
    rows = s_ref.shape[1] // SCORE_SPLIT
    mx = None
    for r in range(SCORE_SPLIT):
        s = score_rows(r * rows, rows)
        s_ref[idx, r * rows:(r + 1) * rows, :] = s
        part = jnp.max(s, axis=0, keepdims=True)
        mx = part if mx is None else jnp.maximum(mx, part)
    mx_ref[idx] = mx


def _softmax_step(s_ref, mx_ref, m_ref, alpha_ref, p_ref, idx):
    m_prev = m_ref[idx]
    m_new = jnp.maximum(m_prev, mx_ref[idx])
    p_ref[idx] = jnp.exp2(s_ref[idx] - m_new).astype(BF16)
    alpha_ref[idx] = jnp.exp2(m_prev - m_new)
    m_ref[idx] = m_new


def _with_ones_rows(v):
    return jnp.concatenate([v, jnp.ones((ONES_ROWS, v.shape[1]), v.dtype)], axis=0)


def _run_pipeline(qi, n_q, tail, produce, produce_next, consume, pv, close_previous, open_tile):
    n_tail = len(tail)

    def kinds_of(q_tile):
        return (("far",) * (q_tile + 1) + tail)[-(q_tile + 1):] if q_tile + 1 > n_tail else tail[n_tail - 1 - q_tile:]

    def first_kind(q_tile):
        return "far" if q_tile >= n_tail else tail[n_tail - 1 - q_tile]

    def first_step(produce_following, has_previous=True):
        if has_previous:
            pv(qi - 1)
            close_previous()
        open_tile()
        consume(0)
        produce_following()

    def step(j, produce_following):
        pv(j - 1)
        consume(j)
        produce_following()

    def rest(first_tile, kinds, following_first_kind):
        j = first_tile
        for kind in kinds:
            step(j, functools.partial(produce, j + 1, kind))
            j = j + 1
        step(j, functools.partial(produce_next, following_first_kind))

    @pl.when(qi == 0)
    def _():
        produce(0, first_kind(0))
        first_step(functools.partial(produce_next, first_kind(1)), has_previous=False)

    for small in range(1, n_tail + 1):
        @pl.when(qi == small)
        def _(small=small):
            kinds = kinds_of(small)
            first_step(functools.partial(produce, 1, kinds[1]))
            rest(1, kinds[2:], first_kind(small + 1))

    @pl.when(jnp.logical_and(qi > n_tail, qi < n_q))
    def _():
        first_step(functools.partial(produce, 1, "far"))

        n_far = qi - n_tail - 1

        def far_pair(i, carry):
            for j in (1 + 2 * i, 2 + 2 * i):
                step(j, functools.partial(produce, j + 1, "far"))
            return carry
        lax.fori_loop(0, n_far // 2, far_pair, 0)

        @pl.when(n_far % 2 == 1)
        def _():
            step(n_far, functools.partial(produce, n_far + 1, "far"))
        rest(qi - n_tail, tail, "far")

    @pl.when(qi == n_q)
    def _():
        pv(qi - 1)
        close_previous()


def _pv_step(v, p_ref, alpha_ref, acc_ref, idx):
    acc_ref[idx] = alpha_ref[idx] * acc_ref[idx] + jnp.dot(
        v, p_ref[idx], preferred_element_type=F32)


def _reset_softmax_state(m_ref, acc_ref):
    m_ref[...] = jnp.full(m_ref.shape, NEG_INF, F32)
    acc_ref[...] = jnp.zeros_like(acc_ref)


def _pipelined_tile_specs(rows, t, n_q):
    def spec(shift):
        return pl.BlockSpec((1, rows, t), lambda bi, hg, qi: (bi, hg, jnp.clip(qi + shift, 0, n_q - 1)))
    return spec(0), spec(1), spec(-1)


def _aug_rows(rows8, shape):
    row = lax.broadcasted_iota(jnp.int32, shape, 0)
    out = jnp.zeros(shape, F32)
    for r, val in enumerate(rows8):
        out = jnp.where(row == r, val, out)
    return out


def _fox_attn_kernel(q_ref, qn_ref, k_ref, v_ref, c_ref, g_ref, o_ref,
                     kaug_ref, vaug_ref, m_ref, alpha_ref, mx_ref, acc_ref, s_ref, p_ref,
                     *, hp, n_chunks):
    qi = pl.program_id(2)
    t = q_ref.shape[-1]
    pad = jnp.zeros((2 * HEAD_DIM - HEAD_DIM - 8, t), F32)

    @pl.when(qi == 0)
    def _():
        def prep(j, carry):
            for h in range(hp):
                kt = k_ref[0, j, h * HEAD_DIM:(h + 1) * HEAD_DIM, :].astype(F32)
                hi, mid, lo = _split3(c_ref[0, j, 0, h:h + 1, :])
                aug = _aug_rows([-hi, -mid, -lo, 1.0, 1.0, 1.0], (8, t))
                kaug_ref[h, j] = jnp.concatenate([kt, aug, pad], axis=0).T.astype(BF16)
                vaug_ref[h, j] = _with_ones_rows(v_ref[0, j, h * HEAD_DIM:(h + 1) * HEAD_DIM, :])
            return carry
        lax.fori_loop(0, n_chunks, prep, 0)

    def augmented_queries(queries_ref, q_tile):
        qa = []
        for h in range(hp):
            qf = queries_ref[0, h * HEAD_DIM:(h + 1) * HEAD_DIM, :].astype(F32)
            hi, mid, lo = _split3(c_ref[0, q_tile, 0, h:h + 1, :])
            aug = _aug_rows([1.0, 1.0, 1.0, hi, mid, lo], (8, t))
            qa.append(jnp.concatenate([qf, aug, pad], axis=0).astype(BF16))
        return qa

    qa_this = augmented_queries(q_ref, jnp.minimum(qi, n_chunks - 1))

    def produce(j, kind, qa=qa_this):
        for h in range(hp):
            def score_rows(r0, n, h=h):
                s = jnp.dot(kaug_ref[h, j, r0:r0 + n, :], qa[h], preferred_element_type=F32)
                if kind == "diag":
                    kpos = r0 + lax.broadcasted_iota(jnp.int32, (n, t), 0)
                    qpos = lax.broadcasted_iota(jnp.int32, (n, t), 1)
                    s = jnp.where(kpos <= qpos, s, NEG_INF)
                return s
            _store_scores(score_rows, s_ref, mx_ref, h)

    def produce_next(kind):
        produce(0, kind, augmented_queries(qn_ref, jnp.minimum(qi + 1, n_chunks - 1)))

    def consume(j):
        for h in range(hp):
            _softmax_step(s_ref, mx_ref, m_ref, alpha_ref, p_ref, h)

    def pv(j):
        for h in range(hp):
            _pv_step(vaug_ref[h, j], p_ref, alpha_ref, acc_ref, h)

    def close_previous():
        for h in range(hp):
            o = acc_ref[h, 0:HEAD_DIM] * (1.0 / acc_ref[h, HEAD_DIM:HEAD_DIM + 1])
            g = g_ref[0, h * HEAD_DIM:(h + 1) * HEAD_DIM, :].astype(F32)
            o_ref[0, h * HEAD_DIM:(h + 1) * HEAD_DIM, :] = (o * _silu(g)).astype(BF16)

    _run_pipeline(qi, n_chunks, ("diag",), produce, produce_next, consume, pv, close_previous,
                  functools.partial(_reset_softmax_state, m_ref, acc_ref))


def _fox_attn(q_t, k_c, v_c, c_c, g_t):
    b, width, s = q_t.shape
    t = ATTN_TILE
    hp = FOX_HEADS_PER_STEP
    n_heads = width // HEAD_DIM
    n_chunks = s // t
    rows = hp * HEAD_DIM
    c_c = c_c.reshape(b, n_chunks, n_heads // hp, hp, t)
    tile, next_tile, prev_tile = _pipelined_tile_specs(rows, t, n_chunks)
    full = pl.BlockSpec((1, n_chunks, rows, t), lambda bi, hg, qi: (bi, 0, hg, 0))
    return pl.pallas_call(
        functools.partial(_fox_attn_kernel, hp=hp, n_chunks=n_chunks),
        grid=(b, n_heads // hp, n_chunks + 1),
        in_specs=[
            tile, next_tile, full, full,
            pl.BlockSpec((1, n_chunks, 1, hp, t), lambda bi, hg, qi: (bi, 0, hg, 0, 0)),
            prev_tile,
        ],
        out_specs=prev_tile,
        out_shape=jax.ShapeDtypeStruct((b, width, s), BF16),
        scratch_shapes=[
            pltpu.VMEM((hp, n_chunks, t, 2 * HEAD_DIM), BF16),
            pltpu.VMEM((hp, n_chunks, HEAD_DIM + ONES_ROWS, t), BF16),
            pltpu.VMEM((hp, 1, t), F32),
            pltpu.VMEM((hp, 1, t), F32),
            pltpu.VMEM((hp, 1, t), F32),
            pltpu.VMEM((hp, HEAD_DIM + ONES_ROWS, t), F32),
            pltpu.VMEM((hp, t, t), F32),
            pltpu.VMEM((hp, t, t), BF16),
        ],
        compiler_params=_params(3, _ATTN_FLAGS),
        name="fox_attn",
    )(q_t, q_t, k_c, v_c, c_c, g_t)


def _mid_kernel(o_ref, x_ref, wo_ref, gkv_ref, gb_ref, wkv_ref, wb_ref, gk_ref, gq_ref,
                h_ref, k_ref, v_ref, q_ref, g_ref, *, n_heads):
    half = n_heads * HEAD_DIM
    h1 = x_ref[0] + lax.dot_general(o_ref[0], wo_ref[...], _TN, preferred_element_type=F32)
    h_ref[0] = h1
    ms = jnp.mean(h1 * h1, axis=-1, keepdims=True)
    hn = h1 * lax.rsqrt(ms + RMS_EPS)
    u_kv = (hn * gkv_ref[...]).astype(BF16)
    u_b = (hn * gb_ref[...]).astype(BF16)

    def pack_heads(ref, a, b2):
        for h in range(n_heads):
            sl = slice(h * HEAD_DIM, (h + 1) * HEAD_DIM)
            ref[..., h * 2 * HEAD_DIM:h * 2 * HEAD_DIM + HEAD_DIM, :] = a[sl]
            ref[..., h * 2 * HEAD_DIM + HEAD_DIM:(h + 1) * 2 * HEAD_DIM, :] = b2[sl]

    kv = lax.dot_general(wkv_ref[...], u_kv, _NT, preferred_element_type=F32)
    k1 = _head_norm(kv[0:half], gk_ref[0], n_heads, 1.0).astype(BF16)
    k2 = _head_norm(kv[half:2 * half], gk_ref[1], n_heads, 1.0).astype(BF16)
    pack_heads(k_ref.at[0, 0], k1, k2)
    v_ref[0, 0] = kv[2 * half:].astype(BF16)

    qg = lax.dot_general(wb_ref[...], u_b, _NT, preferred_element_type=F32)
    q1 = _head_norm(qg[0:half], gq_ref[0], n_heads, QK_SCALE).astype(BF16)
    q2 = _head_norm(qg[half:2 * half], gq_ref[1], n_heads, QK_SCALE).astype(BF16)
    pack_heads(q_ref.at[0], q1, q2)
    g_ref[0] = qg[2 * half:].astype(BF16)


def _mid(o_t, x, w_out, kv_norm, kv_w, kv_k_norm, b_norm, b_w_in, b_q_norm):
    b, s, d = x.shape
    tm = ROW_TILE
    n_heads = kv_w.shape[1] // (4 * HEAD_DIM)
    width = 2 * n_heads * HEAD_DIM
    const2 = lambda *_: (0, 0)
    const3 = lambda *_: (0, 0, 0)
    fm = pl.BlockSpec((1, width, tm), lambda bi, i: (bi, 0, i))
    tok = pl.BlockSpec((1, tm, d), lambda bi, i: (bi, i, 0))
    chunked = pl.BlockSpec((1, 1, width, tm), lambda bi, i: (bi, i, 0, 0))
    return pl.pallas_call(
        functools.partial(_mid_kernel, n_heads=n_heads),
        grid=(b, s // tm),
        in_specs=[
            pl.BlockSpec((1, o_t.shape[1], tm), lambda bi, i: (bi, 0, i)),
            tok,
            pl.BlockSpec(w_out.shape, const2),
            pl.BlockSpec((1, d), const2),
            pl.BlockSpec((1, d), const2),
            pl.BlockSpec((2 * width, d), const2),
            pl.BlockSpec((2 * width, d), const2),
            pl.BlockSpec((2, HEAD_DIM, 1), const3),
            pl.BlockSpec((2, HEAD_DIM, 1), const3),
        ],
        out_specs=[tok, chunked, chunked, fm, fm],
        out_shape=[
            jax.ShapeDtypeStruct((b, s, d), F32),
            jax.ShapeDtypeStruct((b, s // tm, width, tm), BF16),
            jax.ShapeDtypeStruct((b, s // tm, width, tm), BF16),
            jax.ShapeDtypeStruct((b, width, s), BF16),
            jax.ShapeDtypeStruct((b, width, s), BF16),
        ],
        compiler_params=_params(2),
        name="mid_proj",
    )(o_t, x, w_out.astype(BF16), kv_norm.reshape(1, d), b_norm.reshape(1, d),
      kv_w.T.astype(BF16), b_w_in.T.astype(BF16),
      kv_k_norm.reshape(2, HEAD_DIM, 1), b_q_norm.reshape(2, HEAD_DIM, 1))


def _diff_attn_kernel(q_ref, qn_ref, k_ref, v_ref, brow_ref, lam_ref, gs_ref, g_ref, o_ref,
                      ktok_ref, vaug_ref, bias_ref, m_ref, alpha_ref, mx_ref, acc_ref, s_ref, p_ref,
                      *, hd, n_chunks):
    qi = pl.program_id(2)
    t = q_ref.shape[-1]
    dv = DIFF_V_DIM
    streams = [(h, mp) for h in range(hd) for mp in range(2)]

    @pl.when(qi == 0)
    def _():
        def prep(j, carry):
            for h in range(hd):
                ktok_ref[h, j] = k_ref[0, j, h * dv:(h + 1) * dv, :].astype(F32).T.astype(BF16)
                vaug_ref[h, j] = _with_ones_rows(v_ref[0, j, h * dv:(h + 1) * dv, :])
            return carry
        lax.fori_loop(0, n_chunks, prep, 0)
        for h, mp in streams:
            for off in range(2):
                rows = jnp.broadcast_to(brow_ref[h, mp, off], (t, 2 * t))
                bias_ref[h, mp, off] = pltpu.roll(rows, 0, 1, stride=1, stride_axis=0)[:, 0:t]

    def map_queries(queries_ref):
        qa = []
        for h in range(hd):
            q = queries_ref[0, h * dv:(h + 1) * dv, :]
            row = lax.broadcasted_iota(jnp.int32, q.shape, 0)
            zero = jnp.zeros_like(q)
            qa += [jnp.where(row < HEAD_DIM, q, zero), jnp.where(row >= HEAD_DIM, q, zero)]
        return qa

    qa_this = map_queries(q_ref)

    def produce(j, kind, qa=qa_this):
        for i, (h, mp) in enumerate(streams):
            def score_rows(r0, n, i=i, h=h, mp=mp):
                s = jnp.dot(ktok_ref[h, j, r0:r0 + n, :], qa[i], preferred_element_type=F32)
                if kind == "near":
                    s = s + bias_ref[h, mp, 1, r0:r0 + n, :]
                elif kind == "diag":
                    kchunk = (r0 + lax.broadcasted_iota(jnp.int32, (n, t), 0)) // CHUNK
                    qchunk = lax.broadcasted_iota(jnp.int32, (n, t), 1) // CHUNK
                    s = jnp.where(kchunk <= qchunk, s + bias_ref[h, mp, 0, r0:r0 + n, :], NEG_INF)
                return s
            _store_scores(score_rows, s_ref, mx_ref, i)

    def consume(j):
        for i in range(len(streams)):
            _softmax_step(s_ref, mx_ref, m_ref, alpha_ref, p_ref, i)

    def pv(j):
        for i, (h, mp) in enumerate(streams):
            _pv_step(vaug_ref[h, j], p_ref, alpha_ref, acc_ref, i)

    def produce_next(kind):
        produce(0, kind, map_queries(qn_ref))

    def close_previous():
        lam_p = lam_ref[...]
        e1 = jnp.exp(jnp.sum(lam_p[0:1] * lam_p[1:2], axis=1, keepdims=True))
        e2 = jnp.exp(jnp.sum(lam_p[2:3] * lam_p[3:4], axis=1, keepdims=True))
        lam = e1 - e2 + LAMBDA_INIT
        for h in range(hd):
            a1, a2 = acc_ref[2 * h], acc_ref[2 * h + 1]
            o = a1[0:dv] * (1.0 / a1[dv:dv + 1]) - lam * (a2[0:dv] * (1.0 / a2[dv:dv + 1]))
            ms = jnp.mean(o * o, axis=0, keepdims=True)
            y = o * lax.rsqrt(ms + RMS_EPS) * gs_ref[...] * (1.0 - LAMBDA_INIT)
            g = g_ref[0, h * dv:(h + 1) * dv, :].astype(F32)
            o_ref[0, h * dv:(h + 1) * dv, :] = (y * _silu(g)).astype(BF16)

    _run_pipeline(qi, n_chunks, ("near", "diag"), produce, produce_next, consume, pv, close_previous,
                  functools.partial(_reset_softmax_state, m_ref, acc_ref))


def _t5_bucket(rel):
    half = NUM_BUCKETS // 2
    max_exact = half // 2
    ret = jnp.where(rel > 0, half, 0)
    n = jnp.abs(rel)
    n_f = jnp.maximum(n, 1).astype(jnp.float32)
    large = max_exact + (jnp.log(n_f / max_exact) / math.log(MAX_DISTANCE / max_exact)
                         * (half - max_exact)).astype(jnp.int32)
    large = jnp.minimum(large, half - 1)
    return ret + jnp.where(n < max_exact, n, large)


def _bias_rows(rel_bias, t):
    dw = jnp.arange(2 * t)
    d = jnp.where(dw < t, dw, dw - 2 * t)
    rel = jnp.stack([-d, -d - t])
    rows = rel_bias[_t5_bucket(rel)].astype(F32)
    far = rel_bias[_t5_bucket(jnp.int32(-MAX_DISTANCE))].astype(F32)
    return ((rows - far) * LOG2E).transpose(3, 2, 0, 1)[:, :, :, None, :]


def _diff_attn(q_t, k_c, v_c, g_t, rel_bias, lam_params, g_sub):
    b, width, s = q_t.shape
    t = ATTN_TILE
    n_heads = width // DIFF_V_DIM
    n_chunks = s // t
    hd = DIFF_HEADS_PER_STEP
    rows = hd * DIFF_V_DIM
    n_streams = 2 * hd
    tile, next_tile, prev_tile = _pipelined_tile_specs(rows, t, n_chunks)
    full = pl.BlockSpec((1, n_chunks, rows, t), lambda bi, hg, qi: (bi, 0, hg, 0))
    return pl.pallas_call(
        functools.partial(_diff_attn_kernel, hd=hd, n_chunks=n_chunks),
        grid=(b, n_heads // hd, n_chunks + 1),
        in_specs=[
            tile, next_tile, full, full,
            pl.BlockSpec((hd, 2, 2, 1, 2 * t), lambda bi, hg, qi: (hg, 0, 0, 0, 0)),
            pl.BlockSpec((4, HEAD_DIM), lambda *_: (0, 0)),
            pl.BlockSpec((DIFF_V_DIM, 1), lambda *_: (0, 0)),
            prev_tile,
        ],
        out_specs=prev_tile,
        out_shape=jax.ShapeDtypeStruct((b, width, s), BF16),
        scratch_shapes=[
            pltpu.VMEM((hd, n_chunks, t, DIFF_V_DIM), BF16),
            pltpu.VMEM((hd, n_chunks, DIFF_V_DIM + ONES_ROWS, t), BF16),
            pltpu.VMEM((hd, 2, 2, t, t), F32),
            pltpu.VMEM((n_streams, 1, t), F32),
            pltpu.VMEM((n_streams, 1, t), F32),
            pltpu.VMEM((n_streams, 1, t), F32),
            pltpu.VMEM((n_streams, DIFF_V_DIM + ONES_ROWS, t), F32),
            pltpu.VMEM((n_streams, t, t), F32),
            pltpu.VMEM((n_streams, t, t), BF16),
        ],
        compiler_params=_params(3, _ATTN_FLAGS),
        name="diff_attn",
    )(q_t, q_t, k_c, v_c, _bias_rows(rel_bias, t), lam_params, g_sub.reshape(DIFF_V_DIM, 1), g_t)


def _out_kernel(o_ref, h_ref, w_ref, y_ref):
    y_ref[0] = h_ref[0] + lax.dot_general(o_ref[0], w_ref[...], _TN, preferred_element_type=F32)


def _out_proj(o_t, h1, w_out):
    b, s, d = h1.shape
    tm = ROW_TILE
    tok = pl.BlockSpec((1, tm, d), lambda bi, i: (bi, i, 0))
    return pl.pallas_call(
        _out_kernel,
        grid=(b, s // tm),
        in_specs=[
            pl.BlockSpec((1, o_t.shape[1], tm), lambda bi, i: (bi, 0, i)),
            tok,
            pl.BlockSpec(w_out.shape, lambda *_: (0, 0)),
        ],
        out_specs=tok,
        out_shape=jax.ShapeDtypeStruct((b, s, d), F32),
        compiler_params=_params(2),
        name="out_proj",
    )(o_t, h1, w_out.astype(BF16))


def kernel(x, a_norm, a_w_in, a_b_f, a_q_norm, a_k_norm, a_w_out, kv_norm, kv_w, kv_k_norm,
           rel_bias, b_norm, b_w_in, b_q_norm, b_lam_q1, b_lam_k1, b_lam_q2, b_lam_k2,
           b_sub_norm, b_w_out):
    assert a_norm.shape[0] == 1 and b_norm.shape[0] == 1
    assert x.shape[1] % ATTN_TILE == 0 and ATTN_TILE == ROW_TILE
    q_t, k_c, v_c, g_t, c_c = _fox_proj(x, a_norm[0], a_w_in[0], a_b_f[0], a_q_norm[0], a_k_norm[0])
    o_t = _fox_attn(q_t, k_c, v_c, c_c, g_t)
    h1, k2_c, v2_c, q2_t, g2_t = _mid(o_t, x, a_w_out[0], kv_norm, kv_w, kv_k_norm,
                                      b_norm[0], b_w_in[0], b_q_norm[0])
    lam_params = jnp.stack([b_lam_q1[0], b_lam_k1[0], b_lam_q2[0], b_lam_k2[0]])
    o2_t = _diff_attn(q2_t, k2_c, v2_c, g2_t, rel_bias, lam_params, b_sub_norm[0])
    return _out_proj(o2_t, h1, b_w_out[0])
```

```python
import functools
import math

import jax
import jax.numpy as jnp
from jax import lax
from jax.experimental import pallas as pl
from jax.experimental.pallas import tpu as pltpu

HEAD_DIM = 64
DIFF_V_DIM = 2 * HEAD_DIM
CHUNK = 64
NUM_BUCKETS = 32
MAX_DISTANCE = 128
RMS_EPS = 1e-6
NEG_INF = -1e30
LOG2E = math.log2(math.e)
QK_SCALE = HEAD_DIM ** -0.5 * LOG2E
LAMBDA_INIT = 0.8 - 0.6 * math.exp(-0.3 * 1)

ROW_TILE = 512
ATTN_TILE = 512
FOX_HEADS_PER_STEP = 2
DIFF_HEADS_PER_STEP = 1
ONES_ROWS = 16
SCORE_SPLIT = 1
FAR_STEPS_PER_BLOCK = 2
V7X_VMEM_LIMIT_BYTES = 56 * 1024 * 1024

F32 = jnp.float32
BF16 = jnp.bfloat16
_NT = (((1,), (1,)), ((), ()))
_TN = (((0,), (0,)), ((), ()))


def _params(n_axes):
    return pltpu.CompilerParams(
        dimension_semantics=("arbitrary",) * n_axes,
        vmem_limit_bytes=V7X_VMEM_LIMIT_BYTES)


def _head_norm(p_t, g_col, n_heads, scale):
    t = p_t.shape[-1]
    p3 = p_t.reshape(n_heads, HEAD_DIM, t)
    ms = jnp.mean(p3 * p3, axis=1, keepdims=True)
    y = p3 * lax.rsqrt(ms + RMS_EPS) * (g_col * scale)[None]
    return y.reshape(n_heads * HEAD_DIM, t)


def _split3(c):
    hi = c.astype(BF16).astype(F32)
    r = c - hi
    mid = r.astype(BF16).astype(F32)
    lo = (r - mid).astype(BF16).astype(F32)
    return hi, mid, lo


def _silu(g):
    return g / (1.0 + jnp.exp(-g))


def _fox_proj_kernel(x_ref, gn_ref, wq_ref, wk_ref, wv_ref, wg_ref, wf_ref, bf_ref,
                     gq_ref, gk_ref, q_ref, k_ref, v_ref, g_ref, c_ref, carry_ref,
                     *, n_heads):
    @pl.when(pl.program_id(1) == 0)
    def _():
        carry_ref[...] = jnp.zeros_like(carry_ref)

    x = x_ref[0]
    tm = x.shape[0]
    ms = jnp.mean(x * x, axis=-1, keepdims=True)
    u = (x * lax.rsqrt(ms + RMS_EPS) * gn_ref[...]).astype(BF16)

    def proj(w_ref):
        return lax.dot_general(w_ref[...], u, _NT, preferred_element_type=F32)

    q_ref[0, 0] = _head_norm(proj(wq_ref), gq_ref[...], n_heads, QK_SCALE).astype(BF16)
    k_ref[0, 0] = _head_norm(proj(wk_ref), gk_ref[...], n_heads, 1.0).astype(BF16)
    v_ref[0, 0] = proj(wv_ref).astype(BF16)
    g_ref[0, 0] = proj(wg_ref).astype(BF16)

    z = proj(wf_ref) + bf_ref[...]
    log_f = jnp.minimum(z, 0.0) - jnp.log1p(jnp.exp(-jnp.abs(z)))
    lane = lax.broadcasted_iota(jnp.int32, log_f.shape, 1)
    cs = log_f
    shift = 1
    while shift < tm:
        cs = cs + jnp.where(lane >= shift, pltpu.roll(cs, shift, axis=1), 0.0)
        shift *= 2
    c = cs + carry_ref[:, 0:1]
    c_ref[0, 0] = c * LOG2E
    carry_ref[...] = jnp.broadcast_to(c[:, tm - 1:tm], carry_ref.shape)


def _fox_proj(x, a_norm, w_in, b_f, g_q, g_k):
    b, s, d = x.shape
    n_heads = b_f.shape[0]
    width = n_heads * HEAD_DIM
    tm = ROW_TILE
    w_t = w_in.T.astype(BF16)
    wq, wk, wv, wg, wf = (w_t[0:width], w_t[width:2 * width], w_t[2 * width:3 * width],
                          w_t[3 * width:4 * width], w_t[4 * width:])
    const = lambda *_: (0, 0)
    w_spec = pl.BlockSpec((width, d), const)
    col = pl.BlockSpec((HEAD_DIM, 1), const)
    chunked = pl.BlockSpec((1, 1, width, tm), lambda bi, i: (bi, i, 0, 0))
    return pl.pallas_call(
        functools.partial(_fox_proj_kernel, n_heads=n_heads),
        grid=(b, s // tm),
        in_specs=[
            pl.BlockSpec((1, tm, d), lambda bi, i: (bi, i, 0)),
            pl.BlockSpec((1, d), const),
            w_spec, w_spec, w_spec, w_spec,
            pl.BlockSpec((n_heads, d), const),
            pl.BlockSpec((n_heads, 1), const),
            col, col,
        ],
        out_specs=[chunked, chunked, chunked, chunked,
                   pl.BlockSpec((1, 1, n_heads, tm), lambda bi, i: (bi, i, 0, 0))],
        out_shape=[
            jax.ShapeDtypeStruct((b, s // tm, width, tm), BF16),
            jax.ShapeDtypeStruct((b, s // tm, width, tm), BF16),
            jax.ShapeDtypeStruct((b, s // tm, width, tm), BF16),
            jax.ShapeDtypeStruct((b, s // tm, width, tm), BF16),
            jax.ShapeDtypeStruct((b, s // tm, n_heads, tm), F32),
        ],
        scratch_shapes=[pltpu.VMEM((n_heads, 128), F32)],
        compiler_params=_params(2),
        name="fox_proj",
    )(x, a_norm.reshape(1, d), wq, wk, wv, wg, wf, b_f.reshape(n_heads, 1),
      g_q.reshape(HEAD_DIM, 1), g_k.reshape(HEAD_DIM, 1))


def _store_scores(score_rows, s_ref, mx_ref, idx):
    rows = s_ref.shape[1] // SCORE_SPLIT
    mx = None
    for r in range(SCORE_SPLIT):
        s = score_rows(r * rows, rows)
        s_ref[idx, r * rows:(r + 1) * rows, :] = s
        part = jnp.max(s, axis=0, keepdims=True)
        mx = part if mx is None else jnp.maximum(mx, part)
    mx_ref[idx] = mx


def _softmax_step(s_ref, mx_ref, m_ref, alpha_ref, p_ref, idx):
    m_prev = m_ref[idx]
    m_new = jnp.maximum(m_prev, mx_ref[idx])
    p_ref[idx] = jnp.exp2(s_ref[idx] - m_new).astype(BF16)
    alpha_ref[idx] = jnp.exp2(m_prev - m_new)
    m_ref[idx] = m_new


def _with_ones_rows(v):
    return jnp.concatenate([v, jnp.ones((ONES_ROWS, v.shape[1]), v.dtype)], axis=0)


def _run_row(n_q, tail, make_produce, consume, pv, close_tile, open_tile):
    n_tail = len(tail)
    assert n_q > n_tail + 1

    def kinds_of(q_tile):
        return (("far",) * (q_tile + 1) + tail)[-(q_tile + 1):] if q_tile + 1 > n_tail else tail[n_tail - 1 - q_tile:]

    def first_kind(q_tile):
        return "far" if q_tile >= n_tail else tail[n_tail - 1 - q_tile]

    def query_tile(qi, kinds):
        produce = make_produce(qi)
        successor = min(qi + 1, n_q - 1) if isinstance(qi, int) else jnp.minimum(qi + 1, n_q - 1)
        produce_next = functools.partial(make_produce(successor), 0)

        def step(j, produce_following):
            pv(j - 1)
            consume()
            produce_following()

        def first_step(produce_following):
            if not (isinstance(qi, int) and qi == 0):
                pv(qi - 1)
                close_tile(qi - 1)
            open_tile()
            consume()
            produce_following()

        def rest(first_tile, produced_kinds, following_first_kind):
            j = first_tile
            for kind in produced_kinds:
                step(j, functools.partial(produce, j + 1, kind))
                j = j + 1
            step(j, functools.partial(produce_next, following_first_kind))

        if kinds is not None:
            if qi == 0:
                produce(0, kinds[0])
                first_step(functools.partial(produce_next, first_kind(1)))
            else:
                first_step(functools.partial(produce, 1, kinds[1]))
                rest(1, kinds[2:], first_kind(qi + 1))
            return

        first_step(functools.partial(produce, 1, "far"))

        n_far = qi - n_tail - 1

        def far_steps(first, count):
            for j in range(count):
                step(first + j, functools.partial(produce, first + j + 1, "far"))

        def far_block(i, carry):
            far_steps(1 + FAR_STEPS_PER_BLOCK * i, FAR_STEPS_PER_BLOCK)
            return carry
        lax.fori_loop(0, n_far // FAR_STEPS_PER_BLOCK, far_block, 0)

        done = n_far - n_far % FAR_STEPS_PER_BLOCK
        size = FAR_STEPS_PER_BLOCK // 2
        while size:
            @pl.when((n_far & size) != 0)
            def _(size=size, done=done):
                far_steps(1 + done, size)
            done = done + (n_far & size)
            size //= 2
        rest(qi - n_tail, tail, "far")

    for qi in range(n_tail + 1):
        query_tile(qi, kinds_of(qi))

    def general_tile(qi, carry):
        query_tile(qi, None)
        return carry
    lax.fori_loop(n_tail + 1, n_q, general_tile, 0)

    pv(n_q - 1)
    close_tile(n_q - 1)


def _pv_step(v, p_ref, alpha_ref, acc_ref, idx):
    acc_ref[idx] = alpha_ref[idx] * acc_ref[idx] + jnp.dot(
        v, p_ref[idx], preferred_element_type=F32)


def _reset_softmax_state(m_ref, acc_ref):
    m_ref[...] = jnp.full(m_ref.shape, NEG_INF, F32)
    acc_ref[...] = jnp.zeros_like(acc_ref)


def _aug_rows(rows8, shape):
    row = lax.broadcasted_iota(jnp.int32, shape, 0)
    out = jnp.zeros(shape, F32)
    for r, val in enumerate(rows8):
        out = jnp.where(row == r, val, out)
    return out


def _fox_attn_kernel(q_ref, k_ref, v_ref, c_ref, g_ref, o_ref,
                     kaug_ref, vaug_ref, m_ref, alpha_ref, mx_ref, acc_ref, s_ref, p_ref,
                     *, hp, n_chunks):
    t = q_ref.shape[-1]
    pad = jnp.zeros((2 * HEAD_DIM - HEAD_DIM - 8, t), F32)

    def prep(j, carry):
        for h in range(hp):
            kt = k_ref[0, j, h * HEAD_DIM:(h + 1) * HEAD_DIM, :].astype(F32)
            hi, mid, lo = _split3(c_ref[0, j, 0, h:h + 1, :])
            aug = _aug_rows([-hi, -mid, -lo, 1.0, 1.0, 1.0], (8, t))
            kaug_ref[h, j] = jnp.concatenate([kt, aug, pad], axis=0).T.astype(BF16)
            vaug_ref[h, j] = _with_ones_rows(v_ref[0, j, h * HEAD_DIM:(h + 1) * HEAD_DIM, :])
        return carry
    lax.fori_loop(0, n_chunks, prep, 0)

    def make_produce(qi):
        qa = []
        for h in range(hp):
            qf = q_ref[0, qi, h * HEAD_DIM:(h + 1) * HEAD_DIM, :].astype(F32)
            hi, mid, lo = _split3(c_ref[0, qi, 0, h:h + 1, :])
            aug = _aug_rows([1.0, 1.0, 1.0, hi, mid, lo], (8, t))
            qa.append(jnp.concatenate([qf, aug, pad], axis=0).astype(BF16))

        def produce(j, kind):
            for h in range(hp):
                def score_rows(r0, n, h=h):
                    s = jnp.dot(kaug_ref[h, j, r0:r0 + n, :], qa[h], preferred_element_type=F32)
                    if kind == "diag":
                        kpos = r0 + lax.broadcasted_iota(jnp.int32, (n, t), 0)
                        qpos = lax.broadcasted_iota(jnp.int32, (n, t), 1)
                        s = jnp.where(kpos <= qpos, s, NEG_INF)
                    return s
                _store_scores(score_rows, s_ref, mx_ref, h)
        return produce

    def consume():
        for h in range(hp):
            _softmax_step(s_ref, mx_ref, m_ref, alpha_ref, p_ref, h)

    def pv(j):
        for h in range(hp):
            _pv_step(vaug_ref[h, j], p_ref, alpha_ref, acc_ref, h)

    def close_tile(qi):
        for h in range(hp):
            o = acc_ref[h, 0:HEAD_DIM] * (1.0 / acc_ref[h, HEAD_DIM:HEAD_DIM + 1])
            g = g_ref[0, qi, h * HEAD_DIM:(h + 1) * HEAD_DIM, :].astype(F32)
            o_ref[0, qi, h * HEAD_DIM:(h + 1) * HEAD_DIM, :] = (o * _silu(g)).astype(BF16)

    _run_row(n_chunks, ("diag",), make_produce, consume, pv, close_tile,
             functools.partial(_reset_softmax_state, m_ref, acc_ref))


def _fox_attn(q_c, k_c, v_c, c_c, g_c):
    b, n_chunks, width, t = q_c.shape
    hp = FOX_HEADS_PER_STEP
    n_heads = width // HEAD_DIM
    rows = hp * HEAD_DIM
    c_c = c_c.reshape(b, n_chunks, n_heads // hp, hp, t)
    full = pl.BlockSpec((1, n_chunks, rows, t), lambda bi, hg: (bi, 0, hg, 0))
    return pl.pallas_call(
        functools.partial(_fox_attn_kernel, hp=hp, n_chunks=n_chunks),
        grid=(b, n_heads // hp),
        in_specs=[
            full, full, full,
            pl.BlockSpec((1, n_chunks, 1, hp, t), lambda bi, hg: (bi, 0, hg, 0, 0)),
            full,
        ],
        out_specs=full,
        out_shape=jax.ShapeDtypeStruct((b, n_chunks, width, t), BF16),
        scratch_shapes=[
            pltpu.VMEM((hp, n_chunks, t, 2 * HEAD_DIM), BF16),
            pltpu.VMEM((hp, n_chunks, HEAD_DIM + ONES_ROWS, t), BF16),
            pltpu.VMEM((hp, 1, t), F32),
            pltpu.VMEM((hp, 1, t), F32),
            pltpu.VMEM((hp, 1, t), F32),
            pltpu.VMEM((hp, HEAD_DIM + ONES_ROWS, t), F32),
            pltpu.VMEM((hp, t, t), F32),
            pltpu.VMEM((hp, t, t), BF16),
        ],
        compiler_params=_params(2),
        name="fox_attn",
    )(q_c, k_c, v_c, c_c, g_c)


def _mid_kernel(o_ref, x_ref, wo_ref, gkv_ref, gb_ref, wkv_ref, wb_ref, gk_ref, gq_ref,
                h_ref, k_ref, v_ref, q_ref, g_ref, *, n_heads):
    half = n_heads * HEAD_DIM
    h1 = x_ref[0] + lax.dot_general(o_ref[0, 0], wo_ref[...], _TN, preferred_element_type=F32)
    h_ref[0] = h1
    ms = jnp.mean(h1 * h1, axis=-1, keepdims=True)
    hn = h1 * lax.rsqrt(ms + RMS_EPS)
    u_kv = (hn * gkv_ref[...]).astype(BF16)
    u_b = (hn * gb_ref[...]).astype(BF16)

    def pack_heads(ref, a, b2):
        for h in range(n_heads):
            sl = slice(h * HEAD_DIM, (h + 1) * HEAD_DIM)
            ref[..., h * 2 * HEAD_DIM:h * 2 * HEAD_DIM + HEAD_DIM, :] = a[sl]
            ref[..., h * 2 * HEAD_DIM + HEAD_DIM:(h + 1) * 2 * HEAD_DIM, :] = b2[sl]

    kv = lax.dot_general(wkv_ref[...], u_kv, _NT, preferred_element_type=F32)
    k1 = _head_norm(kv[0:half], gk_ref[0], n_heads, 1.0).astype(BF16)
    k2 = _head_norm(kv[half:2 * half], gk_ref[1], n_heads, 1.0).astype(BF16)
    pack_heads(k_ref.at[0, 0], k1, k2)
    v_ref[0, 0] = kv[2 * half:].astype(BF16)

    qg = lax.dot_general(wb_ref[...], u_b, _NT, preferred_element_type=F32)
    q1 = _head_norm(qg[0:half], gq_ref[0], n_heads, QK_SCALE).astype(BF16)
    q2 = _head_norm(qg[half:2 * half], gq_ref[1], n_heads, QK_SCALE).astype(BF16)
    pack_heads(q_ref.at[0, 0], q1, q2)
    g_ref[0, 0] = qg[2 * half:].astype(BF16)


def _mid(o_c, x, w_out, kv_norm, kv_w, kv_k_norm, b_norm, b_w_in, b_q_norm):
    b, s, d = x.shape
    tm = ROW_TILE
    n_heads = kv_w.shape[1] // (4 * HEAD_DIM)
    width = 2 * n_heads * HEAD_DIM
    const2 = lambda *_: (0, 0)
    const3 = lambda *_: (0, 0, 0)
    tok = pl.BlockSpec((1, tm, d), lambda bi, i: (bi, i, 0))
    chunked = pl.BlockSpec((1, 1, width, tm), lambda bi, i: (bi, i, 0, 0))
    return pl.pallas_call(
        functools.partial(_mid_kernel, n_heads=n_heads),
        grid=(b, s // tm),
        in_specs=[
            pl.BlockSpec((1, 1, o_c.shape[2], tm), lambda bi, i: (bi, i, 0, 0)),
            tok,
            pl.BlockSpec(w_out.shape, const2),
            pl.BlockSpec((1, d), const2),
            pl.BlockSpec((1, d), const2),
            pl.BlockSpec((2 * width, d), const2),
            pl.BlockSpec((2 * width, d), const2),
            pl.BlockSpec((2, HEAD_DIM, 1), const3),
            pl.BlockSpec((2, HEAD_DIM, 1), const3),
        ],
        out_specs=[tok, chunked, chunked, chunked, chunked],
        out_shape=[
            jax.ShapeDtypeStruct((b, s, d), F32),
            jax.ShapeDtypeStruct((b, s // tm, width, tm), BF16),
            jax.ShapeDtypeStruct((b, s // tm, width, tm), BF16),
            jax.ShapeDtypeStruct((b, s // tm, width, tm), BF16),
            jax.ShapeDtypeStruct((b, s // tm, width, tm), BF16),
        ],
        compiler_params=_params(2),
        name="mid_proj",
    )(o_c, x, w_out.astype(BF16), kv_norm.reshape(1, d), b_norm.reshape(1, d),
      kv_w.T.astype(BF16), b_w_in.T.astype(BF16),
      kv_k_norm.reshape(2, HEAD_DIM, 1), b_q_norm.reshape(2, HEAD_DIM, 1))


def _diff_attn_kernel(q_ref, k_ref, v_ref, brow_ref, lam_ref, gs_ref, g_ref, o_ref,
                      ktok_ref, vaug_ref, bias_ref, m_ref, alpha_ref, mx_ref, acc_ref, s_ref, p_ref,
                      *, hd, n_chunks):
    t = q_ref.shape[-1]
    dv = DIFF_V_DIM
    streams = [(h, mp) for h in range(hd) for mp in range(2)]

    def prep(j, carry):
        for h in range(hd):
            ktok_ref[h, j] = k_ref[0, j, h * dv:(h + 1) * dv, :].astype(F32).T.astype(BF16)
            vaug_ref[h, j] = _with_ones_rows(v_ref[0, j, h * dv:(h + 1) * dv, :])
        return carry
    lax.fori_loop(0, n_chunks, prep, 0)
    for h, mp in streams:
        for off in range(2):
            rows = jnp.broadcast_to(brow_ref[h, mp, off], (t, 2 * t))
            bias_ref[h, mp, off] = pltpu.roll(rows, 0, 1, stride=1, stride_axis=0)[:, 0:t]

    def make_produce(qi):
        qa = []
        for h in range(hd):
            q = q_ref[0, qi, h * dv:(h + 1) * dv, :]
            row = lax.broadcasted_iota(jnp.int32, q.shape, 0)
            zero = jnp.zeros_like(q)
            qa += [jnp.where(row < HEAD_DIM, q, zero), jnp.where(row >= HEAD_DIM, q, zero)]

        def produce(j, kind):
            for i, (h, mp) in enumerate(streams):
                def score_rows(r0, n, i=i, h=h, mp=mp):
                    s = jnp.dot(ktok_ref[h, j, r0:r0 + n, :], qa[i], preferred_element_type=F32)
                    if kind == "near":
                        s = s + bias_ref[h, mp, 1, r0:r0 + n, :]
                    elif kind == "diag":
                        kchunk = (r0 + lax.broadcasted_iota(jnp.int32, (n, t), 0)) // CHUNK
                        qchunk = lax.broadcasted_iota(jnp.int32, (n, t), 1) // CHUNK
                        s = jnp.where(kchunk <= qchunk, s + bias_ref[h, mp, 0, r0:r0 + n, :], NEG_INF)
                    return s
                _store_scores(score_rows, s_ref, mx_ref, i)
        return produce

    def consume():
        for i in range(len(streams)):
            _softmax_step(s_ref, mx_ref, m_ref, alpha_ref, p_ref, i)

    def pv(j):
        for i, (h, mp) in enumerate(streams):
            _pv_step(vaug_ref[h, j], p_ref, alpha_ref, acc_ref, i)

    def close_tile(qi):
        lam_p = lam_ref[...]
        e1 = jnp.exp(jnp.sum(lam_p[0:1] * lam_p[1:2], axis=1, keepdims=True))
        e2 = jnp.exp(jnp.sum(lam_p[2:3] * lam_p[3:4], axis=1, keepdims=True))
        lam = e1 - e2 + LAMBDA_INIT
        for h in range(hd):
            a1, a2 = acc_ref[2 * h], acc_ref[2 * h + 1]
            o = a1[0:dv] * (1.0 / a1[dv:dv + 1]) - lam * (a2[0:dv] * (1.0 / a2[dv:dv + 1]))
            ms = jnp.mean(o * o, axis=0, keepdims=True)
            y = o * lax.rsqrt(ms + RMS_EPS) * gs_ref[...] * (1.0 - LAMBDA_INIT)
            g = g_ref[0, qi, h * dv:(h + 1) * dv, :].astype(F32)
            o_ref[0, qi, h * dv:(h + 1) * dv, :] = (y * _silu(g)).astype(BF16)

    _run_row(n_chunks, ("near", "diag"), make_produce, consume, pv, close_tile,
             functools.partial(_reset_softmax_state, m_ref, acc_ref))


def _t5_bucket(rel):
    half = NUM_BUCKETS // 2
    max_exact = half // 2
    ret = jnp.where(rel > 0, half, 0)
    n = jnp.abs(rel)
    n_f = jnp.maximum(n, 1).astype(jnp.float32)
    large = max_exact + (jnp.log(n_f / max_exact) / math.log(MAX_DISTANCE / max_exact)
                         * (half - max_exact)).astype(jnp.int32)
    large = jnp.minimum(large, half - 1)
    return ret + jnp.where(n < max_exact, n, large)


def _bias_rows(rel_bias, t):
    dw = jnp.arange(2 * t)
    d = jnp.where(dw < t, dw, dw - 2 * t)
    rel = jnp.stack([-d, -d - t])
    rows = rel_bias[_t5_bucket(rel)].astype(F32)
    far = rel_bias[_t5_bucket(jnp.int32(-MAX_DISTANCE))].astype(F32)
    return ((rows - far) * LOG2E).transpose(3, 2, 0, 1)[:, :, :, None, :]


def _diff_attn(q_c, k_c, v_c, g_c, rel_bias, lam_params, g_sub):
    b, n_chunks, width, t = q_c.shape
    n_heads = width // DIFF_V_DIM
    hd = DIFF_HEADS_PER_STEP
    rows = hd * DIFF_V_DIM
    n_streams = 2 * hd
    full = pl.BlockSpec((1, n_chunks, rows, t), lambda bi, hg: (bi, 0, hg, 0))
    return pl.pallas_call(
        functools.partial(_diff_attn_kernel, hd=hd, n_chunks=n_chunks),
        grid=(b, n_heads // hd),
        in_specs=[
            full, full, full,
            pl.BlockSpec((hd, 2, 2, 1, 2 * t), lambda bi, hg: (hg, 0, 0, 0, 0)),
            pl.BlockSpec((4, HEAD_DIM), lambda *_: (0, 0)),
            pl.BlockSpec((DIFF_V_DIM, 1), lambda *_: (0, 0)),
            full,
        ],
        out_specs=full,
        out_shape=jax.ShapeDtypeStruct((b, n_chunks, width, t), BF16),
        scratch_shapes=[
            pltpu.VMEM((hd, n_chunks, t, DIFF_V_DIM), BF16),
            pltpu.VMEM((hd, n_chunks, DIFF_V_DIM + ONES_ROWS, t), BF16),
            pltpu.VMEM((hd, 2, 2, t, t), F32),
            pltpu.VMEM((n_streams, 1, t), F32),
            pltpu.VMEM((n_streams, 1, t), F32),
            pltpu.VMEM((n_streams, 1, t), F32),
            pltpu.VMEM((n_streams, DIFF_V_DIM + ONES_ROWS, t), F32),
            pltpu.VMEM((n_streams, t, t), F32),
            pltpu.VMEM((n_streams, t, t), BF16),
        ],
        compiler_params=_params(2),
        name="diff_attn",
    )(q_c, k_c, v_c, _bias_rows(rel_bias, t), lam_params, g_sub.reshape(DIFF_V_DIM, 1), g_c)


def _out_kernel(o_ref, h_ref, w_ref, y_ref):
    y_ref[0] = h_ref[0] + lax.dot_general(o_ref[0, 0], w_ref[...], _TN, preferred_element_type=F32)


def _out_proj(o_c, h1, w_out):
    b, s, d = h1.shape
    tm = ROW_TILE
    tok = pl.BlockSpec((1, tm, d), lambda bi, i: (bi, i, 0))
    return pl.pallas_call(
        _out_kernel,
        grid=(b, s // tm),
        in_specs=[
            pl.BlockSpec((1, 1, o_c.shape[2], tm), lambda bi, i: (bi, i, 0, 0)),
            tok,
            pl.BlockSpec(w_out.shape, lambda *_: (0, 0)),
        ],
        out_specs=tok,
        out_shape=jax.ShapeDtypeStruct((b, s, d), F32),
        compiler_params=_params(2),
        name="out_proj",
    )(o_c, h1, w_out.astype(BF16))


def kernel(x, a_norm, a_w_in, a_b_f, a_q_norm, a_k_norm, a_w_out, kv_norm, kv_w, kv_k_norm,
           rel_bias, b_norm, b_w_in, b_q_norm, b_lam_q1, b_lam_k1, b_lam_q2, b_lam_k2,
           b_sub_norm, b_w_out):
    assert a_norm.shape[0] == 1 and b_norm.shape[0] == 1
    assert x.shape[1] % ATTN_TILE == 0 and ATTN_TILE == ROW_TILE
    q_c, k_c, v_c, g_c, c_c = _fox_proj(x, a_norm[0], a_w_in[0], a_b_f[0], a_q_norm[0], a_k_norm[0])
    o_c = _fox_attn(q_c, k_c, v_c, c_c, g_c)
    h1, k2_c, v2_c, q2_c, g2_c = _mid(o_c, x, a_w_out[0], kv_norm, kv_w, kv_k_norm,
                                      b_norm[0], b_w_in[0], b_q_norm[0])
    lam_params = jnp.stack([b_lam_q1[0], b_lam_k1[0], b_lam_q2[0], b_lam_k2[0]])
    o2_c = _diff_attn(q2_c, k2_c, v2_c, g2_c, rel_bias, lam_params, b_sub_norm[0])
    return _out_proj(o2_c, h1, b_w_out[0])
```

```python
import functools
import math

import jax
import jax.numpy as jnp
from jax import lax
from jax.experimental import pallas as pl
from jax.experimental.pallas import tpu as pltpu

HEAD_DIM = 64
DIFF_V_DIM = 2 * HEAD_DIM
CHUNK = 64
NUM_BUCKETS = 32
MAX_DISTANCE = 128
RMS_EPS = 1e-6
NEG_INF = -1e30
LOG2E = math.log2(math.e)
QK_SCALE = HEAD_DIM ** -0.5 * LOG2E
LAMBDA_INIT = 0.8 - 0.6 * math.exp(-0.3 * 1)

ROW_TILE = 512
ATTN_TILE = 512
FOX_HEADS_PER_STEP = 2
DIFF_HEADS_PER_STEP = 1
ONES_ROWS = 16
SCORE_SPLIT = 1
FAR_STEPS_PER_BLOCK = 2
V7X_VMEM_LIMIT_BYTES = 56 * 1024 * 1024

F32 = jnp.float32
BF16 = jnp.bfloat16
_NT = (((1,), (1,)), ((), ()))
_TN = (((0,), (0,)), ((), ()))


def _params(n_axes):
    return pltpu.CompilerParams(
        dimension_semantics=("arbitrary",) * n_axes,
        vmem_limit_bytes=V7X_VMEM_LIMIT_BYTES)


def _head_norm(p_t, g_col, n_heads, scale):
    t = p_t.shape[-1]
    p3 = p_t.reshape(n_heads, HEAD_DIM, t)
    ms = jnp.mean(p3 * p3, axis=1, keepdims=True)
    y = p3 * lax.rsqrt(ms + RMS_EPS) * (g_col * scale)[None]
    return y.reshape(n_heads * HEAD_DIM, t)


def _split3(c):
    hi = c.astype(BF16).astype(F32)
    r = c - hi
    mid = r.astype(BF16).astype(F32)
    lo = (r - mid).astype(BF16).astype(F32)
    return hi, mid, lo


def _silu(g):
    return g / (1.0 + jnp.exp(-g))


def _fox_proj_kernel(x_ref, gn_ref, wq_ref, wk_ref, wv_ref, wg_ref, wf_ref, bf_ref,
                     gq_ref, gk_ref, q_ref, k_ref, v_ref, g_ref, c_ref, carry_ref,
                     *, n_heads):
    @pl.when(pl.program_id(1) == 0)
    def _():
        carry_ref[...] = jnp.zeros_like(carry_ref)

    x = x_ref[0]
    tm = x.shape[0]
    ms = jnp.mean(x * x, axis=-1, keepdims=True)
    u = (x * lax.rsqrt(ms + RMS_EPS) * gn_ref[...]).astype(BF16)

    def proj(w_ref):
        return lax.dot_general(w_ref[...], u, _NT, preferred_element_type=F32)

    z = proj(wf_ref) + bf_ref[...]
    log_f = jnp.minimum(z, 0.0) - jnp.log1p(jnp.exp(-jnp.abs(z)))
    lane = lax.broadcasted_iota(jnp.int32, log_f.shape, 1)
    cs = log_f
    shift = 1
    while shift < tm:
        cs = cs + jnp.where(lane >= shift, pltpu.roll(cs, shift, axis=1), 0.0)
        shift *= 2
    c = cs + carry_ref[:, 0:1]
    c_ref[0, 0] = c * LOG2E
    carry_ref[...] = jnp.broadcast_to(c[:, tm - 1:tm], carry_ref.shape)

    q_ref[0, 0] = _head_norm(proj(wq_ref), gq_ref[...], n_heads, QK_SCALE).astype(BF16)
    k_ref[0, 0] = _head_norm(proj(wk_ref), gk_ref[...], n_heads, 1.0).astype(BF16)
    v_ref[0, 0] = proj(wv_ref).astype(BF16)
    g_ref[0, 0] = proj(wg_ref).astype(BF16)


def _fox_proj(x, a_norm, w_in, b_f, g_q, g_k):
    b, s, d = x.shape
    n_heads = b_f.shape[0]
    width = n_heads * HEAD_DIM
    tm = ROW_TILE
    w_t = w_in.T.astype(BF16)
    wq, wk, wv, wg, wf = (w_t[0:width], w_t[width:2 * width], w_t[2 * width:3 * width],
                          w_t[3 * width:4 * width], w_t[4 * width:])
    const = lambda *_: (0, 0)
    w_spec = pl.BlockSpec((width, d), const)
    col = pl.BlockSpec((HEAD_DIM, 1), const)
    chunked = pl.BlockSpec((1, 1, width, tm), lambda bi, i: (bi, i, 0, 0))
    return pl.pallas_call(
        functools.partial(_fox_proj_kernel, n_heads=n_heads),
        grid=(b, s // tm),
        in_specs=[
            pl.BlockSpec((1, tm, d), lambda bi, i: (bi, i, 0)),
            pl.BlockSpec((1, d), const),
            w_spec, w_spec, w_spec, w_spec,
            pl.BlockSpec((n_heads, d), const),
            pl.BlockSpec((n_heads, 1), const),
            col, col,
        ],
        out_specs=[chunked, chunked, chunked, chunked,
                   pl.BlockSpec((1, 1, n_heads, tm), lambda bi, i: (bi, i, 0, 0))],
        out_shape=[
            jax.ShapeDtypeStruct((b, s // tm, width, tm), BF16),
            jax.ShapeDtypeStruct((b, s // tm, width, tm), BF16),
            jax.ShapeDtypeStruct((b, s // tm, width, tm), BF16),
            jax.ShapeDtypeStruct((b, s // tm, width, tm), BF16),
            jax.ShapeDtypeStruct((b, s // tm, n_heads, tm), F32),
        ],
        scratch_shapes=[pltpu.VMEM((n_heads, 128), F32)],
        compiler_params=_params(2),
        name="fox_proj",
    )(x, a_norm.reshape(1, d), wq, wk, wv, wg, wf, b_f.reshape(n_heads, 1),
      g_q.reshape(HEAD_DIM, 1), g_k.reshape(HEAD_DIM, 1))


def _store_scores(score_rows, s_ref, mx_ref, idx):
    rows = s_ref.shape[1] // SCORE_SPLIT
    mx = None
    for r in range(SCORE_SPLIT):
        s = score_rows(r * rows, rows)
        s_ref[idx, r * rows:(r + 1) * rows, :] = s
        part = jnp.max(s, axis=0, keepdims=True)
        mx = part if mx is None else jnp.maximum(mx, part)
    mx_ref[idx] = mx


def _softmax_step(s_ref, mx_ref, m_ref, alpha_ref, p_ref, idx):
    m_prev = m_ref[idx]
    m_new = jnp.maximum(m_prev, mx_ref[idx])
    p_ref[idx] = jnp.exp2(s_ref[idx] - m_new).astype(BF16)
    alpha_ref[idx] = jnp.exp2(m_prev - m_new)
    m_ref[idx] = m_new


def _with_ones_rows(v):
    return jnp.concatenate([v, jnp.ones((ONES_ROWS, v.shape[1]), v.dtype)], axis=0)


def _run_row(n_q, tail, prepare, make_produce, consume, pv, close_tile, open_tile):
    n_tail = len(tail)
    assert n_q > n_tail + 1
    prepare(0)

    def kinds_of(q_tile):
        return (("far",) * (q_tile + 1) + tail)[-(q_tile + 1):] if q_tile + 1 > n_tail else tail[n_tail - 1 - q_tile:]

    def first_kind(q_tile):
        return "far" if q_tile >= n_tail else tail[n_tail - 1 - q_tile]

    def query_tile(qi, kinds):
        produce = make_produce(qi)
        successor = min(qi + 1, n_q - 1) if isinstance(qi, int) else jnp.minimum(qi + 1, n_q - 1)
        produce_next = functools.partial(make_produce(successor), 0)

        def step(j, produce_following):
            pv(j - 1)
            consume()
            produce_following()

        def first_step(produce_following):
            if not (isinstance(qi, int) and qi == 0):
                pv(qi - 1)
                close_tile(qi - 1)
            open_tile()
            consume()
            produce_following()
            prepare(successor)

        def rest(first_tile, produced_kinds, following_first_kind):
            j = first_tile
            for kind in produced_kinds:
                step(j, functools.partial(produce, j + 1, kind))
                j = j + 1
            step(j, functools.partial(produce_next, following_first_kind))

        if kinds is not None:
            if qi == 0:
                produce(0, kinds[0])
                first_step(functools.partial(produce_next, first_kind(1)))
            else:
                first_step(functools.partial(produce, 1, kinds[1]))
                rest(1, kinds[2:], first_kind(qi + 1))
            return

        first_step(functools.partial(produce, 1, "far"))

        n_far = qi - n_tail - 1

        def far_steps(first, count):
            for j in range(count):
                step(first + j, functools.partial(produce, first + j + 1, "far"))

        def far_block(i, carry):
            far_steps(1 + FAR_STEPS_PER_BLOCK * i, FAR_STEPS_PER_BLOCK)
            return carry
        lax.fori_loop(0, n_far // FAR_STEPS_PER_BLOCK, far_block, 0)

        done = n_far - n_far % FAR_STEPS_PER_BLOCK
        size = FAR_STEPS_PER_BLOCK // 2
        while size:
            @pl.when((n_far & size) != 0)
            def _(size=size, done=done):
                far_steps(1 + done, size)
            done = done + (n_far & size)
            size //= 2
        rest(qi - n_tail, tail, "far")

    for qi in range(n_tail + 1):
        query_tile(qi, kinds_of(qi))

    def general_tile(qi, carry):
        query_tile(qi, None)
        return carry
    lax.fori_loop(n_tail + 1, n_q, general_tile, 0)

    pv(n_q - 1)
    close_tile(n_q - 1)


def _pv_step(v, p_ref, alpha_ref, acc_ref, idx):
    acc_ref[idx] = alpha_ref[idx] * acc_ref[idx] + jnp.dot(
        v, p_ref[idx], preferred_element_type=F32)


def _reset_softmax_state(m_ref, acc_ref):
    m_ref[...] = jnp.full(m_ref.shape, NEG_INF, F32)
    acc_ref[...] = jnp.zeros_like(acc_ref)


def _aug_rows(rows8, shape):
    row = lax.broadcasted_iota(jnp.int32, shape, 0)
    out = jnp.zeros(shape, F32)
    for r, val in enumerate(rows8):
        out = jnp.where(row == r, val, out)
    return out


def _fox_attn_kernel(q_ref, k_ref, v_ref, c_ref, g_ref, o_ref,
                     kaug_ref, vaug_ref, m_ref, alpha_ref, mx_ref, acc_ref, s_ref, p_ref,
                     *, hp, n_chunks):
    t = q_ref.shape[-1]
    pad = jnp.zeros((2 * HEAD_DIM - HEAD_DIM - 8, t), F32)

    def prepare(j):
        for h in range(hp):
            kt = k_ref[0, j, h * HEAD_DIM:(h + 1) * HEAD_DIM, :].astype(F32)
            hi, mid, lo = _split3(c_ref[0, j, 0, h:h + 1, :])
            aug = _aug_rows([-hi, -mid, -lo, 1.0, 1.0, 1.0], (8, t))
            kaug_ref[h, j] = jnp.concatenate([kt, aug, pad], axis=0).T.astype(BF16)
            vaug_ref[h, j] = _with_ones_rows(v_ref[0, j, h * HEAD_DIM:(h + 1) * HEAD_DIM, :])

    def make_produce(qi):
        qa = []
        for h in range(hp):
            qf = q_ref[0, qi, h * HEAD_DIM:(h + 1) * HEAD_DIM, :].astype(F32)
            hi, mid, lo = _split3(c_ref[0, qi, 0, h:h + 1, :])
            aug = _aug_rows([1.0, 1.0, 1.0, hi, mid, lo], (8, t))
            qa.append(jnp.concatenate([qf, aug, pad], axis=0).astype(BF16))

        def produce(j, kind):
            for h in range(hp):
                def score_rows(r0, n, h=h):
                    s = jnp.dot(kaug_ref[h, j, r0:r0 + n, :], qa[h], preferred_element_type=F32)
                    if kind == "diag":
                        kpos = r0 + lax.broadcasted_iota(jnp.int32, (n, t), 0)
                        qpos = lax.broadcasted_iota(jnp.int32, (n, t), 1)
                        s = jnp.where(kpos <= qpos, s, NEG_INF)
                    return s
                _store_scores(score_rows, s_ref, mx_ref, h)
        return produce

    def consume():
        for h in range(hp):
            _softmax_step(s_ref, mx_ref, m_ref, alpha_ref, p_ref, h)

    def pv(j):
        for h in range(hp):
            _pv_step(vaug_ref[h, j], p_ref, alpha_ref, acc_ref, h)

    def close_tile(qi):
        for h in range(hp):
            o = acc_ref[h, 0:HEAD_DIM] * (1.0 / acc_ref[h, HEAD_DIM:HEAD_DIM + 1])
            g = g_ref[0, qi, h * HEAD_DIM:(h + 1) * HEAD_DIM, :].astype(F32)
            o_ref[0, qi, h * HEAD_DIM:(h + 1) * HEAD_DIM, :] = (o * _silu(g)).astype(BF16)

    _run_row(n_chunks, ("diag",), prepare, make_produce, consume, pv, close_tile,
             functools.partial(_reset_softmax_state, m_ref, acc_ref))


def _fox_attn(q_c, k_c, v_c, c_c, g_c):
    b, n_chunks, width, t = q_c.shape
    hp = FOX_HEADS_PER_STEP
    n_heads = width // HEAD_DIM
    rows = hp * HEAD_DIM
    c_c = c_c.reshape(b, n_chunks, n_heads // hp, hp, t)
    full = pl.BlockSpec((1, n_chunks, rows, t), lambda bi, hg: (bi, 0, hg, 0))
    return pl.pallas_call(
        functools.partial(_fox_attn_kernel, hp=hp, n_chunks=n_chunks),
        grid=(b, n_heads // hp),
        in_specs=[
            full, full, full,
            pl.BlockSpec((1, n_chunks, 1, hp, t), lambda bi, hg: (bi, 0, hg, 0, 0)),
            full,
        ],
        out_specs=full,
        out_shape=jax.ShapeDtypeStruct((b, n_chunks, width, t), BF16),
        scratch_shapes=[
            pltpu.VMEM((hp, n_chunks, t, 2 * HEAD_DIM), BF16),
            pltpu.VMEM((hp, n_chunks, HEAD_DIM + ONES_ROWS, t), BF16),
            pltpu.VMEM((hp, 1, t), F32),
            pltpu.VMEM((hp, 1, t), F32),
            pltpu.VMEM((hp, 1, t), F32),
            pltpu.VMEM((hp, HEAD_DIM + ONES_ROWS, t), F32),
            pltpu.VMEM((hp, t, t), F32),
            pltpu.VMEM((hp, t, t), BF16),
        ],
        compiler_params=_params(2),
        name="fox_attn",
    )(q_c, k_c, v_c, c_c, g_c)


def _mid_kernel(o_ref, x_ref, wo_ref, gkv_ref, gb_ref, wkv_ref, wb_ref, gk_ref, gq_ref,
                h_ref, k_ref, v_ref, q_ref, g_ref, *, n_heads):
    half = n_heads * HEAD_DIM
    h1 = x_ref[0] + lax.dot_general(o_ref[0, 0], wo_ref[...], _TN, preferred_element_type=F32)
    h_ref[0] = h1
    ms = jnp.mean(h1 * h1, axis=-1, keepdims=True)
    hn = h1 * lax.rsqrt(ms + RMS_EPS)
    u_kv = (hn * gkv_ref[...]).astype(BF16)
    u_b = (hn * gb_ref[...]).astype(BF16)

    def pack_heads(ref, a, b2):
        for h in range(n_heads):
            sl = slice(h * HEAD_DIM, (h + 1) * HEAD_DIM)
            ref[..., h * 2 * HEAD_DIM:h * 2 * HEAD_DIM + HEAD_DIM, :] = a[sl]
            ref[..., h * 2 * HEAD_DIM + HEAD_DIM:(h + 1) * 2 * HEAD_DIM, :] = b2[sl]

    kv = lax.dot_general(wkv_ref[...], u_kv, _NT, preferred_element_type=F32)
    k1 = _head_norm(kv[0:half], gk_ref[0], n_heads, 1.0).astype(BF16)
    k2 = _head_norm(kv[half:2 * half], gk_ref[1], n_heads, 1.0).astype(BF16)
    pack_heads(k_ref.at[0, 0], k1, k2)
    v_ref[0, 0] = kv[2 * half:].astype(BF16)

    qg = lax.dot_general(wb_ref[...], u_b, _NT, preferred_element_type=F32)
    q1 = _head_norm(qg[0:half], gq_ref[0], n_heads, QK_SCALE).astype(BF16)
    q2 = _head_norm(qg[half:2 * half], gq_ref[1], n_heads, QK_SCALE).astype(BF16)
    pack_heads(q_ref.at[0, 0], q1, q2)
    g_ref[0, 0] = qg[2 * half:].astype(BF16)


def _mid(o_c, x, w_out, kv_norm, kv_w, kv_k_norm, b_norm, b_w_in, b_q_norm):
    b, s, d = x.shape
    tm = ROW_TILE
    n_heads = kv_w.shape[1] // (4 * HEAD_DIM)
    width = 2 * n_heads * HEAD_DIM
    const2 = lambda *_: (0, 0)
    const3 = lambda *_: (0, 0, 0)
    tok = pl.BlockSpec((1, tm, d), lambda bi, i: (bi, i, 0))
    chunked = pl.BlockSpec((1, 1, width, tm), lambda bi, i: (bi, i, 0, 0))
    return pl.pallas_call(
        functools.partial(_mid_kernel, n_heads=n_heads),
        grid=(b, s // tm),
        in_specs=[
            pl.BlockSpec((1, 1, o_c.shape[2], tm), lambda bi, i: (bi, i, 0, 0)),
            tok,
            pl.BlockSpec(w_out.shape, const2),
            pl.BlockSpec((1, d), const2),
            pl.BlockSpec((1, d), const2),
            pl.BlockSpec((2 * width, d), const2),
            pl.BlockSpec((2 * width, d), const2),
            pl.BlockSpec((2, HEAD_DIM, 1), const3),
            pl.BlockSpec((2, HEAD_DIM, 1), const3),
        ],
        out_specs=[tok, chunked, chunked, chunked, chunked],
        out_shape=[
            jax.ShapeDtypeStruct((b, s, d), F32),
            jax.ShapeDtypeStruct((b, s // tm, width, tm), BF16),
            jax.ShapeDtypeStruct((b, s // tm, width, tm), BF16),
            jax.ShapeDtypeStruct((b, s // tm, width, tm), BF16),
            jax.ShapeDtypeStruct((b, s // tm, width, tm), BF16),
        ],
        compiler_params=_params(2),
        name="mid_proj",
    )(o_c, x, w_out.astype(BF16), kv_norm.reshape(1, d), b_norm.reshape(1, d),
      kv_w.T.astype(BF16), b_w_in.T.astype(BF16),
      kv_k_norm.reshape(2, HEAD_DIM, 1), b_q_norm.reshape(2, HEAD_DIM, 1))


def _diff_attn_kernel(q_ref, k_ref, v_ref, brow_ref, lam_ref, gs_ref, g_ref, o_ref,
                      ktok_ref, vaug_ref, bias_ref, m_ref, alpha_ref, mx_ref, acc_ref, s_ref, p_ref,
                      *, hd, n_chunks):
    t = q_ref.shape[-1]
    dv = DIFF_V_DIM
    streams = [(h, mp) for h in range(hd) for mp in range(2)]

    def prepare(j):
        for h in range(hd):
            ktok_ref[h, j] = k_ref[0, j, h * dv:(h + 1) * dv, :].astype(F32).T.astype(BF16)
            vaug_ref[h, j] = _with_ones_rows(v_ref[0, j, h * dv:(h + 1) * dv, :])

    for h, mp in streams:
        for off in range(2):
            rows = jnp.broadcast_to(brow_ref[h, mp, off], (t, 2 * t))
            bias_ref[h, mp, off] = pltpu.roll(rows, 0, 1, stride=1, stride_axis=0)[:, 0:t]

    def make_produce(qi):
        qa = []
        for h in range(hd):
            q = q_ref[0, qi, h * dv:(h + 1) * dv, :]
            row = lax.broadcasted_iota(jnp.int32, q.shape, 0)
            zero = jnp.zeros_like(q)
            qa += [jnp.where(row < HEAD_DIM, q, zero), jnp.where(row >= HEAD_DIM, q, zero)]

        def produce(j, kind):
            for i, (h, mp) in enumerate(streams):
                def score_rows(r0, n, i=i, h=h, mp=mp):
                    s = jnp.dot(ktok_ref[h, j, r0:r0 + n, :], qa[i], preferred_element_type=F32)
                    if kind == "near":
                        s = s + bias_ref[h, mp, 1, r0:r0 + n, :]
                    elif kind == "diag":
                        kchunk = (r0 + lax.broadcasted_iota(jnp.int32, (n, t), 0)) // CHUNK
                        qchunk = lax.broadcasted_iota(jnp.int32, (n, t), 1) // CHUNK
                        s = jnp.where(kchunk <= qchunk, s + bias_ref[h, mp, 0, r0:r0 + n, :], NEG_INF)
                    return s
                _store_scores(score_rows, s_ref, mx_ref, i)
        return produce

    def consume():
        for i in range(len(streams)):
            _softmax_step(s_ref, mx_ref, m_ref, alpha_ref, p_ref, i)

    def pv(j):
        for i, (h, mp) in enumerate(streams):
            _pv_step(vaug_ref[h, j], p_ref, alpha_ref, acc_ref, i)

    def close_tile(qi):
        lam_p = lam_ref[...]
        e1 = jnp.exp(jnp.sum(lam_p[0:1] * lam_p[1:2], axis=1, keepdims=True))
        e2 = jnp.exp(jnp.sum(lam_p[2:3] * lam_p[3:4], axis=1, keepdims=True))
        lam = e1 - e2 + LAMBDA_INIT
        for h in range(hd):
            a1, a2 = acc_ref[2 * h], acc_ref[2 * h + 1]
            o = a1[0:dv] * (1.0 / a1[dv:dv + 1]) - lam * (a2[0:dv] * (1.0 / a2[dv:dv + 1]))
            ms = jnp.mean(o * o, axis=0, keepdims=True)
            y = o * lax.rsqrt(ms + RMS_EPS) * gs_ref[...] * (1.0 - LAMBDA_INIT)
            g = g_ref[0, qi, h * dv:(h + 1) * dv, :].astype(F32)
            o_ref[0, qi, h * dv:(h + 1) * dv, :] = (y * _silu(g)).astype(BF16)

    _run_row(n_chunks, ("near", "diag"), prepare, make_produce, consume, pv, close_tile,
             functools.partial(_reset_softmax_state, m_ref, acc_ref))


def _t5_bucket(rel):
    half = NUM_BUCKETS // 2
    max_exact = half // 2
    ret = jnp.where(rel > 0, half, 0)
    n = jnp.abs(rel)
    n_f = jnp.maximum(n, 1).astype(jnp.float32)
    large = max_exact + (jnp.log(n_f / max_exact) / math.log(MAX_DISTANCE / max_exact)
                         * (half - max_exact)).astype(jnp.int32)
    large = jnp.minimum(large, half - 1)
    return ret + jnp.where(n < max_exact, n, large)


def _bias_rows(rel_bias, t):
    dw = jnp.arange(2 * t)
    d = jnp.where(dw < t, dw, dw - 2 * t)
    rel = jnp.stack([-d, -d - t])
    rows = rel_bias[_t5_bucket(rel)].astype(F32)
    far = rel_bias[_t5_bucket(jnp.int32(-MAX_DISTANCE))].astype(F32)
    return ((rows - far) * LOG2E).transpose(3, 2, 0, 1)[:, :, :, None, :]


def _diff_attn(q_c, k_c, v_c, g_c, rel_bias, lam_params, g_sub):
    b, n_chunks, width, t = q_c.shape
    n_heads = width // DIFF_V_DIM
    hd = DIFF_HEADS_PER_STEP
    rows = hd * DIFF_V_DIM
    n_streams = 2 * hd
    full = pl.BlockSpec((1, n_chunks, rows, t), lambda bi, hg: (bi, 0, hg, 0))
    return pl.pallas_call(
        functools.partial(_diff_attn_kernel, hd=hd, n_chunks=n_chunks),
        grid=(b, n_heads // hd),
        in_specs=[
            full, full, full,
            pl.BlockSpec((hd, 2, 2, 1, 2 * t), lambda bi, hg: (hg, 0, 0, 0, 0)),
            pl.BlockSpec((4, HEAD_DIM), lambda *_: (0, 0)),
            pl.BlockSpec((DIFF_V_DIM, 1), lambda *_: (0, 0)),
            full,
        ],
        out_specs=full,
        out_shape=jax.ShapeDtypeStruct((b, n_chunks, width, t), BF16),
        scratch_shapes=[
            pltpu.VMEM((hd, n_chunks, t, DIFF_V_DIM), BF16),
            pltpu.VMEM((hd, n_chunks, DIFF_V_DIM + ONES_ROWS, t), BF16),
            pltpu.VMEM((hd, 2, 2, t, t), F32),
            pltpu.VMEM((n_streams, 1, t), F32),
            pltpu.VMEM((n_streams, 1, t), F32),
            pltpu.VMEM((n_streams, 1, t), F32),
            pltpu.VMEM((n_streams, DIFF_V_DIM + ONES_ROWS, t), F32),
            pltpu.VMEM((n_streams, t, t), F32),
            pltpu.VMEM((n_streams, t, t), BF16),
        ],
        compiler_params=_params(2),
        name="diff_attn",
    )(q_c, k_c, v_c, _bias_rows(rel_bias, t), lam_params, g_sub.reshape(DIFF_V_DIM, 1), g_c)


def _out_kernel(o_ref, h_ref, w_ref, y_ref):
    y_ref[0] = h_ref[0] + lax.dot_general(o_ref[0, 0], w_ref[...], _TN, preferred_element_type=F32)


def _out_proj(o_c, h1, w_out):
    b, s, d = h1.shape
    tm = ROW_TILE
    tok = pl.BlockSpec((1, tm, d), lambda bi, i: (bi, i, 0))
    return pl.pallas_call(
        _out_kernel,
        grid=(b, s // tm),
        in_specs=[
            pl.BlockSpec((1, 1, o_c.shape[2], tm), lambda bi, i: (bi, i, 0, 0)),
            tok,
            pl.BlockSpec(w_out.shape, lambda *_: (0, 0)),
        ],
        out_specs=tok,
        out_shape=jax.ShapeDtypeStruct((b, s, d), F32),
        compiler_params=_params(2),
        name="out_proj",
    )(o_c, h1, w_out.astype(BF16))


def kernel(x, a_norm, a_w_in, a_b_f, a_q_norm, a_k_norm, a_w_out, kv_norm, kv_w, kv_k_norm,
           rel_bias, b_norm, b_w_in, b_q_norm, b_lam_q1, b_lam_k1, b_lam_q2, b_lam_k2,
           b_sub_norm, b_w_out):
    assert a_norm.shape[0] == 1 and b_norm.shape[0] == 1
    assert x.shape[1] % ATTN_TILE == 0 and ATTN_TILE == ROW_TILE
    q_c, k_c, v_c, g_c, c_c = _fox_proj(x, a_norm[0], a_w_in[0], a_b_f[0], a_q_norm[0], a_k_norm[0])
    o_c = _fox_attn(q_c, k_c, v_c, c_c, g_c)
    h1, k2_c, v2_c, q2_c, g2_c = _mid(o_c, x, a_w_out[0], kv_norm, kv_w, kv_k_norm,
                                      b_norm[0], b_w_in[0], b_q_norm[0])
    lam_params = jnp.stack([b_lam_q1[0], b_lam_k1[0], b_lam_q2[0], b_lam_k2[0]])
    o2_c = _diff_attn(q2_c, k2_c, v2_c, g2_c, rel_bias, lam_params, b_sub_norm[0])
    return _out_proj(o2_c, h1, b_w_out[0])
```

```python
import functools
import math

import jax
import jax.numpy as jnp
from jax import lax
from jax.experimental import pallas as pl
from jax.experimental.pallas import tpu as pltpu

HEAD_DIM = 64
DIFF_V_DIM = 2 * HEAD_DIM
CHUNK = 64
NUM_BUCKETS = 32
MAX_DISTANCE = 128
RMS_EPS = 1e-6
NEG_INF = -1e30
LOG2E = math.log2(math.e)
QK_SCALE = HEAD_DIM ** -0.5 * LOG2E
LAMBDA_INIT = 0.8 - 0.6 * math.exp(-0.3 * 1)

ROW_TILE = 512
ATTN_TILE = 512
FOX_HEADS_PER_STEP = 2
DIFF_HEADS_PER_STEP = 1
ONES_ROWS = 16
SCORE_SPLIT = 1
SCORE_LANE_PAD = 128
FAR_STEPS_PER_BLOCK = 2
V7X_VMEM_LIMIT_BYTES = 56 * 1024 * 1024

F32 = jnp.float32
BF16 = jnp.bfloat16
_NT = (((1,), (1,)), ((), ()))
_TN = (((0,), (0,)), ((), ()))


def _params(n_axes):
    return pltpu.CompilerParams(
        dimension_semantics=("arbitrary",) * n_axes,
        vmem_limit_bytes=V7X_VMEM_LIMIT_BYTES)


def _head_norm(p_t, g_col, n_heads, scale):
    t = p_t.shape[-1]
    p3 = p_t.reshape(n_heads, HEAD_DIM, t)
    ms = jnp.mean(p3 * p3, axis=1, keepdims=True)
    y = p3 * lax.rsqrt(ms + RMS_EPS) * (g_col * scale)[None]
    return y.reshape(n_heads * HEAD_DIM, t)


def _split3(c):
    hi = c.astype(BF16).astype(F32)
    r = c - hi
    mid = r.astype(BF16).astype(F32)
    lo = (r - mid).astype(BF16).astype(F32)
    return hi, mid, lo


def _silu(g):
    return g / (1.0 + jnp.exp(-g))


def _fox_proj_kernel(x_ref, gn_ref, wq_ref, wk_ref, wv_ref, wg_ref, wf_ref, bf_ref,
                     gq_ref, gk_ref, q_ref, k_ref, v_ref, g_ref, c_ref, carry_ref,
                     *, n_heads):
    @pl.when(pl.program_id(1) == 0)
    def _():
        carry_ref[...] = jnp.zeros_like(carry_ref)

    x = x_ref[0]
    tm = x.shape[0]
    ms = jnp.mean(x * x, axis=-1, keepdims=True)
    u = (x * lax.rsqrt(ms + RMS_EPS) * gn_ref[...]).astype(BF16)

    def proj(w_ref):
        return lax.dot_general(w_ref[...], u, _NT, preferred_element_type=F32)

    z = proj(wf_ref) + bf_ref[...]
    log_f = jnp.minimum(z, 0.0) - jnp.log1p(jnp.exp(-jnp.abs(z)))
    lane = lax.broadcasted_iota(jnp.int32, log_f.shape, 1)
    cs = log_f
    shift = 1
    while shift < tm:
        cs = cs + jnp.where(lane >= shift, pltpu.roll(cs, shift, axis=1), 0.0)
        shift *= 2
    c = cs + carry_ref[:, 0:1]
    c_ref[0, 0] = c * LOG2E
    carry_ref[...] = jnp.broadcast_to(c[:, tm - 1:tm], carry_ref.shape)

    q_ref[0, 0] = _head_norm(proj(wq_ref), gq_ref[...], n_heads, QK_SCALE).astype(BF16)
    k_ref[0, 0] = _head_norm(proj(wk_ref), gk_ref[...], n_heads, 1.0).astype(BF16)
    v_ref[0, 0] = proj(wv_ref).astype(BF16)
    g_ref[0, 0] = proj(wg_ref).astype(BF16)


def _fox_proj(x, a_norm, w_in, b_f, g_q, g_k):
    b, s, d = x.shape
    n_heads = b_f.shape[0]
    width = n_heads * HEAD_DIM
    tm = ROW_TILE
    w_t = w_in.T.astype(BF16)
    wq, wk, wv, wg, wf = (w_t[0:width], w_t[width:2 * width], w_t[2 * width:3 * width],
                          w_t[3 * width:4 * width], w_t[4 * width:])
    const = lambda *_: (0, 0)
    w_spec = pl.BlockSpec((width, d), const)
    col = pl.BlockSpec((HEAD_DIM, 1), const)
    chunked = pl.BlockSpec((1, 1, width, tm), lambda bi, i: (bi, i, 0, 0))
    return pl.pallas_call(
        functools.partial(_fox_proj_kernel, n_heads=n_heads),
        grid=(b, s // tm),
        in_specs=[
            pl.BlockSpec((1, tm, d), lambda bi, i: (bi, i, 0)),
            pl.BlockSpec((1, d), const),
            w_spec, w_spec, w_spec, w_spec,
            pl.BlockSpec((n_heads, d), const),
            pl.BlockSpec((n_heads, 1), const),
            col, col,
        ],
        out_specs=[chunked, chunked, chunked, chunked,
                   pl.BlockSpec((1, 1, n_heads, tm), lambda bi, i: (bi, i, 0, 0))],
        out_shape=[
            jax.ShapeDtypeStruct((b, s // tm, width, tm), BF16),
            jax.ShapeDtypeStruct((b, s // tm, width, tm), BF16),
            jax.ShapeDtypeStruct((b, s // tm, width, tm), BF16),
            jax.ShapeDtypeStruct((b, s // tm, width, tm), BF16),
            jax.ShapeDtypeStruct((b, s // tm, n_heads, tm), F32),
        ],
        scratch_shapes=[pltpu.VMEM((n_heads, 128), F32)],
        compiler_params=_params(2),
        name="fox_proj",
    )(x, a_norm.reshape(1, d), wq, wk, wv, wg, wf, b_f.reshape(n_heads, 1),
      g_q.reshape(HEAD_DIM, 1), g_k.reshape(HEAD_DIM, 1))


def _store_scores(score_rows, s_ref, mx_ref, idx):
    rows = s_ref.shape[1] // SCORE_SPLIT
    t = mx_ref.shape[-1]
    mx = None
    for r in range(SCORE_SPLIT):
        s = score_rows(r * rows, rows)
        s_ref[idx, r * rows:(r + 1) * rows, 0:t] = s
        part = jnp.max(s, axis=0, keepdims=True)
        mx = part if mx is None else jnp.maximum(mx, part)
    mx_ref[idx] = mx


def _softmax_step(s_ref, mx_ref, m_ref, alpha_ref, p_ref, idx):
    m_prev = m_ref[idx]
    m_new = jnp.maximum(m_prev, mx_ref[idx])
    t = m_prev.shape[-1]
    p_ref[idx, :, 0:t] = jnp.exp2(s_ref[idx, :, 0:t] - m_new).astype(BF16)
    alpha_ref[idx] = jnp.exp2(m_prev - m_new)
    m_ref[idx] = m_new


def _with_ones_rows(v):
    return jnp.concatenate([v, jnp.ones((ONES_ROWS, v.shape[1]), v.dtype)], axis=0)


def _run_row(n_q, tail, prepare, make_produce, consume, pv, close_tile, open_tile):
    n_tail = len(tail)
    assert n_q > n_tail + 1
    prepare(0)

    def kinds_of(q_tile):
        return (("far",) * (q_tile + 1) + tail)[-(q_tile + 1):] if q_tile + 1 > n_tail else tail[n_tail - 1 - q_tile:]

    def first_kind(q_tile):
        return "far" if q_tile >= n_tail else tail[n_tail - 1 - q_tile]

    def query_tile(qi, kinds):
        produce = make_produce(qi)
        successor = min(qi + 1, n_q - 1) if isinstance(qi, int) else jnp.minimum(qi + 1, n_q - 1)
        produce_next = functools.partial(make_produce(successor), 0)

        def step(j, produce_following):
            pv(j - 1)
            consume()
            produce_following()

        def first_step(produce_following):
            if not (isinstance(qi, int) and qi == 0):
                pv(qi - 1)
                close_tile(qi - 1)
            open_tile()
            consume()
            produce_following()
            prepare(successor)

        def rest(first_tile, produced_kinds, following_first_kind):
            j = first_tile
            for kind in produced_kinds:
                step(j, functools.partial(produce, j + 1, kind))
                j = j + 1
            step(j, functools.partial(produce_next, following_first_kind))

        if kinds is not None:
            if qi == 0:
                produce(0, kinds[0])
                first_step(functools.partial(produce_next, first_kind(1)))
            else:
                first_step(functools.partial(produce, 1, kinds[1]))
                rest(1, kinds[2:], first_kind(qi + 1))
            return

        first_step(functools.partial(produce, 1, "far"))

        n_far = qi - n_tail - 1

        def far_steps(first, count):
            for j in range(count):
                step(first + j, functools.partial(produce, first + j + 1, "far"))

        def far_block(i, carry):
            far_steps(1 + FAR_STEPS_PER_BLOCK * i, FAR_STEPS_PER_BLOCK)
            return carry
        lax.fori_loop(0, n_far // FAR_STEPS_PER_BLOCK, far_block, 0)

        done = n_far - n_far % FAR_STEPS_PER_BLOCK
        size = FAR_STEPS_PER_BLOCK // 2
        while size:
            @pl.when((n_far & size) != 0)
            def _(size=size, done=done):
                far_steps(1 + done, size)
            done = done + (n_far & size)
            size //= 2
        rest(qi - n_tail, tail, "far")

    for qi in range(n_tail + 1):
        query_tile(qi, kinds_of(qi))

    def general_tile(qi, carry):
        query_tile(qi, None)
        return carry
    lax.fori_loop(n_tail + 1, n_q, general_tile, 0)

    pv(n_q - 1)
    close_tile(n_q - 1)


def _pv_step(v, p_ref, alpha_ref, acc_ref, idx):
    t = alpha_ref.shape[-1]
    acc_ref[idx] = alpha_ref[idx] * acc_ref[idx] + jnp.dot(
        v, p_ref[idx, :, 0:t], preferred_element_type=F32)


def _reset_softmax_state(m_ref, acc_ref):
    m_ref[...] = jnp.full(m_ref.shape, NEG_INF, F32)
    acc_ref[...] = jnp.zeros_like(acc_ref)


def _aug_rows(rows8, shape):
    row = lax.broadcasted_iota(jnp.int32, shape, 0)
    out = jnp.zeros(shape, F32)
    for r, val in enumerate(rows8):
        out = jnp.where(row == r, val, out)
    return out


def _fox_attn_kernel(q_ref, k_ref, v_ref, c_ref, g_ref, o_ref,
                     kaug_ref, vaug_ref, m_ref, alpha_ref, mx_ref, acc_ref, s_ref, p_ref,
                     *, hp, n_chunks):
    t = q_ref.shape[-1]
    pad = jnp.zeros((2 * HEAD_DIM - HEAD_DIM - 8, t), F32)

    def prepare(j):
        for h in range(hp):
            kt = k_ref[0, j, h * HEAD_DIM:(h + 1) * HEAD_DIM, :].astype(F32)
            hi, mid, lo = _split3(c_ref[0, j, 0, h:h + 1, :])
            aug = _aug_rows([-hi, -mid, -lo, 1.0, 1.0, 1.0], (8, t))
            kaug_ref[h, j] = jnp.concatenate([kt, aug, pad], axis=0).T.astype(BF16)
            vaug_ref[h, j] = _with_ones_rows(v_ref[0, j, h * HEAD_DIM:(h + 1) * HEAD_DIM, :])

    def make_produce(qi):
        qa = []
        for h in range(hp):
            qf = q_ref[0, qi, h * HEAD_DIM:(h + 1) * HEAD_DIM, :].astype(F32)
            hi, mid, lo = _split3(c_ref[0, qi, 0, h:h + 1, :])
            aug = _aug_rows([1.0, 1.0, 1.0, hi, mid, lo], (8, t))
            qa.append(jnp.concatenate([qf, aug, pad], axis=0).astype(BF16))

        def produce(j, kind):
            for h in range(hp):
                def score_rows(r0, n, h=h):
                    s = jnp.dot(kaug_ref[h, j, r0:r0 + n, :], qa[h], preferred_element_type=F32)
                    if kind == "diag":
                        kpos = r0 + lax.broadcasted_iota(jnp.int32, (n, t), 0)
                        qpos = lax.broadcasted_iota(jnp.int32, (n, t), 1)
                        s = jnp.where(kpos <= qpos, s, NEG_INF)
                    return s
                _store_scores(score_rows, s_ref, mx_ref, h)
        return produce

    def consume():
        for h in range(hp):
            _softmax_step(s_ref, mx_ref, m_ref, alpha_ref, p_ref, h)

    def pv(j):
        for h in range(hp):
            _pv_step(vaug_ref[h, j], p_ref, alpha_ref, acc_ref, h)

    def close_tile(qi):
        for h in range(hp):
            o = acc_ref[h, 0:HEAD_DIM] * (1.0 / acc_ref[h, HEAD_DIM:HEAD_DIM + 1])
            g = g_ref[0, qi, h * HEAD_DIM:(h + 1) * HEAD_DIM, :].astype(F32)
            o_ref[0, qi, h * HEAD_DIM:(h + 1) * HEAD_DIM, :] = (o * _silu(g)).astype(BF16)

    _run_row(n_chunks, ("diag",), prepare, make_produce, consume, pv, close_tile,
             functools.partial(_reset_softmax_state, m_ref, acc_ref))


def _fox_attn(q_c, k_c, v_c, c_c, g_c):
    b, n_chunks, width, t = q_c.shape
    hp = FOX_HEADS_PER_STEP
    n_heads = width // HEAD_DIM
    rows = hp * HEAD_DIM
    c_c = c_c.reshape(b, n_chunks, n_heads // hp, hp, t)
    full = pl.BlockSpec((1, n_chunks, rows, t), lambda bi, hg: (bi, 0, hg, 0))
    return pl.pallas_call(
        functools.partial(_fox_attn_kernel, hp=hp, n_chunks=n_chunks),
        grid=(b, n_heads // hp),
        in_specs=[
            full, full, full,
            pl.BlockSpec((1, n_chunks, 1, hp, t), lambda bi, hg: (bi, 0, hg, 0, 0)),
            full,
        ],
        out_specs=full,
        out_shape=jax.ShapeDtypeStruct((b, n_chunks, width, t), BF16),
        scratch_shapes=[
            pltpu.VMEM((hp, n_chunks, t, 2 * HEAD_DIM), BF16),
            pltpu.VMEM((hp, n_chunks, HEAD_DIM + ONES_ROWS, t), BF16),
            pltpu.VMEM((hp, 1, t), F32),
            pltpu.VMEM((hp, 1, t), F32),
            pltpu.VMEM((hp, 1, t), F32),
            pltpu.VMEM((hp, HEAD_DIM + ONES_ROWS, t), F32),
            pltpu.VMEM((hp, t, t + SCORE_LANE_PAD), F32),
            pltpu.VMEM((hp, t, t + SCORE_LANE_PAD), BF16),
        ],
        compiler_params=_params(2),
        name="fox_attn",
    )(q_c, k_c, v_c, c_c, g_c)


def _mid_kernel(o_ref, x_ref, wo_ref, gkv_ref, gb_ref, wkv_ref, wb_ref, gk_ref, gq_ref,
                h_ref, k_ref, v_ref, q_ref, g_ref, *, n_heads):
    half = n_heads * HEAD_DIM
    h1 = x_ref[0] + lax.dot_general(o_ref[0, 0], wo_ref[...], _TN, preferred_element_type=F32)
    h_ref[0] = h1
    ms = jnp.mean(h1 * h1, axis=-1, keepdims=True)
    hn = h1 * lax.rsqrt(ms + RMS_EPS)
    u_kv = (hn * gkv_ref[...]).astype(BF16)
    u_b = (hn * gb_ref[...]).astype(BF16)

    def pack_heads(ref, a, b2):
        for h in range(n_heads):
            sl = slice(h * HEAD_DIM, (h + 1) * HEAD_DIM)
            ref[..., h * 2 * HEAD_DIM:h * 2 * HEAD_DIM + HEAD_DIM, :] = a[sl]
            ref[..., h * 2 * HEAD_DIM + HEAD_DIM:(h + 1) * 2 * HEAD_DIM, :] = b2[sl]

    kv = lax.dot_general(wkv_ref[...], u_kv, _NT, preferred_element_type=F32)
    k1 = _head_norm(kv[0:half], gk_ref[0], n_heads, 1.0).astype(BF16)
    k2 = _head_norm(kv[half:2 * half], gk_ref[1], n_heads, 1.0).astype(BF16)
    pack_heads(k_ref.at[0, 0], k1, k2)
    v_ref[0, 0] = kv[2 * half:].astype(BF16)

    qg = lax.dot_general(wb_ref[...], u_b, _NT, preferred_element_type=F32)
    q1 = _head_norm(qg[0:half], gq_ref[0], n_heads, QK_SCALE).astype(BF16)
    q2 = _head_norm(qg[half:2 * half], gq_ref[1], n_heads, QK_SCALE).astype(BF16)
    pack_heads(q_ref.at[0, 0], q1, q2)
    g_ref[0, 0] = qg[2 * half:].astype(BF16)


def _mid(o_c, x, w_out, kv_norm, kv_w, kv_k_norm, b_norm, b_w_in, b_q_norm):
    b, s, d = x.shape
    tm = ROW_TILE
    n_heads = kv_w.shape[1] // (4 * HEAD_DIM)
    width = 2 * n_heads * HEAD_DIM
    const2 = lambda *_: (0, 0)
    const3 = lambda *_: (0, 0, 0)
    tok = pl.BlockSpec((1, tm, d), lambda bi, i: (bi, i, 0))
    chunked = pl.BlockSpec((1, 1, width, tm), lambda bi, i: (bi, i, 0, 0))
    return pl.pallas_call(
        functools.partial(_mid_kernel, n_heads=n_heads),
        grid=(b, s // tm),
        in_specs=[
            pl.BlockSpec((1, 1, o_c.shape[2], tm), lambda bi, i: (bi, i, 0, 0)),
            tok,
            pl.BlockSpec(w_out.shape, const2),
            pl.BlockSpec((1, d), const2),
            pl.BlockSpec((1, d), const2),
            pl.BlockSpec((2 * width, d), const2),
            pl.BlockSpec((2 * width, d), const2),
            pl.BlockSpec((2, HEAD_DIM, 1), const3),
            pl.BlockSpec((2, HEAD_DIM, 1), const3),
        ],
        out_specs=[tok, chunked, chunked, chunked, chunked],
        out_shape=[
            jax.ShapeDtypeStruct((b, s, d), F32),
            jax.ShapeDtypeStruct((b, s // tm, width, tm), BF16),
            jax.ShapeDtypeStruct((b, s // tm, width, tm), BF16),
            jax.ShapeDtypeStruct((b, s // tm, width, tm), BF16),
            jax.ShapeDtypeStruct((b, s // tm, width, tm), BF16),
        ],
        compiler_params=_params(2),
        name="mid_proj",
    )(o_c, x, w_out.astype(BF16), kv_norm.reshape(1, d), b_norm.reshape(1, d),
      kv_w.T.astype(BF16), b_w_in.T.astype(BF16),
      kv_k_norm.reshape(2, HEAD_DIM, 1), b_q_norm.reshape(2, HEAD_DIM, 1))


def _diff_attn_kernel(q_ref, k_ref, v_ref, brow_ref, lam_ref, gs_ref, g_ref, o_ref,
                      ktok_ref, vaug_ref, bias_ref, m_ref, alpha_ref, mx_ref, acc_ref, s_ref, p_ref,
                      *, hd, n_chunks):
    t = q_ref.shape[-1]
    dv = DIFF_V_DIM
    streams = [(h, mp) for h in range(hd) for mp in range(2)]

    def prepare(j):
        for h in range(hd):
            ktok_ref[h, j] = k_ref[0, j, h * dv:(h + 1) * dv, :].astype(F32).T.astype(BF16)
            vaug_ref[h, j] = _with_ones_rows(v_ref[0, j, h * dv:(h + 1) * dv, :])

    for h, mp in streams:
        for off in range(2):
            rows = jnp.broadcast_to(brow_ref[h, mp, off], (t, 2 * t))
            bias_ref[h, mp, off] = pltpu.roll(rows, 0, 1, stride=1, stride_axis=0)[:, 0:t]

    def make_produce(qi):
        qa = []
        for h in range(hd):
            q = q_ref[0, qi, h * dv:(h + 1) * dv, :]
            row = lax.broadcasted_iota(jnp.int32, q.shape, 0)
            zero = jnp.zeros_like(q)
            qa += [jnp.where(row < HEAD_DIM, q, zero), jnp.where(row >= HEAD_DIM, q, zero)]

        def produce(j, kind):
            for i, (h, mp) in enumerate(streams):
                def score_rows(r0, n, i=i, h=h, mp=mp):
                    s = jnp.dot(ktok_ref[h, j, r0:r0 + n, :], qa[i], preferred_element_type=F32)
                    if kind == "near":
                        s = s + bias_ref[h, mp, 1, r0:r0 + n, :]
                    elif kind == "diag":
                        kchunk = (r0 + lax.broadcasted_iota(jnp.int32, (n, t), 0)) // CHUNK
                        qchunk = lax.broadcasted_iota(jnp.int32, (n, t), 1) // CHUNK
                        s = jnp.where(kchunk <= qchunk, s + bias_ref[h, mp, 0, r0:r0 + n, :], NEG_INF)
                    return s
                _store_scores(score_rows, s_ref, mx_ref, i)
        return produce

    def consume():
        for i in range(len(streams)):
            _softmax_step(s_ref, mx_ref, m_ref, alpha_ref, p_ref, i)

    def pv(j):
        for i, (h, mp) in enumerate(streams):
            _pv_step(vaug_ref[h, j], p_ref, alpha_ref, acc_ref, i)

    def close_tile(qi):
        lam_p = lam_ref[...]
        e1 = jnp.exp(jnp.sum(lam_p[0:1] * lam_p[1:2], axis=1, keepdims=True))
        e2 = jnp.exp(jnp.sum(lam_p[2:3] * lam_p[3:4], axis=1, keepdims=True))
        lam = e1 - e2 + LAMBDA_INIT
        for h in range(hd):
            a1, a2 = acc_ref[2 * h], acc_ref[2 * h + 1]
            o = a1[0:dv] * (1.0 / a1[dv:dv + 1]) - lam * (a2[0:dv] * (1.0 / a2[dv:dv + 1]))
            ms = jnp.mean(o * o, axis=0, keepdims=True)
            y = o * lax.rsqrt(ms + RMS_EPS) * gs_ref[...] * (1.0 - LAMBDA_INIT)
            g = g_ref[0, qi, h * dv:(h + 1) * dv, :].astype(F32)
            o_ref[0, qi, h * dv:(h + 1) * dv, :] = (y * _silu(g)).astype(BF16)

    _run_row(n_chunks, ("near", "diag"), prepare, make_produce, consume, pv, close_tile,
             functools.partial(_reset_softmax_state, m_ref, acc_ref))


def _t5_bucket(rel):
    half = NUM_BUCKETS // 2
    max_exact = half // 2
    ret = jnp.where(rel > 0, half, 0)
    n = jnp.abs(rel)
    n_f = jnp.maximum(n, 1).astype(jnp.float32)
    large = max_exact + (jnp.log(n_f / max_exact) / math.log(MAX_DISTANCE / max_exact)
                         * (half - max_exact)).astype(jnp.int32)
    large = jnp.minimum(large, half - 1)
    return ret + jnp.where(n < max_exact, n, large)


def _bias_rows(rel_bias, t):
    dw = jnp.arange(2 * t)
    d = jnp.where(dw < t, dw, dw - 2 * t)
    rel = jnp.stack([-d, -d - t])
    rows = rel_bias[_t5_bucket(rel)].astype(F32)
    far = rel_bias[_t5_bucket(jnp.int32(-MAX_DISTANCE))].astype(F32)
    return ((rows - far) * LOG2E).transpose(3, 2, 0, 1)[:, :, :, None, :]


def _diff_attn(q_c, k_c, v_c, g_c, rel_bias, lam_params, g_sub):
    b, n_chunks, width, t = q_c.shape
    n_heads = width // DIFF_V_DIM
    hd = DIFF_HEADS_PER_STEP
    rows = hd * DIFF_V_DIM
    n_streams = 2 * hd
    full = pl.BlockSpec((1, n_chunks, rows, t), lambda bi, hg: (bi, 0, hg, 0))
    return pl.pallas_call(
        functools.partial(_diff_attn_kernel, hd=hd, n_chunks=n_chunks),
        grid=(b, n_heads // hd),
        in_specs=[
            full, full, full,
            pl.BlockSpec((hd, 2, 2, 1, 2 * t), lambda bi, hg: (hg, 0, 0, 0, 0)),
            pl.BlockSpec((4, HEAD_DIM), lambda *_: (0, 0)),
            pl.BlockSpec((DIFF_V_DIM, 1), lambda *_: (0, 0)),
            full,
        ],
        out_specs=full,
        out_shape=jax.ShapeDtypeStruct((b, n_chunks, width, t), BF16),
        scratch_shapes=[
            pltpu.VMEM((hd, n_chunks, t, DIFF_V_DIM), BF16),
            pltpu.VMEM((hd, n_chunks, DIFF_V_DIM + ONES_ROWS, t), BF16),
            pltpu.VMEM((hd, 2, 2, t, t), F32),
            pltpu.VMEM((n_streams, 1, t), F32),
            pltpu.VMEM((n_streams, 1, t), F32),
            pltpu.VMEM((n_streams, 1, t), F32),
            pltpu.VMEM((n_streams, DIFF_V_DIM + ONES_ROWS, t), F32),
            pltpu.VMEM((n_streams, t, t + SCORE_LANE_PAD), F32),
            pltpu.VMEM((n_streams, t, t + SCORE_LANE_PAD), BF16),
        ],
        compiler_params=_params(2),
        name="diff_attn",
    )(q_c, k_c, v_c, _bias_rows(rel_bias, t), lam_params, g_sub.reshape(DIFF_V_DIM, 1), g_c)


def _out_kernel(o_ref, h_ref, w_ref, y_ref):
    y_ref[0] = h_ref[0] + lax.dot_general(o_ref[0, 0], w_ref[...], _TN, preferred_element_type=F32)


def _out_proj(o_c, h1, w_out):
    b, s, d = h1.shape
    tm = ROW_TILE
    tok = pl.BlockSpec((1, tm, d), lambda bi, i: (bi, i, 0))
    return pl.pallas_call(
        _out_kernel,
        grid=(b, s // tm),
        in_specs=[
            pl.BlockSpec((1, 1, o_c.shape[2], tm), lambda bi, i: (bi, i, 0, 0)),
            tok,
            pl.BlockSpec(w_out.shape, lambda *_: (0, 0)),
        ],
        out_specs=tok,
        out_shape=jax.ShapeDtypeStruct((b, s, d), F32),
        compiler_params=_params(2),
        name="out_proj",
    )(o_c, h1, w_out.astype(BF16))


def kernel(x, a_norm, a_w_in, a_b_f, a_q_norm, a_k_norm, a_w_out, kv_norm, kv_w, kv_k_norm,
           rel_bias, b_norm, b_w_in, b_q_norm, b_lam_q1, b_lam_k1, b_lam_q2, b_lam_k2,
           b_sub_norm, b_w_out):
    assert a_norm.shape[0] == 1 and b_norm.shape[0] == 1
    assert x.shape[1] % ATTN_TILE == 0 and ATTN_TILE == ROW_TILE
    q_c, k_c, v_c, g_c, c_c = _fox_proj(x, a_norm[0], a_w_in[0], a_b_f[0], a_q_norm[0], a_k_norm[0])
    o_c = _fox_attn(q_c, k_c, v_c, c_c, g_c)
    h1, k2_c, v2_c, q2_c, g2_c = _mid(o_c, x, a_w_out[0], kv_norm, kv_w, kv_k_norm,
                                      b_norm[0], b_w_in[0], b_q_norm[0])
    lam_params = jnp.stack([b_lam_q1[0], b_lam_k1[0], b_lam_q2[0], b_lam_k2[0]])
    o2_c = _diff_attn(q2_c, k2_c, v2_c, g2_c, rel_bias, lam_params, b_sub_norm[0])
    return _out_proj(o2_c, h1, b_w_out[0])
```

```python
import functools
import math

import jax
import jax.numpy as jnp
from jax import lax
from jax.experimental import pallas as pl
from jax.experimental.pallas import tpu as pltpu

HEAD_DIM = 64
DIFF_V_DIM = 2 * HEAD_DIM
CHUNK = 64
NUM_BUCKETS = 32
MAX_DISTANCE = 128
RMS_EPS = 1e-6
NEG_INF = -1e30
LOG2E = math.log2(math.e)
QK_SCALE = HEAD_DIM ** -0.5 * LOG2E
LAMBDA_INIT = 0.8 - 0.6 * math.exp(-0.3 * 1)

ROW_TILE = 512
ATTN_TILE = 512
FOX_HEADS_PER_STEP = 2
DIFF_HEADS_PER_STEP = 1
ONES_ROWS = 16
FAR_STEPS_PER_BLOCK = 2
V7X_VMEM_LIMIT_BYTES = 56 * 1024 * 1024

F32 = jnp.float32
BF16 = jnp.bfloat16
_NT = (((1,), (1,)), ((), ()))
_TN = (((0,), (0,)), ((), ()))


def _params(n_axes):
    return pltpu.CompilerParams(
        dimension_semantics=("arbitrary",) * n_axes,
        vmem_limit_bytes=V7X_VMEM_LIMIT_BYTES)


def _head_norm(p_t, g_col, n_heads, scale):
    t = p_t.shape[-1]
    p3 = p_t.reshape(n_heads, HEAD_DIM, t)
    ms = jnp.mean(p3 * p3, axis=1, keepdims=True)
    y = p3 * lax.rsqrt(ms + RMS_EPS) * (g_col * scale)[None]
    return y.reshape(n_heads * HEAD_DIM, t)


def _split3(c):
    hi = c.astype(BF16).astype(F32)
    r = c - hi
    mid = r.astype(BF16).astype(F32)
    lo = (r - mid).astype(BF16).astype(F32)
    return hi, mid, lo


def _silu(g):
    return g / (1.0 + jnp.exp(-g))


def _fox_proj_kernel(x_ref, gn_ref, wq_ref, wk_ref, wv_ref, wg_ref, wf_ref, bf_ref,
                     gq_ref, gk_ref, q_ref, k_ref, v_ref, g_ref, c_ref, carry_ref,
                     *, n_heads):
    @pl.when(pl.program_id(1) == 0)
    def _():
        carry_ref[...] = jnp.zeros_like(carry_ref)

    x = x_ref[0]
    tm = x.shape[0]
    ms = jnp.mean(x * x, axis=-1, keepdims=True)
    u = (x * lax.rsqrt(ms + RMS_EPS) * gn_ref[...]).astype(BF16)

    def proj(w_ref):
        return lax.dot_general(w_ref[...], u, _NT, preferred_element_type=F32)

    z = proj(wf_ref) + bf_ref[...]
    log_f = jnp.minimum(z, 0.0) - jnp.log1p(jnp.exp(-jnp.abs(z)))
    lane = lax.broadcasted_iota(jnp.int32, log_f.shape, 1)
    cs = log_f
    shift = 1
    while shift < tm:
        cs = cs + jnp.where(lane >= shift, pltpu.roll(cs, shift, axis=1), 0.0)
        shift *= 2
    c = cs + carry_ref[:, 0:1]
    c_ref[0, 0] = c * LOG2E
    carry_ref[...] = jnp.broadcast_to(c[:, tm - 1:tm], carry_ref.shape)

    q_ref[0, 0] = _head_norm(proj(wq_ref), gq_ref[...], n_heads, QK_SCALE).astype(BF16)
    k_ref[0, 0] = _head_norm(proj(wk_ref), gk_ref[...], n_heads, 1.0).astype(BF16)
    v_ref[0, 0] = proj(wv_ref).astype(BF16)
    g_ref[0, 0] = proj(wg_ref).astype(BF16)


def _fox_proj(x, a_norm, w_in, b_f, g_q, g_k):
    b, s, d = x.shape
    n_heads = b_f.shape[0]
    width = n_heads * HEAD_DIM
    tm = ROW_TILE
    w_t = w_in.T.astype(BF16)
    wq, wk, wv, wg, wf = (w_t[0:width], w_t[width:2 * width], w_t[2 * width:3 * width],
                          w_t[3 * width:4 * width], w_t[4 * width:])
    const = lambda *_: (0, 0)
    w_spec = pl.BlockSpec((width, d), const)
    col = pl.BlockSpec((HEAD_DIM, 1), const)
    chunked = pl.BlockSpec((1, 1, width, tm), lambda bi, i: (bi, i, 0, 0))
    return pl.pallas_call(
        functools.partial(_fox_proj_kernel, n_heads=n_heads),
        grid=(b, s // tm),
        in_specs=[
            pl.BlockSpec((1, tm, d), lambda bi, i: (bi, i, 0)),
            pl.BlockSpec((1, d), const),
            w_spec, w_spec, w_spec, w_spec,
            pl.BlockSpec((n_heads, d), const),
            pl.BlockSpec((n_heads, 1), const),
            col, col,
        ],
        out_specs=[chunked, chunked, chunked, chunked,
                   pl.BlockSpec((1, 1, n_heads, tm), lambda bi, i: (bi, i, 0, 0))],
        out_shape=[
            jax.ShapeDtypeStruct((b, s // tm, width, tm), BF16),
            jax.ShapeDtypeStruct((b, s // tm, width, tm), BF16),
            jax.ShapeDtypeStruct((b, s // tm, width, tm), BF16),
            jax.ShapeDtypeStruct((b, s // tm, width, tm), BF16),
            jax.ShapeDtypeStruct((b, s // tm, n_heads, tm), F32),
        ],
        scratch_shapes=[pltpu.VMEM((n_heads, 128), F32)],
        compiler_params=_params(2),
        name="fox_proj",
    )(x, a_norm.reshape(1, d), wq, wk, wv, wg, wf, b_f.reshape(n_heads, 1),
      g_q.reshape(HEAD_DIM, 1), g_k.reshape(HEAD_DIM, 1))


def _store_scores(score_block, s_ref, mx_ref, idx, diagonal):
    t = s_ref.shape[1]
    if not diagonal:
        s = score_block(0, t, 0, t)
        s_ref[idx] = s
        mx_ref[idx] = jnp.max(s, axis=0, keepdims=True)
        return
    half = t // 2
    top = score_block(0, half, 0, t)
    bottom = score_block(half, half, half, half)
    s_ref[idx, 0:half, :] = top
    s_ref[idx, half:t, 0:half] = jnp.full((half, half), NEG_INF, F32)
    s_ref[idx, half:t, half:t] = bottom
    top_max = jnp.max(top, axis=0, keepdims=True)
    bottom_max = jnp.max(bottom, axis=0, keepdims=True)
    mx_ref[idx] = jnp.concatenate(
        [top_max[:, 0:half], jnp.maximum(top_max[:, half:t], bottom_max)], axis=1)


def _softmax_step(s_ref, mx_ref, m_ref, alpha_ref, p_ref, idx):
    m_prev = m_ref[idx]
    m_new = jnp.maximum(m_prev, mx_ref[idx])
    p_ref[idx] = jnp.exp2(s_ref[idx] - m_new).astype(BF16)
    alpha_ref[idx] = jnp.exp2(m_prev - m_new)
    m_ref[idx] = m_new


def _with_ones_rows(v):
    return jnp.concatenate([v, jnp.ones((ONES_ROWS, v.shape[1]), v.dtype)], axis=0)


def _run_row(n_q, tail, prepare, make_produce, consume, pv, close_tile, open_tile):
    n_tail = len(tail)
    assert n_q > n_tail + 1
    prepare(0)

    def kinds_of(q_tile):
        return (("far",) * (q_tile + 1) + tail)[-(q_tile + 1):] if q_tile + 1 > n_tail else tail[n_tail - 1 - q_tile:]

    def first_kind(q_tile):
        return "far" if q_tile >= n_tail else tail[n_tail - 1 - q_tile]

    def query_tile(qi, kinds):
        produce = make_produce(qi)
        successor = min(qi + 1, n_q - 1) if isinstance(qi, int) else jnp.minimum(qi + 1, n_q - 1)
        produce_next = functools.partial(make_produce(successor), 0)

        def step(j, produce_following):
            pv(j - 1)
            consume()
            produce_following()

        def first_step(produce_following):
            if not (isinstance(qi, int) and qi == 0):
                pv(qi - 1, diagonal=True)
                close_tile(qi - 1)
            open_tile()
            consume()
            produce_following()
            prepare(successor)

        def rest(first_tile, produced_kinds, following_first_kind):
            j = first_tile
            for kind in produced_kinds:
                step(j, functools.partial(produce, j + 1, kind))
                j = j + 1
            step(j, functools.partial(produce_next, following_first_kind))

        if kinds is not None:
            if qi == 0:
                produce(0, kinds[0])
                first_step(functools.partial(produce_next, first_kind(1)))
            else:
                first_step(functools.partial(produce, 1, kinds[1]))
                rest(1, kinds[2:], first_kind(qi + 1))
            return

        first_step(functools.partial(produce, 1, "far"))

        n_far = qi - n_tail - 1

        def far_steps(first, count):
            for j in range(count):
                step(first + j, functools.partial(produce, first + j + 1, "far"))

        def far_block(i, carry):
            far_steps(1 + FAR_STEPS_PER_BLOCK * i, FAR_STEPS_PER_BLOCK)
            return carry
        lax.fori_loop(0, n_far // FAR_STEPS_PER_BLOCK, far_block, 0)

        done = n_far - n_far % FAR_STEPS_PER_BLOCK
        size = FAR_STEPS_PER_BLOCK // 2
        while size:
            @pl.when((n_far & size) != 0)
            def _(size=size, done=done):
                far_steps(1 + done, size)
            done = done + (n_far & size)
            size //= 2
        rest(qi - n_tail, tail, "far")

    for qi in range(n_tail + 1):
        query_tile(qi, kinds_of(qi))

    def general_tile(qi, carry):
        query_tile(qi, None)
        return carry
    lax.fori_loop(n_tail + 1, n_q, general_tile, 0)

    pv(n_q - 1, diagonal=True)
    close_tile(n_q - 1)


def _pv_step(v, p_ref, alpha_ref, acc_ref, idx, diagonal=False):
    if not diagonal:
        pv = jnp.dot(v, p_ref[idx], preferred_element_type=F32)
    else:
        t = p_ref.shape[1]
        half = t // 2
        top = jnp.dot(v[:, 0:half], p_ref[idx, 0:half, :], preferred_element_type=F32)
        bottom = jnp.dot(v[:, half:t], p_ref[idx, half:t, half:t], preferred_element_type=F32)
        pv = jnp.concatenate([top[:, 0:half], top[:, half:t] + bottom], axis=1)
    acc_ref[idx] = alpha_ref[idx] * acc_ref[idx] + pv


def _reset_softmax_state(m_ref, acc_ref):
    m_ref[...] = jnp.full(m_ref.shape, NEG_INF, F32)
    acc_ref[...] = jnp.zeros_like(acc_ref)


def _aug_rows(rows8, shape):
    row = lax.broadcasted_iota(jnp.int32, shape, 0)
    out = jnp.zeros(shape, F32)
    for r, val in enumerate(rows8):
        out = jnp.where(row == r, val, out)
    return out


def _fox_attn_kernel(q_ref, k_ref, v_ref, c_ref, g_ref, o_ref,
                     kaug_ref, vaug_ref, m_ref, alpha_ref, mx_ref, acc_ref, s_ref, p_ref,
                     *, hp, n_chunks):
    t = q_ref.shape[-1]
    pad = jnp.zeros((2 * HEAD_DIM - HEAD_DIM - 8, t), F32)

    def prepare(j):
        for h in range(hp):
            kt = k_ref[0, j, h * HEAD_DIM:(h + 1) * HEAD_DIM, :].astype(F32)
            hi, mid, lo = _split3(c_ref[0, j, 0, h:h + 1, :])
            aug = _aug_rows([-hi, -mid, -lo, 1.0, 1.0, 1.0], (8, t))
            kaug_ref[h, j] = jnp.concatenate([kt, aug, pad], axis=0).T.astype(BF16)
            vaug_ref[h, j] = _with_ones_rows(v_ref[0, j, h * HEAD_DIM:(h + 1) * HEAD_DIM, :])

    def make_produce(qi):
        qa = []
        for h in range(hp):
            qf = q_ref[0, qi, h * HEAD_DIM:(h + 1) * HEAD_DIM, :].astype(F32)
            hi, mid, lo = _split3(c_ref[0, qi, 0, h:h + 1, :])
            aug = _aug_rows([1.0, 1.0, 1.0, hi, mid, lo], (8, t))
            qa.append(jnp.concatenate([qf, aug, pad], axis=0).astype(BF16))

        def produce(j, kind):
            for h in range(hp):
                def score_block(r0, rn, c0, cn, h=h):
                    s = jnp.dot(kaug_ref[h, j, r0:r0 + rn, :], qa[h][:, c0:c0 + cn],
                                preferred_element_type=F32)
                    if kind == "diag":
                        kpos = r0 + lax.broadcasted_iota(jnp.int32, (rn, cn), 0)
                        qpos = c0 + lax.broadcasted_iota(jnp.int32, (rn, cn), 1)
                        s = jnp.where(kpos <= qpos, s, NEG_INF)
                    return s
                _store_scores(score_block, s_ref, mx_ref, h, kind == "diag")
        return produce

    def consume():
        for h in range(hp):
            _softmax_step(s_ref, mx_ref, m_ref, alpha_ref, p_ref, h)

    def pv(j, diagonal=False):
        for h in range(hp):
            _pv_step(vaug_ref[h, j], p_ref, alpha_ref, acc_ref, h, diagonal)

    def close_tile(qi):
        for h in range(hp):
            o = acc_ref[h, 0:HEAD_DIM] * (1.0 / acc_ref[h, HEAD_DIM:HEAD_DIM + 1])
            g = g_ref[0, qi, h * HEAD_DIM:(h + 1) * HEAD_DIM, :].astype(F32)
            o_ref[0, qi, h * HEAD_DIM:(h + 1) * HEAD_DIM, :] = (o * _silu(g)).astype(BF16)

    _run_row(n_chunks, ("diag",), prepare, make_produce, consume, pv, close_tile,
             functools.partial(_reset_softmax_state, m_ref, acc_ref))


def _fox_attn(q_c, k_c, v_c, c_c, g_c):
    b, n_chunks, width, t = q_c.shape
    hp = FOX_HEADS_PER_STEP
    n_heads = width // HEAD_DIM
    rows = hp * HEAD_DIM
    c_c = c_c.reshape(b, n_chunks, n_heads // hp, hp, t)
    full = pl.BlockSpec((1, n_chunks, rows, t), lambda bi, hg: (bi, 0, hg, 0))
    return pl.pallas_call(
        functools.partial(_fox_attn_kernel, hp=hp, n_chunks=n_chunks),
        grid=(b, n_heads // hp),
        in_specs=[
            full, full, full,
            pl.BlockSpec((1, n_chunks, 1, hp, t), lambda bi, hg: (bi, 0, hg, 0, 0)),
            full,
        ],
        out_specs=full,
        out_shape=jax.ShapeDtypeStruct((b, n_chunks, width, t), BF16),
        scratch_shapes=[
            pltpu.VMEM((hp, n_chunks, t, 2 * HEAD_DIM), BF16),
            pltpu.VMEM((hp, n_chunks, HEAD_DIM + ONES_ROWS, t), BF16),
            pltpu.VMEM((hp, 1, t), F32),
            pltpu.VMEM((hp, 1, t), F32),
            pltpu.VMEM((hp, 1, t), F32),
            pltpu.VMEM((hp, HEAD_DIM + ONES_ROWS, t), F32),
            pltpu.VMEM((hp, t, t), F32),
            pltpu.VMEM((hp, t, t), BF16),
        ],
        compiler_params=_params(2),
        name="fox_attn",
    )(q_c, k_c, v_c, c_c, g_c)


def _mid_kernel(o_ref, x_ref, wo_ref, gkv_ref, gb_ref, wkv_ref, wb_ref, gk_ref, gq_ref,
                h_ref, k_ref, v_ref, q_ref, g_ref, *, n_heads):
    half = n_heads * HEAD_DIM
    h1 = x_ref[0] + lax.dot_general(o_ref[0, 0], wo_ref[...], _TN, preferred_element_type=F32)
    h_ref[0] = h1
    ms = jnp.mean(h1 * h1, axis=-1, keepdims=True)
    hn = h1 * lax.rsqrt(ms + RMS_EPS)
    u_kv = (hn * gkv_ref[...]).astype(BF16)
    u_b = (hn * gb_ref[...]).astype(BF16)

    def pack_heads(ref, a, b2):
        for h in range(n_heads):
            sl = slice(h * HEAD_DIM, (h + 1) * HEAD_DIM)
            ref[..., h * 2 * HEAD_DIM:h * 2 * HEAD_DIM + HEAD_DIM, :] = a[sl]
            ref[..., h * 2 * HEAD_DIM + HEAD_DIM:(h + 1) * 2 * HEAD_DIM, :] = b2[sl]

    kv = lax.dot_general(wkv_ref[...], u_kv, _NT, preferred_element_type=F32)
    k1 = _head_norm(kv[0:half], gk_ref[0], n_heads, 1.0).astype(BF16)
    k2 = _head_norm(kv[half:2 * half], gk_ref[1], n_heads, 1.0).astype(BF16)
    pack_heads(k_ref.at[0, 0], k1, k2)
    v_ref[0, 0] = kv[2 * half:].astype(BF16)

    qg = lax.dot_general(wb_ref[...], u_b, _NT, preferred_element_type=F32)
    q1 = _head_norm(qg[0:half], gq_ref[0], n_heads, QK_SCALE).astype(BF16)
    q2 = _head_norm(qg[half:2 * half], gq_ref[1], n_heads, QK_SCALE).astype(BF16)
    pack_heads(q_ref.at[0, 0], q1, q2)
    g_ref[0, 0] = qg[2 * half:].astype(BF16)


def _mid(o_c, x, w_out, kv_norm, kv_w, kv_k_norm, b_norm, b_w_in, b_q_norm):
    b, s, d = x.shape
    tm = ROW_TILE
    n_heads = kv_w.shape[1] // (4 * HEAD_DIM)
    width = 2 * n_heads * HEAD_DIM
    const2 = lambda *_: (0, 0)
    const3 = lambda *_: (0, 0, 0)
    tok = pl.BlockSpec((1, tm, d), lambda bi, i: (bi, i, 0))
    chunked = pl.BlockSpec((1, 1, width, tm), lambda bi, i: (bi, i, 0, 0))
    return pl.pallas_call(
        functools.partial(_mid_kernel, n_heads=n_heads),
        grid=(b, s // tm),
        in_specs=[
            pl.BlockSpec((1, 1, o_c.shape[2], tm), lambda bi, i: (bi, i, 0, 0)),
            tok,
            pl.BlockSpec(w_out.shape, const2),
            pl.BlockSpec((1, d), const2),
            pl.BlockSpec((1, d), const2),
            pl.BlockSpec((2 * width, d), const2),
            pl.BlockSpec((2 * width, d), const2),
            pl.BlockSpec((2, HEAD_DIM, 1), const3),
            pl.BlockSpec((2, HEAD_DIM, 1), const3),
        ],
        out_specs=[tok, chunked, chunked, chunked, chunked],
        out_shape=[
            jax.ShapeDtypeStruct((b, s, d), F32),
            jax.ShapeDtypeStruct((b, s // tm, width, tm), BF16),
            jax.ShapeDtypeStruct((b, s // tm, width, tm), BF16),
            jax.ShapeDtypeStruct((b, s // tm, width, tm), BF16),
            jax.ShapeDtypeStruct((b, s // tm, width, tm), BF16),
        ],
        compiler_params=_params(2),
        name="mid_proj",
    )(o_c, x, w_out.astype(BF16), kv_norm.reshape(1, d), b_norm.reshape(1, d),
      kv_w.T.astype(BF16), b_w_in.T.astype(BF16),
      kv_k_norm.reshape(2, HEAD_DIM, 1), b_q_norm.reshape(2, HEAD_DIM, 1))


def _diff_attn_kernel(q_ref, k_ref, v_ref, brow_ref, lam_ref, gs_ref, g_ref, o_ref,
                      ktok_ref, vaug_ref, bias_ref, m_ref, alpha_ref, mx_ref, acc_ref, s_ref, p_ref,
                      *, hd, n_chunks):
    t = q_ref.shape[-1]
    dv = DIFF_V_DIM
    streams = [(h, mp) for h in range(hd) for mp in range(2)]

    def prepare(j):
        for h in range(hd):
            ktok_ref[h, j] = k_ref[0, j, h * dv:(h + 1) * dv, :].astype(F32).T.astype(BF16)
            vaug_ref[h, j] = _with_ones_rows(v_ref[0, j, h * dv:(h + 1) * dv, :])

    for h, mp in streams:
        for off in range(2):
            rows = jnp.broadcast_to(brow_ref[h, mp, off], (t, 2 * t))
            bias_ref[h, mp, off] = pltpu.roll(rows, 0, 1, stride=1, stride_axis=0)[:, 0:t]

    def make_produce(qi):
        qa = []
        for h in range(hd):
            q = q_ref[0, qi, h * dv:(h + 1) * dv, :]
            row = lax.broadcasted_iota(jnp.int32, q.shape, 0)
            zero = jnp.zeros_like(q)
            qa += [jnp.where(row < HEAD_DIM, q, zero), jnp.where(row >= HEAD_DIM, q, zero)]

        def produce(j, kind):
            for i, (h, mp) in enumerate(streams):
                def score_block(r0, rn, c0, cn, i=i, h=h, mp=mp):
                    s = jnp.dot(ktok_ref[h, j, r0:r0 + rn, :], qa[i][:, c0:c0 + cn],
                                preferred_element_type=F32)
                    if kind == "near":
                        s = s + bias_ref[h, mp, 1, r0:r0 + rn, c0:c0 + cn]
                    elif kind == "diag":
                        kchunk = (r0 + lax.broadcasted_iota(jnp.int32, (rn, cn), 0)) // CHUNK
                        qchunk = (c0 + lax.broadcasted_iota(jnp.int32, (rn, cn), 1)) // CHUNK
                        s = jnp.where(kchunk <= qchunk,
                                      s + bias_ref[h, mp, 0, r0:r0 + rn, c0:c0 + cn], NEG_INF)
                    return s
                _store_scores(score_block, s_ref, mx_ref, i, kind == "diag")
        return produce

    def consume():
        for i in range(len(streams)):
            _softmax_step(s_ref, mx_ref, m_ref, alpha_ref, p_ref, i)

    def pv(j, diagonal=False):
        for i, (h, mp) in enumerate(streams):
            _pv_step(vaug_ref[h, j], p_ref, alpha_ref, acc_ref, i, diagonal)

    def close_tile(qi):
        lam_p = lam_ref[...]
        e1 = jnp.exp(jnp.sum(lam_p[0:1] * lam_p[1:2], axis=1, keepdims=True))
        e2 = jnp.exp(jnp.sum(lam_p[2:3] * lam_p[3:4], axis=1, keepdims=True))
        lam = e1 - e2 + LAMBDA_INIT
        for h in range(hd):
            a1, a2 = acc_ref[2 * h], acc_ref[2 * h + 1]
            o = a1[0:dv] * (1.0 / a1[dv:dv + 1]) - lam * (a2[0:dv] * (1.0 / a2[dv:dv + 1]))
            ms = jnp.mean(o * o, axis=0, keepdims=True)
            y = o * lax.rsqrt(ms + RMS_EPS) * gs_ref[...] * (1.0 - LAMBDA_INIT)
            g = g_ref[0, qi, h * dv:(h + 1) * dv, :].astype(F32)
            o_ref[0, qi, h * dv:(h + 1) * dv, :] = (y * _silu(g)).astype(BF16)

    _run_row(n_chunks, ("near", "diag"), prepare, make_produce, consume, pv, close_tile,
             functools.partial(_reset_softmax_state, m_ref, acc_ref))


def _t5_bucket(rel):
    half = NUM_BUCKETS // 2
    max_exact = half // 2
    ret = jnp.where(rel > 0, half, 0)
    n = jnp.abs(rel)
    n_f = jnp.maximum(n, 1).astype(jnp.float32)
    large = max_exact + (jnp.log(n_f / max_exact) / math.log(MAX_DISTANCE / max_exact)
                         * (half - max_exact)).astype(jnp.int32)
    large = jnp.minimum(large, half - 1)
    return ret + jnp.where(n < max_exact, n, large)


def _bias_rows(rel_bias, t):
    dw = jnp.arange(2 * t)
    d = jnp.where(dw < t, dw, dw - 2 * t)
    rel = jnp.stack([-d, -d - t])
    rows = rel_bias[_t5_bucket(rel)].astype(F32)
    far = rel_bias[_t5_bucket(jnp.int32(-MAX_DISTANCE))].astype(F32)
    return ((rows - far) * LOG2E).transpose(3, 2, 0, 1)[:, :, :, None, :]


def _diff_attn(q_c, k_c, v_c, g_c, rel_bias, lam_params, g_sub):
    b, n_chunks, width, t = q_c.shape
    n_heads = width // DIFF_V_DIM
    hd = DIFF_HEADS_PER_STEP
    assert (t // 2) % CHUNK == 0
    rows = hd * DIFF_V_DIM
    n_streams = 2 * hd
    full = pl.BlockSpec((1, n_chunks, rows, t), lambda bi, hg: (bi, 0, hg, 0))
    return pl.pallas_call(
        functools.partial(_diff_attn_kernel, hd=hd, n_chunks=n_chunks),
        grid=(b, n_heads // hd),
        in_specs=[
            full, full, full,
            pl.BlockSpec((hd, 2, 2, 1, 2 * t), lambda bi, hg: (hg, 0, 0, 0, 0)),
            pl.BlockSpec((4, HEAD_DIM), lambda *_: (0, 0)),
            pl.BlockSpec((DIFF_V_DIM, 1), lambda *_: (0, 0)),
            full,
        ],
        out_specs=full,
        out_shape=jax.ShapeDtypeStruct((b, n_chunks, width, t), BF16),
        scratch_shapes=[
            pltpu.VMEM((hd, n_chunks, t, DIFF_V_DIM), BF16),
            pltpu.VMEM((hd, n_chunks, DIFF_V_DIM + ONES_ROWS, t), BF16),
            pltpu.VMEM((hd, 2, 2, t, t), F32),
            pltpu.VMEM((n_streams, 1, t), F32),
            pltpu.VMEM((n_streams, 1, t), F32),
            pltpu.VMEM((n_streams, 1, t), F32),
            pltpu.VMEM((n_streams, DIFF_V_DIM + ONES_ROWS, t), F32),
            pltpu.VMEM((n_streams, t, t), F32),
            pltpu.VMEM((n_streams, t, t), BF16),
        ],
        compiler_params=_params(2),
        name="diff_attn",
    )(q_c, k_c, v_c, _bias_rows(rel_bias, t), lam_params, g_sub.reshape(DIFF_V_DIM, 1), g_c)


def _out_kernel(o_ref, h_ref, w_ref, y_ref):
    y_ref[0] = h_ref[0] + lax.dot_general(o_ref[0, 0], w_ref[...], _TN, preferred_element_type=F32)


def _out_proj(o_c, h1, w_out):
    b, s, d = h1.shape
    tm = ROW_TILE
    tok = pl.BlockSpec((1, tm, d), lambda bi, i: (bi, i, 0))
    return pl.pallas_call(
        _out_kernel,
        grid=(b, s // tm),
        in_specs=[
            pl.BlockSpec((1, 1, o_c.shape[2], tm), lambda bi, i: (bi, i, 0, 0)),
            tok,
            pl.BlockSpec(w_out.shape, lambda *_: (0, 0)),
        ],
        out_specs=tok,
        out_shape=jax.ShapeDtypeStruct((b, s, d), F32),
        compiler_params=_params(2),
        name="out_proj",
    )(o_c, h1, w_out.astype(BF16))


def kernel(x, a_norm, a_w_in, a_b_f, a_q_norm, a_k_norm, a_w_out, kv_norm, kv_w, kv_k_norm,
           rel_bias, b_norm, b_w_in, b_q_norm, b_lam_q1, b_lam_k1, b_lam_q2, b_lam_k2,
           b_sub_norm, b_w_out):
    assert a_norm.shape[0] == 1 and b_norm.shape[0] == 1
    assert x.shape[1] % ATTN_TILE == 0 and ATTN_TILE == ROW_TILE
    q_c, k_c, v_c, g_c, c_c = _fox_proj(x, a_norm[0], a_w_in[0], a_b_f[0], a_q_norm[0], a_k_norm[0])
    o_c = _fox_attn(q_c, k_c, v_c, c_c, g_c)
    h1, k2_c, v2_c, q2_c, g2_c = _mid(o_c, x, a_w_out[0], kv_norm, kv_w, kv_k_norm,
                                      b_norm[0], b_w_in[0], b_q_norm[0])
    lam_params = jnp.stack([b_lam_q1[0], b_lam_k1[0], b_lam_q2[0], b_lam_k2[0]])
    o2_c = _diff_attn(q2_c, k2_c, v2_c, g2_c, rel_bias, lam_params, b_sub_norm[0])
    return _out_proj(o2_c, h1, b_w_out[0])
```

```python
import functools
import math

import jax
import jax.numpy as jnp
from jax import lax
from jax.experimental import pallas as pl
from jax.experimental.pallas import tpu as pltpu

HEAD_DIM = 64
DIFF_V_DIM = 2 * HEAD_DIM
CHUNK = 64
NUM_BUCKETS = 32
MAX_DISTANCE = 128
RMS_EPS = 1e-6
NEG_INF = -1e30
LOG2E = math.log2(math.e)
QK_SCALE = HEAD_DIM ** -0.5 * LOG2E
LAMBDA_INIT = 0.8 - 0.6 * math.exp(-0.3 * 1)

V7X_LANES = 128
V7X_SUBLANES = 8
V7X_VMEM_LIMIT_BYTES = 56 * 1024 * 1024

ROW_TILE = 512
ATTN_TILE = 512
FOX_HEADS_PER_STEP = 2
DIFF_HEADS_PER_STEP = 1
ONES_ROWS = 16
AUG_DEPTH = 2 * HEAD_DIM
FAR_STEPS_PER_BLOCK = 2

F32 = jnp.float32
BF16 = jnp.bfloat16
_NT = (((1,), (1,)), ((), ()))
_TN = (((0,), (0,)), ((), ()))


def _params(n_axes):
    return pltpu.CompilerParams(
        dimension_semantics=("arbitrary",) * n_axes,
        vmem_limit_bytes=V7X_VMEM_LIMIT_BYTES)


def _head_norm(p_t, g_col, n_heads, scale):
    t = p_t.shape[-1]
    p3 = p_t.reshape(n_heads, HEAD_DIM, t)
    ms = jnp.mean(p3 * p3, axis=1, keepdims=True)
    y = p3 * lax.rsqrt(ms + RMS_EPS) * (g_col * scale)[None]
    return y.reshape(n_heads * HEAD_DIM, t)


def _split3(c):
    hi = c.astype(BF16).astype(F32)
    r = c - hi
    mid = r.astype(BF16).astype(F32)
    lo = (r - mid).astype(BF16).astype(F32)
    return hi, mid, lo


def _silu(g):
    return g / (1.0 + jnp.exp(-g))


def _fox_proj_kernel(x_ref, gn_ref, wq_ref, wk_ref, wv_ref, wg_ref, wf_ref, bf_ref,
                     gq_ref, gk_ref, q_ref, k_ref, v_ref, g_ref, c_ref, carry_ref,
                     *, n_heads):
    @pl.when(pl.program_id(1) == 0)
    def _():
        carry_ref[...] = jnp.zeros_like(carry_ref)

    x = x_ref[0]
    tm = x.shape[0]
    ms = jnp.mean(x * x, axis=-1, keepdims=True)
    u = (x * lax.rsqrt(ms + RMS_EPS) * gn_ref[...]).astype(BF16)

    def proj(w_ref):
        return lax.dot_general(w_ref[...], u, _NT, preferred_element_type=F32)

    z = proj(wf_ref) + bf_ref[...]
    log_f = jnp.minimum(z, 0.0) - jnp.log1p(jnp.exp(-jnp.abs(z)))
    lane = lax.broadcasted_iota(jnp.int32, log_f.shape, 1)
    cs = log_f
    shift = 1
    while shift < tm:
        cs = cs + jnp.where(lane >= shift, pltpu.roll(cs, shift, axis=1), 0.0)
        shift *= 2
    c = cs + carry_ref[:, 0:1]
    c_ref[0, 0] = c * LOG2E
    carry_ref[...] = jnp.broadcast_to(c[:, tm - 1:tm], carry_ref.shape)

    q_ref[0, 0] = _head_norm(proj(wq_ref), gq_ref[...], n_heads, QK_SCALE).astype(BF16)
    k_ref[0, 0] = _head_norm(proj(wk_ref), gk_ref[...], n_heads, 1.0).astype(BF16)
    v_ref[0, 0] = proj(wv_ref).astype(BF16)
    g_ref[0, 0] = proj(wg_ref).astype(BF16)


def _fox_proj(x, a_norm, w_in, b_f, g_q, g_k):
    b, s, d = x.shape
    n_heads = b_f.shape[0]
    width = n_heads * HEAD_DIM
    tm = ROW_TILE
    w_t = w_in.T.astype(BF16)
    wq, wk, wv, wg, wf = (w_t[0:width], w_t[width:2 * width], w_t[2 * width:3 * width],
                          w_t[3 * width:4 * width], w_t[4 * width:])
    const = lambda *_: (0, 0)
    w_spec = pl.BlockSpec((width, d), const)
    col = pl.BlockSpec((HEAD_DIM, 1), const)
    chunked = pl.BlockSpec((1, 1, width, tm), lambda bi, i: (bi, i, 0, 0))
    return pl.pallas_call(
        functools.partial(_fox_proj_kernel, n_heads=n_heads),
        grid=(b, s // tm),
        in_specs=[
            pl.BlockSpec((1, tm, d), lambda bi, i: (bi, i, 0)),
            pl.BlockSpec((1, d), const),
            w_spec, w_spec, w_spec, w_spec,
            pl.BlockSpec((n_heads, d), const),
            pl.BlockSpec((n_heads, 1), const),
            col, col,
        ],
        out_specs=[chunked, chunked, chunked, chunked,
                   pl.BlockSpec((1, 1, n_heads, tm), lambda bi, i: (bi, i, 0, 0))],
        out_shape=[
            jax.ShapeDtypeStruct((b, s // tm, width, tm), BF16),
            jax.ShapeDtypeStruct((b, s // tm, width, tm), BF16),
            jax.ShapeDtypeStruct((b, s // tm, width, tm), BF16),
            jax.ShapeDtypeStruct((b, s // tm, width, tm), BF16),
            jax.ShapeDtypeStruct((b, s // tm, n_heads, tm), F32),
        ],
        scratch_shapes=[pltpu.VMEM((n_heads, V7X_LANES), F32)],
        compiler_params=_params(2),
        name="fox_proj",
    )(x, a_norm.reshape(1, d), wq, wk, wv, wg, wf, b_f.reshape(n_heads, 1),
      g_q.reshape(HEAD_DIM, 1), g_k.reshape(HEAD_DIM, 1))


def _store_scores(score_block, s_ref, mx_ref, idx, diagonal):
    t = s_ref.shape[1]
    if not diagonal:
        s = score_block(0, t, 0, t)
        s_ref[idx] = s
        mx_ref[idx] = jnp.max(s, axis=0, keepdims=True)
        return
    half = t // 2
    top = score_block(0, half, 0, t)
    bottom = score_block(half, half, half, half)
    s_ref[idx, 0:half, :] = top
    s_ref[idx, half:t, 0:half] = jnp.full((half, half), NEG_INF, F32)
    s_ref[idx, half:t, half:t] = bottom
    top_max = jnp.max(top, axis=0, keepdims=True)
    bottom_max = jnp.max(bottom, axis=0, keepdims=True)
    mx_ref[idx] = jnp.concatenate(
        [top_max[:, 0:half], jnp.maximum(top_max[:, half:t], bottom_max)], axis=1)


def _softmax_step(s_ref, mx_ref, m_ref, alpha_ref, p_ref, idx):
    m_prev = m_ref[idx]
    m_new = jnp.maximum(m_prev, mx_ref[idx])
    p_ref[idx] = jnp.exp2(s_ref[idx] - m_new).astype(BF16)
    alpha_ref[idx] = jnp.exp2(m_prev - m_new)
    m_ref[idx] = m_new


def _pv_step(v, p_ref, alpha_ref, acc_ref, idx, diagonal=False):
    if not diagonal:
        pv = jnp.dot(v, p_ref[idx], preferred_element_type=F32)
    else:
        t = p_ref.shape[1]
        half = t // 2
        top = jnp.dot(v[:, 0:half], p_ref[idx, 0:half, :], preferred_element_type=F32)
        bottom = jnp.dot(v[:, half:t], p_ref[idx, half:t, half:t], preferred_element_type=F32)
        pv = jnp.concatenate([top[:, 0:half], top[:, half:t] + bottom], axis=1)
    acc_ref[idx] = alpha_ref[idx] * acc_ref[idx] + pv


def _with_ones_rows(v):
    return jnp.concatenate([v, jnp.ones((ONES_ROWS, v.shape[1]), v.dtype)], axis=0)


def _reset_softmax_state(m_ref, acc_ref):
    m_ref[...] = jnp.full(m_ref.shape, NEG_INF, F32)
    acc_ref[...] = jnp.zeros_like(acc_ref)


def _run_row(n_q, tail, prepare, make_produce, consume, pv, close_tile, open_tile):
    n_tail = len(tail)
    assert n_q > n_tail + 1
    prepare(0)

    def kinds_of(q_tile):
        return ((("far",) * (q_tile + 1)) + tail)[-(q_tile + 1):]

    def first_kind(q_tile):
        return kinds_of(q_tile)[0] if q_tile <= n_tail else "far"

    def query_tile(qi, kinds):
        produce = make_produce(qi)
        successor = min(qi + 1, n_q - 1) if isinstance(qi, int) else jnp.minimum(qi + 1, n_q - 1)
        produce_next = functools.partial(make_produce(successor), 0)

        def step(j, produce_following):
            pv(j - 1)
            consume()
            produce_following()

        def first_step(produce_following):
            if not (isinstance(qi, int) and qi == 0):
                pv(qi - 1, diagonal=True)
                close_tile(qi - 1)
            open_tile()
            consume()
            produce_following()
            prepare(successor)

        def rest(first_tile, produced_kinds, following_first_kind):
            j = first_tile
            for kind in produced_kinds:
                step(j, functools.partial(produce, j + 1, kind))
                j = j + 1
            step(j, functools.partial(produce_next, following_first_kind))

        if kinds is not None:
            if qi == 0:
                produce(0, kinds[0])
                first_step(functools.partial(produce_next, first_kind(1)))
            else:
                first_step(functools.partial(produce, 1, kinds[1]))
                rest(1, kinds[2:], first_kind(qi + 1))
            return

        first_step(functools.partial(produce, 1, "far"))

        n_far = qi - n_tail - 1

        def far_steps(first, count):
            for j in range(count):
                step(first + j, functools.partial(produce, first + j + 1, "far"))

        def far_block(i, carry):
            far_steps(1 + FAR_STEPS_PER_BLOCK * i, FAR_STEPS_PER_BLOCK)
            return carry
        lax.fori_loop(0, n_far // FAR_STEPS_PER_BLOCK, far_block, 0)

        remainder = n_far % FAR_STEPS_PER_BLOCK
        for r in range(1, FAR_STEPS_PER_BLOCK):
            @pl.when(remainder == r)
            def _(r=r):
                far_steps(1 + n_far - r, r)
        rest(qi - n_tail, tail, "far")

    for qi in range(n_tail + 1):
        query_tile(qi, kinds_of(qi))

    def general_tile(qi, carry):
        query_tile(qi, None)
        return carry
    lax.fori_loop(n_tail + 1, n_q, general_tile, 0)

    pv(n_q - 1, diagonal=True)
    close_tile(n_q - 1)


def _aug_rows(rows, t):
    shape = (V7X_SUBLANES, t)
    row = lax.broadcasted_iota(jnp.int32, shape, 0)
    out = jnp.zeros(shape, F32)
    for r, val in enumerate(rows):
        out = jnp.where(row == r, val, out)
    return out


def _fox_attn_kernel(q_ref, k_ref, v_ref, c_ref, g_ref, o_ref,
                     kaug_ref, vaug_ref, m_ref, alpha_ref, mx_ref, acc_ref, s_ref, p_ref,
                     *, hp, n_chunks):
    t = q_ref.shape[-1]
    pad = jnp.zeros((AUG_DEPTH - HEAD_DIM - V7X_SUBLANES, t), F32)

    def prepare(j):
        for h in range(hp):
            kt = k_ref[0, j, h * HEAD_DIM:(h + 1) * HEAD_DIM, :].astype(F32)
            hi, mid, lo = _split3(c_ref[0, j, 0, h:h + 1, :])
            aug = _aug_rows([-hi, -mid, -lo, 1.0, 1.0, 1.0], t)
            kaug_ref[h, j] = jnp.concatenate([kt, aug, pad], axis=0).T.astype(BF16)
            vaug_ref[h, j] = _with_ones_rows(v_ref[0, j, h * HEAD_DIM:(h + 1) * HEAD_DIM, :])

    def make_produce(qi):
        qa = []
        for h in range(hp):
            qf = q_ref[0, qi, h * HEAD_DIM:(h + 1) * HEAD_DIM, :].astype(F32)
            hi, mid, lo = _split3(c_ref[0, qi, 0, h:h + 1, :])
            aug = _aug_rows([1.0, 1.0, 1.0, hi, mid, lo], t)
            qa.append(jnp.concatenate([qf, aug, pad], axis=0).astype(BF16))

        def produce(j, kind):
            for h in range(hp):
                def score_block(r0, rn, c0, cn, h=h):
                    s = jnp.dot(kaug_ref[h, j, r0:r0 + rn, :], qa[h][:, c0:c0 + cn],
                                preferred_element_type=F32)
                    if kind == "diag":
                        kpos = r0 + lax.broadcasted_iota(jnp.int32, (rn, cn), 0)
                        qpos = c0 + lax.broadcasted_iota(jnp.int32, (rn, cn), 1)
                        s = jnp.where(kpos <= qpos, s, NEG_INF)
                    return s
                _store_scores(score_block, s_ref, mx_ref, h, kind == "diag")
        return produce

    def consume():
        for h in range(hp):
            _softmax_step(s_ref, mx_ref, m_ref, alpha_ref, p_ref, h)

    def pv(j, diagonal=False):
        for h in range(hp):
            _pv_step(vaug_ref[h, j], p_ref, alpha_ref, acc_ref, h, diagonal)

    def close_tile(qi):
        for h in range(hp):
            o = acc_ref[h, 0:HEAD_DIM] * (1.0 / acc_ref[h, HEAD_DIM:HEAD_DIM + 1])
            g = g_ref[0, qi, h * HEAD_DIM:(h + 1) * HEAD_DIM, :].astype(F32)
            o_ref[0, qi, h * HEAD_DIM:(h + 1) * HEAD_DIM, :] = (o * _silu(g)).astype(BF16)

    _run_row(n_chunks, ("diag",), prepare, make_produce, consume, pv, close_tile,
             functools.partial(_reset_softmax_state, m_ref, acc_ref))


def _fox_attn(q_c, k_c, v_c, c_c, g_c):
    b, n_chunks, width, t = q_c.shape
    hp = FOX_HEADS_PER_STEP
    n_heads = width // HEAD_DIM
    rows = hp * HEAD_DIM
    c_c = c_c.reshape(b, n_chunks, n_heads // hp, hp, t)
    full = pl.BlockSpec((1, n_chunks, rows, t), lambda bi, hg: (bi, 0, hg, 0))
    return pl.pallas_call(
        functools.partial(_fox_attn_kernel, hp=hp, n_chunks=n_chunks),
        grid=(b, n_heads // hp),
        in_specs=[
            full, full, full,
            pl.BlockSpec((1, n_chunks, 1, hp, t), lambda bi, hg: (bi, 0, hg, 0, 0)),
            full,
        ],
        out_specs=full,
        out_shape=jax.ShapeDtypeStruct((b, n_chunks, width, t), BF16),
        scratch_shapes=[
            pltpu.VMEM((hp, n_chunks, t, AUG_DEPTH), BF16),
            pltpu.VMEM((hp, n_chunks, HEAD_DIM + ONES_ROWS, t), BF16),
            pltpu.VMEM((hp, 1, t), F32),
            pltpu.VMEM((hp, 1, t), F32),
            pltpu.VMEM((hp, 1, t), F32),
            pltpu.VMEM((hp, HEAD_DIM + ONES_ROWS, t), F32),
            pltpu.VMEM((hp, t, t), F32),
            pltpu.VMEM((hp, t, t), BF16),
        ],
        compiler_params=_params(2),
        name="fox_attn",
    )(q_c, k_c, v_c, c_c, g_c)


def _mid_kernel(o_ref, x_ref, wo_ref, gkv_ref, gb_ref, wkv_ref, wb_ref, gk_ref, gq_ref,
                h_ref, k_ref, v_ref, q_ref, g_ref, *, n_heads):
    half = n_heads * HEAD_DIM
    h1 = x_ref[0] + lax.dot_general(o_ref[0, 0], wo_ref[...], _TN, preferred_element_type=F32)
    h_ref[0] = h1
    ms = jnp.mean(h1 * h1, axis=-1, keepdims=True)
    hn = h1 * lax.rsqrt(ms + RMS_EPS)
    u_kv = (hn * gkv_ref[...]).astype(BF16)
    u_b = (hn * gb_ref[...]).astype(BF16)

    def pack_heads(ref, a, b2):
        for h in range(n_heads):
            sl = slice(h * HEAD_DIM, (h + 1) * HEAD_DIM)
            ref[..., h * 2 * HEAD_DIM:h * 2 * HEAD_DIM + HEAD_DIM, :] = a[sl]
            ref[..., h * 2 * HEAD_DIM + HEAD_DIM:(h + 1) * 2 * HEAD_DIM, :] = b2[sl]

    kv = lax.dot_general(wkv_ref[...], u_kv, _NT, preferred_element_type=F32)
    k1 = _head_norm(kv[0:half], gk_ref[0], n_heads, 1.0).astype(BF16)
    k2 = _head_norm(kv[half:2 * half], gk_ref[1], n_heads, 1.0).astype(BF16)
    pack_heads(k_ref.at[0, 0], k1, k2)
    v_ref[0, 0] = kv[2 * half:].astype(BF16)

    qg = lax.dot_general(wb_ref[...], u_b, _NT, preferred_element_type=F32)
    q1 = _head_norm(qg[0:half], gq_ref[0], n_heads, QK_SCALE).astype(BF16)
    q2 = _head_norm(qg[half:2 * half], gq_ref[1], n_heads, QK_SCALE).astype(BF16)
    pack_heads(q_ref.at[0, 0], q1, q2)
    g_ref[0, 0] = qg[2 * half:].astype(BF16)


def _mid(o_c, x, w_out, kv_norm, kv_w, kv_k_norm, b_norm, b_w_in, b_q_norm):
    b, s, d = x.shape
    tm = ROW_TILE
    n_heads = kv_w.shape[1] // (4 * HEAD_DIM)
    width = 2 * n_heads * HEAD_DIM
    const2 = lambda *_: (0, 0)
    const3 = lambda *_: (0, 0, 0)
    tok = pl.BlockSpec((1, tm, d), lambda bi, i: (bi, i, 0))
    chunked = pl.BlockSpec((1, 1, width, tm), lambda bi, i: (bi, i, 0, 0))
    return pl.pallas_call(
        functools.partial(_mid_kernel, n_heads=n_heads),
        grid=(b, s // tm),
        in_specs=[
            pl.BlockSpec((1, 1, o_c.shape[2], tm), lambda bi, i: (bi, i, 0, 0)),
            tok,
            pl.BlockSpec(w_out.shape, const2),
            pl.BlockSpec((1, d), const2),
            pl.BlockSpec((1, d), const2),
            pl.BlockSpec((2 * width, d), const2),
            pl.BlockSpec((2 * width, d), const2),
            pl.BlockSpec((2, HEAD_DIM, 1), const3),
            pl.BlockSpec((2, HEAD_DIM, 1), const3),
        ],
        out_specs=[tok, chunked, chunked, chunked, chunked],
        out_shape=[
            jax.ShapeDtypeStruct((b, s, d), F32),
            jax.ShapeDtypeStruct((b, s // tm, width, tm), BF16),
            jax.ShapeDtypeStruct((b, s // tm, width, tm), BF16),
            jax.ShapeDtypeStruct((b, s // tm, width, tm), BF16),
            jax.ShapeDtypeStruct((b, s // tm, width, tm), BF16),
        ],
        compiler_params=_params(2),
        name="mid_proj",
    )(o_c, x, w_out.astype(BF16), kv_norm.reshape(1, d), b_norm.reshape(1, d),
      kv_w.T.astype(BF16), b_w_in.T.astype(BF16),
      kv_k_norm.reshape(2, HEAD_DIM, 1), b_q_norm.reshape(2, HEAD_DIM, 1))


def _diff_attn_kernel(q_ref, k_ref, v_ref, brow_ref, lam_ref, gs_ref, g_ref, o_ref,
                      ktok_ref, vaug_ref, bias_ref, m_ref, alpha_ref, mx_ref, acc_ref, s_ref, p_ref,
                      *, hd, n_chunks):
    t = q_ref.shape[-1]
    dv = DIFF_V_DIM
    streams = [(h, mp) for h in range(hd) for mp in range(2)]

    def prepare(j):
        for h in range(hd):
            ktok_ref[h, j] = k_ref[0, j, h * dv:(h + 1) * dv, :].astype(F32).T.astype(BF16)
            vaug_ref[h, j] = _with_ones_rows(v_ref[0, j, h * dv:(h + 1) * dv, :])

    for h, mp in streams:
        for off in range(2):
            rows = jnp.broadcast_to(brow_ref[h, mp, off], (t, 2 * t))
            bias_ref[h, mp, off] = pltpu.roll(rows, 0, 1, stride=1, stride_axis=0)[:, 0:t]

    def make_produce(qi):
        qa = []
        for h in range(hd):
            q = q_ref[0, qi, h * dv:(h + 1) * dv, :]
            row = lax.broadcasted_iota(jnp.int32, q.shape, 0)
            zero = jnp.zeros_like(q)
            qa += [jnp.where(row < HEAD_DIM, q, zero), jnp.where(row >= HEAD_DIM, q, zero)]

        def produce(j, kind):
            for i, (h, mp) in enumerate(streams):
                def score_block(r0, rn, c0, cn, i=i, h=h, mp=mp):
                    s = jnp.dot(ktok_ref[h, j, r0:r0 + rn, :], qa[i][:, c0:c0 + cn],
                                preferred_element_type=F32)
                    if kind == "near":
                        s = s + bias_ref[h, mp, 1, r0:r0 + rn, c0:c0 + cn]
                    elif kind == "diag":
                        kchunk = (r0 + lax.broadcasted_iota(jnp.int32, (rn, cn), 0)) // CHUNK
                        qchunk = (c0 + lax.broadcasted_iota(jnp.int32, (rn, cn), 1)) // CHUNK
                        s = jnp.where(kchunk <= qchunk,
                                      s + bias_ref[h, mp, 0, r0:r0 + rn, c0:c0 + cn], NEG_INF)
                    return s
                _store_scores(score_block, s_ref, mx_ref, i, kind == "diag")
        return produce

    def consume():
        for i in range(len(streams)):
            _softmax_step(s_ref, mx_ref, m_ref, alpha_ref, p_ref, i)

    def pv(j, diagonal=False):
        for i, (h, mp) in enumerate(streams):
            _pv_step(vaug_ref[h, j], p_ref, alpha_ref, acc_ref, i, diagonal)

    def close_tile(qi):
        lam_p = lam_ref[...]
        e1 = jnp.exp(jnp.sum(lam_p[0:1] * lam_p[1:2], axis=1, keepdims=True))
        e2 = jnp.exp(jnp.sum(lam_p[2:3] * lam_p[3:4], axis=1, keepdims=True))
        lam = e1 - e2 + LAMBDA_INIT
        for h in range(hd):
            a1, a2 = acc_ref[2 * h], acc_ref[2 * h + 1]
            o = a1[0:dv] * (1.0 / a1[dv:dv + 1]) - lam * (a2[0:dv] * (1.0 / a2[dv:dv + 1]))
            ms = jnp.mean(o * o, axis=0, keepdims=True)
            y = o * lax.rsqrt(ms + RMS_EPS) * gs_ref[...] * (1.0 - LAMBDA_INIT)
            g = g_ref[0, qi, h * dv:(h + 1) * dv, :].astype(F32)
            o_ref[0, qi, h * dv:(h + 1) * dv, :] = (y * _silu(g)).astype(BF16)

    _run_row(n_chunks, ("near", "diag"), prepare, make_produce, consume, pv, close_tile,
             functools.partial(_reset_softmax_state, m_ref, acc_ref))


def _t5_bucket(rel):
    half = NUM_BUCKETS // 2
    max_exact = half // 2
    ret = jnp.where(rel > 0, half, 0)
    n = jnp.abs(rel)
    n_f = jnp.maximum(n, 1).astype(jnp.float32)
    large = max_exact + (jnp.log(n_f / max_exact) / math.log(MAX_DISTANCE / max_exact)
                         * (half - max_exact)).astype(jnp.int32)
    large = jnp.minimum(large, half - 1)
    return ret + jnp.where(n < max_exact, n, large)


def _bias_rows(rel_bias, t):
    dw = jnp.arange(2 * t)
    d = jnp.where(dw < t, dw, dw - 2 * t)
    rel = jnp.stack([-d, -d - t])
    rows = rel_bias[_t5_bucket(rel)].astype(F32)
    far = rel_bias[_t5_bucket(jnp.int32(-MAX_DISTANCE))].astype(F32)
    return ((rows - far) * LOG2E).transpose(3, 2, 0, 1)[:, :, :, None, :]


def _diff_attn(q_c, k_c, v_c, g_c, rel_bias, lam_params, g_sub):
    b, n_chunks, width, t = q_c.shape
    n_heads = width // DIFF_V_DIM
    hd = DIFF_HEADS_PER_STEP
    assert t >= MAX_DISTANCE
    assert (t // 2) % CHUNK == 0
    rows = hd * DIFF_V_DIM
    n_streams = 2 * hd
    full = pl.BlockSpec((1, n_chunks, rows, t), lambda bi, hg: (bi, 0, hg, 0))
    return pl.pallas_call(
        functools.partial(_diff_attn_kernel, hd=hd, n_chunks=n_chunks),
        grid=(b, n_heads // hd),
        in_specs=[
            full, full, full,
            pl.BlockSpec((hd, 2, 2, 1, 2 * t), lambda bi, hg: (hg, 0, 0, 0, 0)),
            pl.BlockSpec((4, HEAD_DIM), lambda *_: (0, 0)),
            pl.BlockSpec((DIFF_V_DIM, 1), lambda *_: (0, 0)),
            full,
        ],
        out_specs=full,
        out_shape=jax.ShapeDtypeStruct((b, n_chunks, width, t), BF16),
        scratch_shapes=[
            pltpu.VMEM((hd, n_chunks, t, DIFF_V_DIM), BF16),
            pltpu.VMEM((hd, n_chunks, DIFF_V_DIM + ONES_ROWS, t), BF16),
            pltpu.VMEM((hd, 2, 2, t, t), F32),
            pltpu.VMEM((n_streams, 1, t), F32),
            pltpu.VMEM((n_streams, 1, t), F32),
            pltpu.VMEM((n_streams, 1, t), F32),
            pltpu.VMEM((n_streams, DIFF_V_DIM + ONES_ROWS, t), F32),
            pltpu.VMEM((n_streams, t, t), F32),
            pltpu.VMEM((n_streams, t, t), BF16),
        ],
        compiler_params=_params(2),
        name="diff_attn",
    )(q_c, k_c, v_c, _bias_rows(rel_bias, t), lam_params, g_sub.reshape(DIFF_V_DIM, 1), g_c)


def _out_kernel(o_ref, h_ref, w_ref, y_ref):
    y_ref[0] = h_ref[0] + lax.dot_general(o_ref[0, 0], w_ref[...], _TN, preferred_element_type=F32)


def _out_proj(o_c, h1, w_out):
    b, s, d = h1.shape
    tm = ROW_TILE
    tok = pl.BlockSpec((1, tm, d), lambda bi, i: (bi, i, 0))
    return pl.pallas_call(
        _out_kernel,
        grid=(b, s // tm),
        in_specs=[
            pl.BlockSpec((1, 1, o_c.shape[2], tm), lambda bi, i: (bi, i, 0, 0)),
            tok,
            pl.BlockSpec(w_out.shape, lambda *_: (0, 0)),
        ],
        out_specs=tok,
        out_shape=jax.ShapeDtypeStruct((b, s, d), F32),
        compiler_params=_params(2),
        name="out_proj",
    )(o_c, h1, w_out.astype(BF16))


def kernel(x, a_norm, a_w_in, a_b_f, a_q_norm, a_k_norm, a_w_out, kv_norm, kv_w, kv_k_norm,
           rel_bias, b_norm, b_w_in, b_q_norm, b_lam_q1, b_lam_k1, b_lam_q2, b_lam_k2,
           b_sub_norm, b_w_out):
    assert a_norm.shape[0] == 1 and b_norm.shape[0] == 1
    assert x.shape[1] % ATTN_TILE == 0 and ATTN_TILE == ROW_TILE
    q_c, k_c, v_c, g_c, c_c = _fox_proj(x, a_norm[0], a_w_in[0], a_b_f[0], a_q_norm[0], a_k_norm[0])
    o_c = _fox_attn(q_c, k_c, v_c, c_c, g_c)
    h1, k2_c, v2_c, q2_c, g2_c = _mid(o_c, x, a_w_out[0], kv_norm, kv_w, kv_k_norm,
                                      b_norm[0], b_w_in[0], b_q_norm[0])
    lam_params = jnp.stack([b_lam_q1[0], b_lam_k1[0], b_lam_q2[0], b_lam_k2[0]])
    o2_c = _diff_attn(q2_c, k2_c, v2_c, g2_c, rel_bias, lam_params, b_sub_norm[0])
    return _out_proj(o2_c, h1, b_w_out[0])
```

```python
import functools
import math

import jax
import jax.numpy as jnp
from jax import lax
from jax.experimental import pallas as pl
from jax.experimental.pallas import tpu as pltpu

HEAD_DIM = 64
DIFF_V_DIM = 2 * HEAD_DIM
CHUNK = 64
NUM_BUCKETS = 32
MAX_DISTANCE = 128
RMS_EPS = 1e-6
NEG_INF = -1e30
LOG2E = math.log2(math.e)
QK_SCALE = HEAD_DIM ** -0.5 * LOG2E
LAMBDA_INIT = 0.8 - 0.6 * math.exp(-0.3 * 1)

V7X_LANES = 128
V7X_SUBLANES = 8
V7X_VMEM_LIMIT_BYTES = 56 * 1024 * 1024

ROW_TILE = 512
ATTN_TILE = 512
FOX_HEADS_PER_STEP = 2
DIFF_HEADS_PER_STEP = 1
ONES_ROWS = 16
AUG_DEPTH = 2 * HEAD_DIM
FAR_STEPS_PER_BLOCK = 2

F32 = jnp.float32
BF16 = jnp.bfloat16
_NT = (((1,), (1,)), ((), ()))
_TN = (((0,), (0,)), ((), ()))


def _params(n_axes):
    return pltpu.CompilerParams(
        dimension_semantics=("arbitrary",) * n_axes,
        vmem_limit_bytes=V7X_VMEM_LIMIT_BYTES)


def _head_norm(p_t, g_col, n_heads, scale):
    t = p_t.shape[-1]
    p3 = p_t.reshape(n_heads, HEAD_DIM, t)
    ms = jnp.mean(p3 * p3, axis=1, keepdims=True)
    y = p3 * lax.rsqrt(ms + RMS_EPS) * (g_col * scale)[None]
    return y.reshape(n_heads * HEAD_DIM, t)


def _split3(c):
    hi = c.astype(BF16).astype(F32)
    r = c - hi
    mid = r.astype(BF16).astype(F32)
    lo = (r - mid).astype(BF16).astype(F32)
    return hi, mid, lo


def _silu(g):
    return g / (1.0 + jnp.exp(-g))


def _fox_proj_kernel(x_ref, gn_ref, wq_ref, wk_ref, wv_ref, wg_ref, wf_ref, bf_ref,
                     gq_ref, gk_ref, q_ref, k_ref, v_ref, g_ref, c_ref, carry_ref,
                     *, n_heads):
    @pl.when(pl.program_id(1) == 0)
    def _():
        carry_ref[...] = jnp.zeros_like(carry_ref)

    x = x_ref[0]
    tm = x.shape[0]
    ms = jnp.mean(x * x, axis=-1, keepdims=True)
    u = (x * lax.rsqrt(ms + RMS_EPS) * gn_ref[...]).astype(BF16)

    def proj(w_ref):
        return lax.dot_general(w_ref[...], u, _NT, preferred_element_type=F32)

    z = proj(wf_ref) + bf_ref[...]
    log_f = jnp.minimum(z, 0.0) - jnp.log1p(jnp.exp(-jnp.abs(z)))
    lane = lax.broadcasted_iota(jnp.int32, log_f.shape, 1)
    cs = log_f
    shift = 1
    while shift < tm:
        cs = cs + jnp.where(lane >= shift, pltpu.roll(cs, shift, axis=1), 0.0)
        shift *= 2
    c = cs + carry_ref[:, 0:1]
    c_ref[0, 0] = c * LOG2E
    carry_ref[...] = jnp.broadcast_to(c[:, tm - 1:tm], carry_ref.shape)

    q_ref[0, 0] = _head_norm(proj(wq_ref), gq_ref[...], n_heads, QK_SCALE).astype(BF16)
    k_ref[0, 0] = _head_norm(proj(wk_ref), gk_ref[...], n_heads, 1.0).astype(BF16)
    v_ref[0, 0] = proj(wv_ref).astype(BF16)
    g_ref[0, 0] = proj(wg_ref).astype(BF16)


def _fox_proj(x, a_norm, w_in, b_f, g_q, g_k):
    b, s, d = x.shape
    n_heads = b_f.shape[0]
    width = n_heads * HEAD_DIM
    tm = ROW_TILE
    w_t = w_in.T.astype(BF16)
    wq, wk, wv, wg, wf = (w_t[0:width], w_t[width:2 * width], w_t[2 * width:3 * width],
                          w_t[3 * width:4 * width], w_t[4 * width:])
    const = lambda *_: (0, 0)
    w_spec = pl.BlockSpec((width, d), const)
    col = pl.BlockSpec((HEAD_DIM, 1), const)
    chunked = pl.BlockSpec((1, 1, width, tm), lambda bi, i: (bi, i, 0, 0))
    return pl.pallas_call(
        functools.partial(_fox_proj_kernel, n_heads=n_heads),
        grid=(b, s // tm),
        in_specs=[
            pl.BlockSpec((1, tm, d), lambda bi, i: (bi, i, 0)),
            pl.BlockSpec((1, d), const),
            w_spec, w_spec, w_spec, w_spec,
            pl.BlockSpec((n_heads, d), const),
            pl.BlockSpec((n_heads, 1), const),
            col, col,
        ],
        out_specs=[chunked, chunked, chunked, chunked,
                   pl.BlockSpec((1, 1, n_heads, tm), lambda bi, i: (bi, i, 0, 0))],
        out_shape=[
            jax.ShapeDtypeStruct((b, s // tm, width, tm), BF16),
            jax.ShapeDtypeStruct((b, s // tm, width, tm), BF16),
            jax.ShapeDtypeStruct((b, s // tm, width, tm), BF16),
            jax.ShapeDtypeStruct((b, s // tm, width, tm), BF16),
            jax.ShapeDtypeStruct((b, s // tm, n_heads, tm), F32),
        ],
        scratch_shapes=[pltpu.VMEM((n_heads, V7X_LANES), F32)],
        compiler_params=_params(2),
        name="fox_proj",
    )(x, a_norm.reshape(1, d), wq, wk, wv, wg, wf, b_f.reshape(n_heads, 1),
      g_q.reshape(HEAD_DIM, 1), g_k.reshape(HEAD_DIM, 1))


def _store_scores(score_block, s_ref, mx_ref, idx, diagonal):
    t = s_ref.shape[1]
    if not diagonal:
        s = score_block(0, t, 0, t)
        s_ref[idx] = s
        mx_ref[idx] = jnp.max(s, axis=0, keepdims=True)
        return
    half = t // 2
    top = score_block(0, half, 0, t)
    bottom = score_block(half, half, half, half)
    s_ref[idx, 0:half, :] = top
    s_ref[idx, half:t, 0:half] = jnp.full((half, half), NEG_INF, F32)
    s_ref[idx, half:t, half:t] = bottom
    top_max = jnp.max(top, axis=0, keepdims=True)
    bottom_max = jnp.max(bottom, axis=0, keepdims=True)
    mx_ref[idx] = jnp.concatenate(
        [top_max[:, 0:half], jnp.maximum(top_max[:, half:t], bottom_max)], axis=1)


def _softmax_step(s_ref, mx_ref, m_ref, alpha_ref, p_ref, idx):
    m_prev = m_ref[idx]
    m_new = jnp.maximum(m_prev, mx_ref[idx])
    p_ref[idx] = jnp.exp2(s_ref[idx] - m_new).astype(BF16)
    alpha_ref[idx] = jnp.exp2(m_prev - m_new)
    m_ref[idx] = m_new


def _pv_step(v, p_ref, alpha_ref, acc_ref, idx, diagonal=False):
    if not diagonal:
        pv = jnp.dot(v, p_ref[idx], preferred_element_type=F32)
    else:
        t = p_ref.shape[1]
        half = t // 2
        top = jnp.dot(v[:, 0:half], p_ref[idx, 0:half, :], preferred_element_type=F32)
        bottom = jnp.dot(v[:, half:t], p_ref[idx, half:t, half:t], preferred_element_type=F32)
        pv = jnp.concatenate([top[:, 0:half], top[:, half:t] + bottom], axis=1)
    acc_ref[idx] = alpha_ref[idx] * acc_ref[idx] + pv


def _with_ones_rows(v):
    return jnp.concatenate([v, jnp.ones((ONES_ROWS, v.shape[1]), v.dtype)], axis=0)


def _reset_softmax_state(m_ref, acc_ref):
    m_ref[...] = jnp.full(m_ref.shape, NEG_INF, F32)
    acc_ref[...] = jnp.zeros_like(acc_ref)


def _run_row(n_q, tail, prepare, make_produce, consume, pv, close_tile, open_tile):
    n_tail = len(tail)
    assert n_q > n_tail + 1
    prepare(0)

    def kinds_of(q_tile):
        return ((("far",) * (q_tile + 1)) + tail)[-(q_tile + 1):]

    def first_kind(q_tile):
        return kinds_of(q_tile)[0] if q_tile <= n_tail else "far"

    def query_tile(qi, kinds):
        produce = make_produce(qi)
        successor = min(qi + 1, n_q - 1) if isinstance(qi, int) else jnp.minimum(qi + 1, n_q - 1)
        produce_next = functools.partial(make_produce(successor), 0)

        def step(j, produce_following):
            pv(j - 1)
            consume()
            produce_following()

        def first_step(produce_following):
            if not (isinstance(qi, int) and qi == 0):
                pv(qi - 1, diagonal=True)
                close_tile(qi - 1)
            open_tile()
            consume()
            produce_following()
            prepare(successor)

        def rest(first_tile, produced_kinds, following_first_kind):
            j = first_tile
            for kind in produced_kinds:
                step(j, functools.partial(produce, j + 1, kind))
                j = j + 1
            step(j, functools.partial(produce_next, following_first_kind))

        if kinds is not None:
            if qi == 0:
                produce(0, kinds[0])
                first_step(functools.partial(produce_next, first_kind(1)))
            else:
                first_step(functools.partial(produce, 1, kinds[1]))
                rest(1, kinds[2:], first_kind(qi + 1))
            return

        n_far = qi - n_tail - 1
        remainder = n_far % FAR_STEPS_PER_BLOCK

        def far_steps(first, count):
            for j in range(count):
                step(first + j, functools.partial(produce, first + j + 1, "far"))

        for r in range(FAR_STEPS_PER_BLOCK):
            @pl.when(remainder == r)
            def _(r=r):
                first_step(functools.partial(produce, 1, "far"))
                far_steps(1, r)

        def far_block(i, carry):
            far_steps(1 + remainder + FAR_STEPS_PER_BLOCK * i, FAR_STEPS_PER_BLOCK)
            return carry
        lax.fori_loop(0, n_far // FAR_STEPS_PER_BLOCK, far_block, 0)
        rest(qi - n_tail, tail, "far")

    for qi in range(n_tail + 1):
        query_tile(qi, kinds_of(qi))

    def general_tile(qi, carry):
        query_tile(qi, None)
        return carry
    lax.fori_loop(n_tail + 1, n_q, general_tile, 0)

    pv(n_q - 1, diagonal=True)
    close_tile(n_q - 1)


def _aug_rows(rows, t):
    shape = (V7X_SUBLANES, t)
    row = lax.broadcasted_iota(jnp.int32, shape, 0)
    out = jnp.zeros(shape, F32)
    for r, val in enumerate(rows):
        out = jnp.where(row == r, val, out)
    return out


def _fox_attn_kernel(q_ref, k_ref, v_ref, c_ref, g_ref, o_ref,
                     kaug_ref, vaug_ref, m_ref, alpha_ref, mx_ref, acc_ref, s_ref, p_ref,
                     *, hp, n_chunks):
    t = q_ref.shape[-1]
    pad = jnp.zeros((AUG_DEPTH - HEAD_DIM - V7X_SUBLANES, t), F32)

    def prepare(j):
        for h in range(hp):
            kt = k_ref[0, j, h * HEAD_DIM:(h + 1) * HEAD_DIM, :].astype(F32)
            hi, mid, lo = _split3(c_ref[0, j, 0, h:h + 1, :])
            aug = _aug_rows([-hi, -mid, -lo, 1.0, 1.0, 1.0], t)
            kaug_ref[h, j] = jnp.concatenate([kt, aug, pad], axis=0).T.astype(BF16)
            vaug_ref[h, j] = _with_ones_rows(v_ref[0, j, h * HEAD_DIM:(h + 1) * HEAD_DIM, :])

    def make_produce(qi):
        qa = []
        for h in range(hp):
            qf = q_ref[0, qi, h * HEAD_DIM:(h + 1) * HEAD_DIM, :].astype(F32)
            hi, mid, lo = _split3(c_ref[0, qi, 0, h:h + 1, :])
            aug = _aug_rows([1.0, 1.0, 1.0, hi, mid, lo], t)
            qa.append(jnp.concatenate([qf, aug, pad], axis=0).astype(BF16))

        def produce(j, kind):
            for h in range(hp):
                def score_block(r0, rn, c0, cn, h=h):
                    s = jnp.dot(kaug_ref[h, j, r0:r0 + rn, :], qa[h][:, c0:c0 + cn],
                                preferred_element_type=F32)
                    if kind == "diag":
                        kpos = r0 + lax.broadcasted_iota(jnp.int32, (rn, cn), 0)
                        qpos = c0 + lax.broadcasted_iota(jnp.int32, (rn, cn), 1)
                        s = jnp.where(kpos <= qpos, s, NEG_INF)
                    return s
                _store_scores(score_block, s_ref, mx_ref, h, kind == "diag")
        return produce

    def consume():
        for h in range(hp):
            _softmax_step(s_ref, mx_ref, m_ref, alpha_ref, p_ref, h)

    def pv(j, diagonal=False):
        for h in range(hp):
            _pv_step(vaug_ref[h, j], p_ref, alpha_ref, acc_ref, h, diagonal)

    def close_tile(qi):
        for h in range(hp):
            o = acc_ref[h, 0:HEAD_DIM] * (1.0 / acc_ref[h, HEAD_DIM:HEAD_DIM + 1])
            g = g_ref[0, qi, h * HEAD_DIM:(h + 1) * HEAD_DIM, :].astype(F32)
            o_ref[0, qi, h * HEAD_DIM:(h + 1) * HEAD_DIM, :] = (o * _silu(g)).astype(BF16)

    _run_row(n_chunks, ("diag",), prepare, make_produce, consume, pv, close_tile,
             functools.partial(_reset_softmax_state, m_ref, acc_ref))


def _fox_attn(q_c, k_c, v_c, c_c, g_c):
    b, n_chunks, width, t = q_c.shape
    hp = FOX_HEADS_PER_STEP
    n_heads = width // HEAD_DIM
    rows = hp * HEAD_DIM
    c_c = c_c.reshape(b, n_chunks, n_heads // hp, hp, t)
    full = pl.BlockSpec((1, n_chunks, rows, t), lambda bi, hg: (bi, 0, hg, 0))
    return pl.pallas_call(
        functools.partial(_fox_attn_kernel, hp=hp, n_chunks=n_chunks),
        grid=(b, n_heads // hp),
        in_specs=[
            full, full, full,
            pl.BlockSpec((1, n_chunks, 1, hp, t), lambda bi, hg: (bi, 0, hg, 0, 0)),
            full,
        ],
        out_specs=full,
        out_shape=jax.ShapeDtypeStruct((b, n_chunks, width, t), BF16),
        scratch_shapes=[
            pltpu.VMEM((hp, n_chunks, t, AUG_DEPTH), BF16),
            pltpu.VMEM((hp, n_chunks, HEAD_DIM + ONES_ROWS, t), BF16),
            pltpu.VMEM((hp, 1, t), F32),
            pltpu.VMEM((hp, 1, t), F32),
            pltpu.VMEM((hp, 1, t), F32),
            pltpu.VMEM((hp, HEAD_DIM + ONES_ROWS, t), F32),
            pltpu.VMEM((hp, t, t), F32),
            pltpu.VMEM((hp, t, t), BF16),
        ],
        compiler_params=_params(2),
        name="fox_attn",
    )(q_c, k_c, v_c, c_c, g_c)


def _mid_kernel(o_ref, x_ref, wo_ref, gkv_ref, gb_ref, wkv_ref, wb_ref, gk_ref, gq_ref,
                h_ref, k_ref, v_ref, q_ref, g_ref, *, n_heads):
    half = n_heads * HEAD_DIM
    h1 = x_ref[0] + lax.dot_general(o_ref[0, 0], wo_ref[...], _TN, preferred_element_type=F32)
    h_ref[0] = h1
    ms = jnp.mean(h1 * h1, axis=-1, keepdims=True)
    hn = h1 * lax.rsqrt(ms + RMS_EPS)
    u_kv = (hn * gkv_ref[...]).astype(BF16)
    u_b = (hn * gb_ref[...]).astype(BF16)

    def pack_heads(ref, a, b2):
        for h in range(n_heads):
            sl = slice(h * HEAD_DIM, (h + 1) * HEAD_DIM)
            ref[..., h * 2 * HEAD_DIM:h * 2 * HEAD_DIM + HEAD_DIM, :] = a[sl]
            ref[..., h * 2 * HEAD_DIM + HEAD_DIM:(h + 1) * 2 * HEAD_DIM, :] = b2[sl]

    kv = lax.dot_general(wkv_ref[...], u_kv, _NT, preferred_element_type=F32)
    k1 = _head_norm(kv[0:half], gk_ref[0], n_heads, 1.0).astype(BF16)
    k2 = _head_norm(kv[half:2 * half], gk_ref[1], n_heads, 1.0).astype(BF16)
    pack_heads(k_ref.at[0, 0], k1, k2)
    v_ref[0, 0] = kv[2 * half:].astype(BF16)

    qg = lax.dot_general(wb_ref[...], u_b, _NT, preferred_element_type=F32)
    q1 = _head_norm(qg[0:half], gq_ref[0], n_heads, QK_SCALE).astype(BF16)
    q2 = _head_norm(qg[half:2 * half], gq_ref[1], n_heads, QK_SCALE).astype(BF16)
    pack_heads(q_ref.at[0, 0], q1, q2)
    g_ref[0, 0] = qg[2 * half:].astype(BF16)


def _mid(o_c, x, w_out, kv_norm, kv_w, kv_k_norm, b_norm, b_w_in, b_q_norm):
    b, s, d = x.shape
    tm = ROW_TILE
    n_heads = kv_w.shape[1] // (4 * HEAD_DIM)
    width = 2 * n_heads * HEAD_DIM
    const2 = lambda *_: (0, 0)
    const3 = lambda *_: (0, 0, 0)
    tok = pl.BlockSpec((1, tm, d), lambda bi, i: (bi, i, 0))
    chunked = pl.BlockSpec((1, 1, width, tm), lambda bi, i: (bi, i, 0, 0))
    return pl.pallas_call(
        functools.partial(_mid_kernel, n_heads=n_heads),
        grid=(b, s // tm),
        in_specs=[
            pl.BlockSpec((1, 1, o_c.shape[2], tm), lambda bi, i: (bi, i, 0, 0)),
            tok,
            pl.BlockSpec(w_out.shape, const2),
            pl.BlockSpec((1, d), const2),
            pl.BlockSpec((1, d), const2),
            pl.BlockSpec((2 * width, d), const2),
            pl.BlockSpec((2 * width, d), const2),
            pl.BlockSpec((2, HEAD_DIM, 1), const3),
            pl.BlockSpec((2, HEAD_DIM, 1), const3),
        ],
        out_specs=[tok, chunked, chunked, chunked, chunked],
        out_shape=[
            jax.ShapeDtypeStruct((b, s, d), F32),
            jax.ShapeDtypeStruct((b, s // tm, width, tm), BF16),
            jax.ShapeDtypeStruct((b, s // tm, width, tm), BF16),
            jax.ShapeDtypeStruct((b, s // tm, width, tm), BF16),
            jax.ShapeDtypeStruct((b, s // tm, width, tm), BF16),
        ],
        compiler_params=_params(2),
        name="mid_proj",
    )(o_c, x, w_out.astype(BF16), kv_norm.reshape(1, d), b_norm.reshape(1, d),
      kv_w.T.astype(BF16), b_w_in.T.astype(BF16),
      kv_k_norm.reshape(2, HEAD_DIM, 1), b_q_norm.reshape(2, HEAD_DIM, 1))


def _diff_attn_kernel(q_ref, k_ref, v_ref, brow_ref, lam_ref, gs_ref, g_ref, o_ref,
                      ktok_ref, vaug_ref, bias_ref, m_ref, alpha_ref, mx_ref, acc_ref, s_ref, p_ref,
                      *, hd, n_chunks):
    t = q_ref.shape[-1]
    dv = DIFF_V_DIM
    streams = [(h, mp) for h in range(hd) for mp in range(2)]

    def prepare(j):
        for h in range(hd):
            ktok_ref[h, j] = k_ref[0, j, h * dv:(h + 1) * dv, :].astype(F32).T.astype(BF16)
            vaug_ref[h, j] = _with_ones_rows(v_ref[0, j, h * dv:(h + 1) * dv, :])

    for h, mp in streams:
        for off in range(2):
            rows = jnp.broadcast_to(brow_ref[h, mp, off], (t, 2 * t))
            bias_ref[h, mp, off] = pltpu.roll(rows, 0, 1, stride=1, stride_axis=0)[:, 0:t]

    def make_produce(qi):
        qa = []
        for h in range(hd):
            q = q_ref[0, qi, h * dv:(h + 1) * dv, :]
            row = lax.broadcasted_iota(jnp.int32, q.shape, 0)
            zero = jnp.zeros_like(q)
            qa += [jnp.where(row < HEAD_DIM, q, zero), jnp.where(row >= HEAD_DIM, q, zero)]

        def produce(j, kind):
            for i, (h, mp) in enumerate(streams):
                def score_block(r0, rn, c0, cn, i=i, h=h, mp=mp):
                    s = jnp.dot(ktok_ref[h, j, r0:r0 + rn, :], qa[i][:, c0:c0 + cn],
                                preferred_element_type=F32)
                    if kind == "near":
                        s = s + bias_ref[h, mp, 1, r0:r0 + rn, c0:c0 + cn]
                    elif kind == "diag":
                        kchunk = (r0 + lax.broadcasted_iota(jnp.int32, (rn, cn), 0)) // CHUNK
                        qchunk = (c0 + lax.broadcasted_iota(jnp.int32, (rn, cn), 1)) // CHUNK
                        s = jnp.where(kchunk <= qchunk,
                                      s + bias_ref[h, mp, 0, r0:r0 + rn, c0:c0 + cn], NEG_INF)
                    return s
                _store_scores(score_block, s_ref, mx_ref, i, kind == "diag")
        return produce

    def consume():
        for i in range(len(streams)):
            _softmax_step(s_ref, mx_ref, m_ref, alpha_ref, p_ref, i)

    def pv(j, diagonal=False):
        for i, (h, mp) in enumerate(streams):
            _pv_step(vaug_ref[h, j], p_ref, alpha_ref, acc_ref, i, diagonal)

    def close_tile(qi):
        lam_p = lam_ref[...]
        e1 = jnp.exp(jnp.sum(lam_p[0:1] * lam_p[1:2], axis=1, keepdims=True))
        e2 = jnp.exp(jnp.sum(lam_p[2:3] * lam_p[3:4], axis=1, keepdims=True))
        lam = e1 - e2 + LAMBDA_INIT
        for h in range(hd):
            a1, a2 = acc_ref[2 * h], acc_ref[2 * h + 1]
            o = a1[0:dv] * (1.0 / a1[dv:dv + 1]) - lam * (a2[0:dv] * (1.0 / a2[dv:dv + 1]))
            ms = jnp.mean(o * o, axis=0, keepdims=True)
            y = o * lax.rsqrt(ms + RMS_EPS) * gs_ref[...] * (1.0 - LAMBDA_INIT)
            g = g_ref[0, qi, h * dv:(h + 1) * dv, :].astype(F32)
            o_ref[0, qi, h * dv:(h + 1) * dv, :] = (y * _silu(g)).astype(BF16)

    _run_row(n_chunks, ("near", "diag"), prepare, make_produce, consume, pv, close_tile,
             functools.partial(_reset_softmax_state, m_ref, acc_ref))


def _t5_bucket(rel):
    half = NUM_BUCKETS // 2
    max_exact = half // 2
    ret = jnp.where(rel > 0, half, 0)
    n = jnp.abs(rel)
    n_f = jnp.maximum(n, 1).astype(jnp.float32)
    large = max_exact + (jnp.log(n_f / max_exact) / math.log(MAX_DISTANCE / max_exact)
                         * (half - max_exact)).astype(jnp.int32)
    large = jnp.minimum(large, half - 1)
    return ret + jnp.where(n < max_exact, n, large)


def _bias_rows(rel_bias, t):
    dw = jnp.arange(2 * t)
    d = jnp.where(dw < t, dw, dw - 2 * t)
    rel = jnp.stack([-d, -d - t])
    rows = rel_bias[_t5_bucket(rel)].astype(F32)
    far = rel_bias[_t5_bucket(jnp.int32(-MAX_DISTANCE))].astype(F32)
    return ((rows - far) * LOG2E).transpose(3, 2, 0, 1)[:, :, :, None, :]


def _diff_attn(q_c, k_c, v_c, g_c, rel_bias, lam_params, g_sub):
    b, n_chunks, width, t = q_c.shape
    n_heads = width // DIFF_V_DIM
    hd = DIFF_HEADS_PER_STEP
    assert t >= MAX_DISTANCE
    assert (t // 2) % CHUNK == 0
    rows = hd * DIFF_V_DIM
    n_streams = 2 * hd
    full = pl.BlockSpec((1, n_chunks, rows, t), lambda bi, hg: (bi, 0, hg, 0))
    return pl.pallas_call(
        functools.partial(_diff_attn_kernel, hd=hd, n_chunks=n_chunks),
        grid=(b, n_heads // hd),
        in_specs=[
            full, full, full,
            pl.BlockSpec((hd, 2, 2, 1, 2 * t), lambda bi, hg: (hg, 0, 0, 0, 0)),
            pl.BlockSpec((4, HEAD_DIM), lambda *_: (0, 0)),
            pl.BlockSpec((DIFF_V_DIM, 1), lambda *_: (0, 0)),
            full,
        ],
        out_specs=full,
        out_shape=jax.ShapeDtypeStruct((b, n_chunks, width, t), BF16),
        scratch_shapes=[
            pltpu.VMEM((hd, n_chunks, t, DIFF_V_DIM), BF16),
            pltpu.VMEM((hd, n_chunks, DIFF_V_DIM + ONES_ROWS, t), BF16),
            pltpu.VMEM((hd, 2, 2, t, t), F32),
            pltpu.VMEM((n_streams, 1, t), F32),
            pltpu.VMEM((n_streams, 1, t), F32),
            pltpu.VMEM((n_streams, 1, t), F32),
            pltpu.VMEM((n_streams, DIFF_V_DIM + ONES_ROWS, t), F32),
            pltpu.VMEM((n_streams, t, t), F32),
            pltpu.VMEM((n_streams, t, t), BF16),
        ],
        compiler_params=_params(2),
        name="diff_attn",
    )(q_c, k_c, v_c, _bias_rows(rel_bias, t), lam_params, g_sub.reshape(DIFF_V_DIM, 1), g_c)


def _out_kernel(o_ref, h_ref, w_ref, y_ref):
    y_ref[0] = h_ref[0] + lax.dot_general(o_ref[0, 0], w_ref[...], _TN, preferred_element_type=F32)


def _out_proj(o_c, h1, w_out):
    b, s, d = h1.shape
    tm = ROW_TILE
    tok = pl.BlockSpec((1, tm, d), lambda bi, i: (bi, i, 0))
    return pl.pallas_call(
        _out_kernel,
        grid=(b, s // tm),
        in_specs=[
            pl.BlockSpec((1, 1, o_c.shape[2], tm), lambda bi, i: (bi, i, 0, 0)),
            tok,
            pl.BlockSpec(w_out.shape, lambda *_: (0, 0)),
        ],
        out_specs=tok,
        out_shape=jax.ShapeDtypeStruct((b, s, d), F32),
        compiler_params=_params(2),
        name="out_proj",
    )(o_c, h1, w_out.astype(BF16))


def kernel(x, a_norm, a_w_in, a_b_f, a_q_norm, a_k_norm, a_w_out, kv_norm, kv_w, kv_k_norm,
           rel_bias, b_norm, b_w_in, b_q_norm, b_lam_q1, b_lam_k1, b_lam_q2, b_lam_k2,
           b_sub_norm, b_w_out):
    assert a_norm.shape[0] == 1 and b_norm.shape[0] == 1
    assert x.shape[1] % ATTN_TILE == 0 and ATTN_TILE == ROW_TILE
    q_c, k_c, v_c, g_c, c_c = _fox_proj(x, a_norm[0], a_w_in[0], a_b_f[0], a_q_norm[0], a_k_norm[0])
    o_c = _fox_attn(q_c, k_c, v_c, c_c, g_c)
    h1, k2_c, v2_c, q2_c, g2_c = _mid(o_c, x, a_w_out[0], kv_norm, kv_w, kv_k_norm,
                                      b_norm[0], b_w_in[0], b_q_norm[0])
    lam_params = jnp.stack([b_lam_q1[0], b_lam_k1[0], b_lam_q2[0], b_lam_k2[0]])
    o2_c = _diff_attn(q2_c, k2_c, v2_c, g2_c, rel_bias, lam_params, b_sub_norm[0])
    return _out_proj(o2_c, h1, b_w_out[0])
```

```python
import functools
import math

import jax
import jax.numpy as jnp
from jax import lax
from jax.experimental import pallas as pl
from jax.experimental.pallas import tpu as pltpu

HEAD_DIM = 64
DIFF_V_DIM = 2 * HEAD_DIM
CHUNK = 64
NUM_BUCKETS = 32
MAX_DISTANCE = 128
RMS_EPS = 1e-6
NEG_INF = -1e30
LOG2E = math.log2(math.e)
QK_SCALE = HEAD_DIM ** -0.5 * LOG2E
LAMBDA_INIT = 0.8 - 0.6 * math.exp(-0.3 * 1)

V7X_LANES = 128
V7X_SUBLANES = 8
V7X_VMEM_LIMIT_BYTES = 56 * 1024 * 1024

ROW_TILE = 512
ATTN_TILE = 512
FOX_HEADS_PER_STEP = 2
DIFF_HEADS_PER_STEP = 1
ONES_ROWS = 16
AUG_DEPTH = 2 * HEAD_DIM
FAR_STEPS_PER_BLOCK = 2

F32 = jnp.float32
BF16 = jnp.bfloat16
_NT = (((1,), (1,)), ((), ()))
_TN = (((0,), (0,)), ((), ()))


def _params(n_axes):
    return pltpu.CompilerParams(
        dimension_semantics=("arbitrary",) * n_axes,
        vmem_limit_bytes=V7X_VMEM_LIMIT_BYTES)


def _head_norm(p_t, g_col, n_heads, scale):
    t = p_t.shape[-1]
    p3 = p_t.reshape(n_heads, HEAD_DIM, t)
    ms = jnp.mean(p3 * p3, axis=1, keepdims=True)
    y = p3 * lax.rsqrt(ms + RMS_EPS) * (g_col * scale)[None]
    return y.reshape(n_heads * HEAD_DIM, t)


def _split3(c):
    hi = c.astype(BF16).astype(F32)
    r = c - hi
    mid = r.astype(BF16).astype(F32)
    lo = (r - mid).astype(BF16).astype(F32)
    return hi, mid, lo


def _silu(g):
    return g / (1.0 + jnp.exp(-g))


def _fox_proj_kernel(x_ref, gn_ref, wq_ref, wk_ref, wv_ref, wg_ref, wf_ref, bf_ref,
                     gq_ref, gk_ref, q_ref, k_ref, v_ref, g_ref, c_ref, carry_ref,
                     *, n_heads):
    @pl.when(pl.program_id(1) == 0)
    def _():
        carry_ref[...] = jnp.zeros_like(carry_ref)

    x = x_ref[0]
    tm = x.shape[0]
    ms = jnp.mean(x * x, axis=-1, keepdims=True)
    u = (x * lax.rsqrt(ms + RMS_EPS) * gn_ref[...]).astype(BF16)

    def proj(w_ref):
        return lax.dot_general(w_ref[...], u, _NT, preferred_element_type=F32)

    z = proj(wf_ref) + bf_ref[...]
    log_f = jnp.minimum(z, 0.0) - jnp.log1p(jnp.exp(-jnp.abs(z)))
    lane = lax.broadcasted_iota(jnp.int32, log_f.shape, 1)
    cs = log_f
    shift = 1
    while shift < tm:
        cs = cs + jnp.where(lane >= shift, pltpu.roll(cs, shift, axis=1), 0.0)
        shift *= 2
    c = cs + carry_ref[:, 0:1]
    c_ref[0, 0] = c * LOG2E
    carry_ref[...] = jnp.broadcast_to(c[:, tm - 1:tm], carry_ref.shape)

    q_ref[0, 0] = _head_norm(proj(wq_ref), gq_ref[...], n_heads, QK_SCALE).astype(BF16)
    k_ref[0, 0] = _head_norm(proj(wk_ref), gk_ref[...], n_heads, 1.0).astype(BF16)
    v_ref[0, 0] = proj(wv_ref).astype(BF16)
    g_ref[0, 0] = proj(wg_ref).astype(BF16)


def _fox_proj(x, a_norm, w_in, b_f, g_q, g_k):
    b, s, d = x.shape
    n_heads = b_f.shape[0]
    width = n_heads * HEAD_DIM
    tm = ROW_TILE
    w_t = w_in.T.astype(BF16)
    wq, wk, wv, wg, wf = (w_t[0:width], w_t[width:2 * width], w_t[2 * width:3 * width],
                          w_t[3 * width:4 * width], w_t[4 * width:])
    const = lambda *_: (0, 0)
    w_spec = pl.BlockSpec((width, d), const)
    col = pl.BlockSpec((HEAD_DIM, 1), const)
    chunked = pl.BlockSpec((1, 1, width, tm), lambda bi, i: (bi, i, 0, 0))
    return pl.pallas_call(
        functools.partial(_fox_proj_kernel, n_heads=n_heads),
        grid=(b, s // tm),
        in_specs=[
            pl.BlockSpec((1, tm, d), lambda bi, i: (bi, i, 0)),
            pl.BlockSpec((1, d), const),
            w_spec, w_spec, w_spec, w_spec,
            pl.BlockSpec((n_heads, d), const),
            pl.BlockSpec((n_heads, 1), const),
            col, col,
        ],
        out_specs=[chunked, chunked, chunked, chunked,
                   pl.BlockSpec((1, 1, n_heads, tm), lambda bi, i: (bi, i, 0, 0))],
        out_shape=[
            jax.ShapeDtypeStruct((b, s // tm, width, tm), BF16),
            jax.ShapeDtypeStruct((b, s // tm, width, tm), BF16),
            jax.ShapeDtypeStruct((b, s // tm, width, tm), BF16),
            jax.ShapeDtypeStruct((b, s // tm, width, tm), BF16),
            jax.ShapeDtypeStruct((b, s // tm, n_heads, tm), F32),
        ],
        scratch_shapes=[pltpu.VMEM((n_heads, V7X_LANES), F32)],
        compiler_params=_params(2),
        name="fox_proj",
    )(x, a_norm.reshape(1, d), wq, wk, wv, wg, wf, b_f.reshape(n_heads, 1),
      g_q.reshape(HEAD_DIM, 1), g_k.reshape(HEAD_DIM, 1))


def _store_scores(score_block, s_ref, mx_ref, idx, diagonal):
    t = s_ref.shape[1]
    if not diagonal:
        s = score_block(0, t, 0, t)
        s_ref[idx] = s
        mx_ref[idx] = jnp.max(s, axis=0, keepdims=True)
        return
    half = t // 2
    top = score_block(0, half, 0, t)
    bottom = score_block(half, half, half, half)
    s_ref[idx, 0:half, :] = top
    s_ref[idx, half:t, 0:half] = jnp.full((half, half), NEG_INF, F32)
    s_ref[idx, half:t, half:t] = bottom
    top_max = jnp.max(top, axis=0, keepdims=True)
    bottom_max = jnp.max(bottom, axis=0, keepdims=True)
    mx_ref[idx] = jnp.concatenate(
        [top_max[:, 0:half], jnp.maximum(top_max[:, half:t], bottom_max)], axis=1)


def _softmax_step(s_ref, mx_ref, m_ref, alpha_ref, p_ref, idx):
    m_prev = m_ref[idx]
    m_new = jnp.maximum(m_prev, mx_ref[idx])
    p_ref[idx] = jnp.exp2(s_ref[idx] - m_new).astype(BF16)
    alpha_ref[idx] = jnp.exp2(m_prev - m_new)
    m_ref[idx] = m_new


def _pv_step(v, p_ref, alpha_ref, acc_ref, idx, diagonal=False):
    if not diagonal:
        pv = jnp.dot(v, p_ref[idx], preferred_element_type=F32)
    else:
        t = p_ref.shape[1]
        half = t // 2
        top = jnp.dot(v[:, 0:half], p_ref[idx, 0:half, :], preferred_element_type=F32)
        bottom = jnp.dot(v[:, half:t], p_ref[idx, half:t, half:t], preferred_element_type=F32)
        pv = jnp.concatenate([top[:, 0:half], top[:, half:t] + bottom], axis=1)
    acc_ref[idx] = alpha_ref[idx] * acc_ref[idx] + pv


def _with_ones_rows(v):
    return jnp.concatenate([v, jnp.ones((ONES_ROWS, v.shape[1]), v.dtype)], axis=0)


def _reset_softmax_state(m_ref, acc_ref):
    m_ref[...] = jnp.full(m_ref.shape, NEG_INF, F32)
    acc_ref[...] = jnp.zeros_like(acc_ref)


def _run_row(n_q, tail, prepare, make_produce, consume, pv, close_tile, open_tile):
    n_tail = len(tail)
    assert n_q > n_tail + 1
    prepare(0)

    def kinds_of(q_tile):
        return ((("far",) * (q_tile + 1)) + tail)[-(q_tile + 1):]

    def first_kind(q_tile):
        return kinds_of(q_tile)[0] if q_tile <= n_tail else "far"

    def query_tile(qi, kinds):
        produce = make_produce(qi)
        successor = min(qi + 1, n_q - 1) if isinstance(qi, int) else jnp.minimum(qi + 1, n_q - 1)
        produce_next = functools.partial(make_produce(successor), 0)

        def step(j, produce_following):
            pv(j - 1)
            consume()
            produce_following()

        def first_step(produce_following):
            if not (isinstance(qi, int) and qi == 0):
                pv(qi - 1, diagonal=True)
                close_tile(qi - 1)
            open_tile()
            consume()
            produce_following()
            prepare(successor)

        def rest(first_tile, produced_kinds, following_first_kind):
            j = first_tile
            for kind in produced_kinds:
                step(j, functools.partial(produce, j + 1, kind))
                j = j + 1
            step(j, functools.partial(produce_next, following_first_kind))

        if kinds is not None:
            if qi == 0:
                produce(0, kinds[0])
                first_step(functools.partial(produce_next, first_kind(1)))
            else:
                first_step(functools.partial(produce, 1, kinds[1]))
                rest(1, kinds[2:], first_kind(qi + 1))
            return

        n_far = qi - n_tail - 1
        remainder = n_far % FAR_STEPS_PER_BLOCK

        def far_steps(first, count):
            for j in range(count):
                step(first + j, functools.partial(produce, first + j + 1, "far"))

        for r in range(FAR_STEPS_PER_BLOCK):
            @pl.when(remainder == r)
            def _(r=r):
                first_step(functools.partial(produce, 1, "far"))
                far_steps(1, r)

        def far_block(i, carry):
            far_steps(1 + remainder + FAR_STEPS_PER_BLOCK * i, FAR_STEPS_PER_BLOCK)
            return carry
        lax.fori_loop(0, n_far // FAR_STEPS_PER_BLOCK, far_block, 0)
        rest(qi - n_tail, tail, "far")

    for qi in range(n_tail + 1):
        query_tile(qi, kinds_of(qi))

    def general_tile(qi, carry):
        query_tile(qi, None)
        return carry
    lax.fori_loop(n_tail + 1, n_q, general_tile, 0)

    pv(n_q - 1, diagonal=True)
    close_tile(n_q - 1)


def _aug_rows(rows, t):
    shape = (V7X_SUBLANES, t)
    row = lax.broadcasted_iota(jnp.int32, shape, 0)
    out = jnp.zeros(shape, F32)
    for r, val in enumerate(rows):
        out = jnp.where(row == r, val, out)
    return out


def _fox_attn_kernel(q_ref, k_ref, v_ref, c_ref, g_ref, o_ref,
                     kaug_ref, vaug_ref, m_ref, alpha_ref, mx_ref, acc_ref, s_ref, p_ref,
                     *, hp, n_chunks):
    t = q_ref.shape[-1]
    pad = jnp.zeros((AUG_DEPTH - HEAD_DIM - V7X_SUBLANES, t), F32)

    def prepare(j):
        for h in range(hp):
            kt = k_ref[0, j, h * HEAD_DIM:(h + 1) * HEAD_DIM, :].astype(F32)
            hi, mid, lo = _split3(c_ref[0, j, 0, h:h + 1, :])
            aug = _aug_rows([-hi, -mid, -lo, 1.0, 1.0, 1.0], t)
            kaug_ref[h, j] = jnp.concatenate([kt, aug, pad], axis=0).T.astype(BF16)
            vaug_ref[h, j] = _with_ones_rows(v_ref[0, j, h * HEAD_DIM:(h + 1) * HEAD_DIM, :])

    def make_produce(qi):
        qa = []
        for h in range(hp):
            qf = q_ref[0, qi, h * HEAD_DIM:(h + 1) * HEAD_DIM, :].astype(F32)
            hi, mid, lo = _split3(c_ref[0, qi, 0, h:h + 1, :])
            aug = _aug_rows([1.0, 1.0, 1.0, hi, mid, lo], t)
            qa.append(jnp.concatenate([qf, aug, pad], axis=0).astype(BF16))

        def produce(j, kind):
            for h in range(hp):
                def score_block(r0, rn, c0, cn, h=h):
                    s = jnp.dot(kaug_ref[h, j, r0:r0 + rn, :], qa[h][:, c0:c0 + cn],
                                preferred_element_type=F32)
                    if kind == "diag":
                        kpos = r0 + lax.broadcasted_iota(jnp.int32, (rn, cn), 0)
                        qpos = c0 + lax.broadcasted_iota(jnp.int32, (rn, cn), 1)
                        s = jnp.where(kpos <= qpos, s, NEG_INF)
                    return s
                _store_scores(score_block, s_ref, mx_ref, h, kind == "diag")
        return produce

    def consume():
        for h in range(hp):
            _softmax_step(s_ref, mx_ref, m_ref, alpha_ref, p_ref, h)

    def pv(j, diagonal=False):
        for h in range(hp):
            _pv_step(vaug_ref[h, j], p_ref, alpha_ref, acc_ref, h, diagonal)

    def close_tile(qi):
        for h in range(hp):
            o = acc_ref[h, 0:HEAD_DIM] * (1.0 / acc_ref[h, HEAD_DIM:HEAD_DIM + 1])
            g = g_ref[0, qi, h * HEAD_DIM:(h + 1) * HEAD_DIM, :].astype(F32)
            o_ref[0, qi, h * HEAD_DIM:(h + 1) * HEAD_DIM, :] = (o * _silu(g)).astype(BF16)

    _run_row(n_chunks, ("diag",), prepare, make_produce, consume, pv, close_tile,
             functools.partial(_reset_softmax_state, m_ref, acc_ref))


def _fox_attn(q_c, k_c, v_c, c_c, g_c):
    b, n_chunks, width, t = q_c.shape
    hp = FOX_HEADS_PER_STEP
    n_heads = width // HEAD_DIM
    rows = hp * HEAD_DIM
    c_c = c_c.reshape(b, n_chunks, n_heads // hp, hp, t)
    full = pl.BlockSpec((1, n_chunks, rows, t), lambda bi, hg: (bi, 0, hg, 0))
    return pl.pallas_call(
        functools.partial(_fox_attn_kernel, hp=hp, n_chunks=n_chunks),
        grid=(b, n_heads // hp),
        in_specs=[
            full, full, full,
            pl.BlockSpec((1, n_chunks, 1, hp, t), lambda bi, hg: (bi, 0, hg, 0, 0)),
            full,
        ],
        out_specs=full,
        out_shape=jax.ShapeDtypeStruct((b, n_chunks, width, t), BF16),
        scratch_shapes=[
            pltpu.VMEM((hp, n_chunks, t, AUG_DEPTH), BF16),
            pltpu.VMEM((hp, n_chunks, HEAD_DIM + ONES_ROWS, t), BF16),
            pltpu.VMEM((hp, 1, t), F32),
            pltpu.VMEM((hp, 1, t), F32),
            pltpu.VMEM((hp, 1, t), F32),
            pltpu.VMEM((hp, HEAD_DIM + ONES_ROWS, t), F32),
            pltpu.VMEM((hp, t, t), F32),
            pltpu.VMEM((hp, t, t), BF16),
        ],
        compiler_params=_params(2),
        name="fox_attn",
    )(q_c, k_c, v_c, c_c, g_c)


def _mid_kernel(o_ref, x_ref, wo_ref, gkv_ref, gb_ref, wkv_ref, wb_ref, gk_ref, gq_ref,
                h_ref, k_ref, v_ref, q_ref, g_ref, *, n_heads):
    half = n_heads * HEAD_DIM
    h1 = x_ref[0] + lax.dot_general(o_ref[0, 0], wo_ref[...], _TN, preferred_element_type=F32)
    h_ref[0] = h1
    ms = jnp.mean(h1 * h1, axis=-1, keepdims=True)
    hn = h1 * lax.rsqrt(ms + RMS_EPS)
    u_kv = (hn * gkv_ref[...]).astype(BF16)
    u_b = (hn * gb_ref[...]).astype(BF16)

    def pack_heads(ref, a, b2):
        for h in range(n_heads):
            sl = slice(h * HEAD_DIM, (h + 1) * HEAD_DIM)
            ref[..., h * 2 * HEAD_DIM:h * 2 * HEAD_DIM + HEAD_DIM, :] = a[sl]
            ref[..., h * 2 * HEAD_DIM + HEAD_DIM:(h + 1) * 2 * HEAD_DIM, :] = b2[sl]

    def proj(w_ref, u, section):
        return lax.dot_general(w_ref[section * half:(section + 1) * half, :], u, _NT,
                               preferred_element_type=F32)

    k1 = _head_norm(proj(wkv_ref, u_kv, 0), gk_ref[0], n_heads, 1.0).astype(BF16)
    k2 = _head_norm(proj(wkv_ref, u_kv, 1), gk_ref[1], n_heads, 1.0).astype(BF16)
    pack_heads(k_ref.at[0, 0], k1, k2)
    for section in (2, 3):
        v_ref[0, 0, (section - 2) * half:(section - 1) * half, :] = proj(wkv_ref, u_kv, section).astype(BF16)

    q1 = _head_norm(proj(wb_ref, u_b, 0), gq_ref[0], n_heads, QK_SCALE).astype(BF16)
    q2 = _head_norm(proj(wb_ref, u_b, 1), gq_ref[1], n_heads, QK_SCALE).astype(BF16)
    pack_heads(q_ref.at[0, 0], q1, q2)
    for section in (2, 3):
        g_ref[0, 0, (section - 2) * half:(section - 1) * half, :] = proj(wb_ref, u_b, section).astype(BF16)


def _mid(o_c, x, w_out, kv_norm, kv_w, kv_k_norm, b_norm, b_w_in, b_q_norm):
    b, s, d = x.shape
    tm = ROW_TILE
    n_heads = kv_w.shape[1] // (4 * HEAD_DIM)
    width = 2 * n_heads * HEAD_DIM
    const2 = lambda *_: (0, 0)
    const3 = lambda *_: (0, 0, 0)
    tok = pl.BlockSpec((1, tm, d), lambda bi, i: (bi, i, 0))
    chunked = pl.BlockSpec((1, 1, width, tm), lambda bi, i: (bi, i, 0, 0))
    return pl.pallas_call(
        functools.partial(_mid_kernel, n_heads=n_heads),
        grid=(b, s // tm),
        in_specs=[
            pl.BlockSpec((1, 1, o_c.shape[2], tm), lambda bi, i: (bi, i, 0, 0)),
            tok,
            pl.BlockSpec(w_out.shape, const2),
            pl.BlockSpec((1, d), const2),
            pl.BlockSpec((1, d), const2),
            pl.BlockSpec((2 * width, d), const2),
            pl.BlockSpec((2 * width, d), const2),
            pl.BlockSpec((2, HEAD_DIM, 1), const3),
            pl.BlockSpec((2, HEAD_DIM, 1), const3),
        ],
        out_specs=[tok, chunked, chunked, chunked, chunked],
        out_shape=[
            jax.ShapeDtypeStruct((b, s, d), F32),
            jax.ShapeDtypeStruct((b, s // tm, width, tm), BF16),
            jax.ShapeDtypeStruct((b, s // tm, width, tm), BF16),
            jax.ShapeDtypeStruct((b, s // tm, width, tm), BF16),
            jax.ShapeDtypeStruct((b, s // tm, width, tm), BF16),
        ],
        compiler_params=_params(2),
        name="mid_proj",
    )(o_c, x, w_out.astype(BF16), kv_norm.reshape(1, d), b_norm.reshape(1, d),
      kv_w.T.astype(BF16), b_w_in.T.astype(BF16),
      kv_k_norm.reshape(2, HEAD_DIM, 1), b_q_norm.reshape(2, HEAD_DIM, 1))


def _diff_attn_kernel(q_ref, k_ref, v_ref, brow_ref, lam_ref, gs_ref, g_ref, o_ref,
                      ktok_ref, vaug_ref, bias_ref, m_ref, alpha_ref, mx_ref, acc_ref, s_ref, p_ref,
                      *, hd, n_chunks):
    t = q_ref.shape[-1]
    dv = DIFF_V_DIM
    streams = [(h, mp) for h in range(hd) for mp in range(2)]

    def prepare(j):
        for h in range(hd):
            ktok_ref[h, j] = k_ref[0, j, h * dv:(h + 1) * dv, :].astype(F32).T.astype(BF16)
            vaug_ref[h, j] = _with_ones_rows(v_ref[0, j, h * dv:(h + 1) * dv, :])

    for h, mp in streams:
        for off in range(2):
            rows = jnp.broadcast_to(brow_ref[h, mp, off], (t, 2 * t))
            bias_ref[h, mp, off] = pltpu.roll(rows, 0, 1, stride=1, stride_axis=0)[:, 0:t]

    def make_produce(qi):
        qa = []
        for h in range(hd):
            q = q_ref[0, qi, h * dv:(h + 1) * dv, :]
            row = lax.broadcasted_iota(jnp.int32, q.shape, 0)
            zero = jnp.zeros_like(q)
            qa += [jnp.where(row < HEAD_DIM, q, zero), jnp.where(row >= HEAD_DIM, q, zero)]

        def produce(j, kind):
            for i, (h, mp) in enumerate(streams):
                def score_block(r0, rn, c0, cn, i=i, h=h, mp=mp):
                    s = jnp.dot(ktok_ref[h, j, r0:r0 + rn, :], qa[i][:, c0:c0 + cn],
                                preferred_element_type=F32)
                    if kind == "near":
                        s = s + bias_ref[h, mp, 1, r0:r0 + rn, c0:c0 + cn]
                    elif kind == "diag":
                        kchunk = (r0 + lax.broadcasted_iota(jnp.int32, (rn, cn), 0)) // CHUNK
                        qchunk = (c0 + lax.broadcasted_iota(jnp.int32, (rn, cn), 1)) // CHUNK
                        s = jnp.where(kchunk <= qchunk,
                                      s + bias_ref[h, mp, 0, r0:r0 + rn, c0:c0 + cn], NEG_INF)
                    return s
                _store_scores(score_block, s_ref, mx_ref, i, kind == "diag")
        return produce

    def consume():
        for i in range(len(streams)):
            _softmax_step(s_ref, mx_ref, m_ref, alpha_ref, p_ref, i)

    def pv(j, diagonal=False):
        for i, (h, mp) in enumerate(streams):
            _pv_step(vaug_ref[h, j], p_ref, alpha_ref, acc_ref, i, diagonal)

    def close_tile(qi):
        lam_p = lam_ref[...]
        e1 = jnp.exp(jnp.sum(lam_p[0:1] * lam_p[1:2], axis=1, keepdims=True))
        e2 = jnp.exp(jnp.sum(lam_p[2:3] * lam_p[3:4], axis=1, keepdims=True))
        lam = e1 - e2 + LAMBDA_INIT
        for h in range(hd):
            a1, a2 = acc_ref[2 * h], acc_ref[2 * h + 1]
            o = a1[0:dv] * (1.0 / a1[dv:dv + 1]) - lam * (a2[0:dv] * (1.0 / a2[dv:dv + 1]))
            ms = jnp.mean(o * o, axis=0, keepdims=True)
            y = o * lax.rsqrt(ms + RMS_EPS) * gs_ref[...] * (1.0 - LAMBDA_INIT)
            g = g_ref[0, qi, h * dv:(h + 1) * dv, :].astype(F32)
            o_ref[0, qi, h * dv:(h + 1) * dv, :] = (y * _silu(g)).astype(BF16)

    _run_row(n_chunks, ("near", "diag"), prepare, make_produce, consume, pv, close_tile,
             functools.partial(_reset_softmax_state, m_ref, acc_ref))


def _t5_bucket(rel):
    half = NUM_BUCKETS // 2
    max_exact = half // 2
    ret = jnp.where(rel > 0, half, 0)
    n = jnp.abs(rel)
    n_f = jnp.maximum(n, 1).astype(jnp.float32)
    large = max_exact + (jnp.log(n_f / max_exact) / math.log(MAX_DISTANCE / max_exact)
                         * (half - max_exact)).astype(jnp.int32)
    large = jnp.minimum(large, half - 1)
    return ret + jnp.where(n < max_exact, n, large)


def _bias_rows(rel_bias, t):
    dw = jnp.arange(2 * t)
    d = jnp.where(dw < t, dw, dw - 2 * t)
    rel = jnp.stack([-d, -d - t])
    rows = rel_bias[_t5_bucket(rel)].astype(F32)
    far = rel_bias[_t5_bucket(jnp.int32(-MAX_DISTANCE))].astype(F32)
    return ((rows - far) * LOG2E).transpose(3, 2, 0, 1)[:, :, :, None, :]


def _diff_attn(q_c, k_c, v_c, g_c, rel_bias, lam_params, g_sub):
    b, n_chunks, width, t = q_c.shape
    n_heads = width // DIFF_V_DIM
    hd = DIFF_HEADS_PER_STEP
    assert t >= MAX_DISTANCE
    assert (t // 2) % CHUNK == 0
    rows = hd * DIFF_V_DIM
    n_streams = 2 * hd
    full = pl.BlockSpec((1, n_chunks, rows, t), lambda bi, hg: (bi, 0, hg, 0))
    return pl.pallas_call(
        functools.partial(_diff_attn_kernel, hd=hd, n_chunks=n_chunks),
        grid=(b, n_heads // hd),
        in_specs=[
            full, full, full,
            pl.BlockSpec((hd, 2, 2, 1, 2 * t), lambda bi, hg: (hg, 0, 0, 0, 0)),
            pl.BlockSpec((4, HEAD_DIM), lambda *_: (0, 0)),
            pl.BlockSpec((DIFF_V_DIM, 1), lambda *_: (0, 0)),
            full,
        ],
        out_specs=full,
        out_shape=jax.ShapeDtypeStruct((b, n_chunks, width, t), BF16),
        scratch_shapes=[
            pltpu.VMEM((hd, n_chunks, t, DIFF_V_DIM), BF16),
            pltpu.VMEM((hd, n_chunks, DIFF_V_DIM + ONES_ROWS, t), BF16),
            pltpu.VMEM((hd, 2, 2, t, t), F32),
            pltpu.VMEM((n_streams, 1, t), F32),
            pltpu.VMEM((n_streams, 1, t), F32),
            pltpu.VMEM((n_streams, 1, t), F32),
            pltpu.VMEM((n_streams, DIFF_V_DIM + ONES_ROWS, t), F32),
            pltpu.VMEM((n_streams, t, t), F32),
            pltpu.VMEM((n_streams, t, t), BF16),
        ],
        compiler_params=_params(2),
        name="diff_attn",
    )(q_c, k_c, v_c, _bias_rows(rel_bias, t), lam_params, g_sub.reshape(DIFF_V_DIM, 1), g_c)


def _out_kernel(o_ref, h_ref, w_ref, y_ref):
    y_ref[0] = h_ref[0] + lax.dot_general(o_ref[0, 0], w_ref[...], _TN, preferred_element_type=F32)


def _out_proj(o_c, h1, w_out):
    b, s, d = h1.shape
    tm = ROW_TILE
    tok = pl.BlockSpec((1, tm, d), lambda bi, i: (bi, i, 0))
    return pl.pallas_call(
        _out_kernel,
        grid=(b, s // tm),
        in_specs=[
            pl.BlockSpec((1, 1, o_c.shape[2], tm), lambda bi, i: (bi, i, 0, 0)),
            tok,
            pl.BlockSpec(w_out.shape, lambda *_: (0, 0)),
        ],
        out_specs=tok,
        out_shape=jax.ShapeDtypeStruct((b, s, d), F32),
        compiler_params=_params(2),
        name="out_proj",
    )(o_c, h1, w_out.astype(BF16))


def kernel(x, a_norm, a_w_in, a_b_f, a_q_norm, a_k_norm, a_w_out, kv_norm, kv_w, kv_k_norm,
           rel_bias, b_norm, b_w_in, b_q_norm, b_lam_q1, b_lam_k1, b_lam_q2, b_lam_k2,
           b_sub_norm, b_w_out):
    assert a_norm.shape[0] == 1 and b_norm.shape[0] == 1
    assert x.shape[1] % ATTN_TILE == 0 and ATTN_TILE == ROW_TILE
    q_c, k_c, v_c, g_c, c_c = _fox_proj(x, a_norm[0], a_w_in[0], a_b_f[0], a_q_norm[0], a_k_norm[0])
    o_c = _fox_attn(q_c, k_c, v_c, c_c, g_c)
    h1, k2_c, v2_c, q2_c, g2_c = _mid(o_c, x, a_w_out[0], kv_norm, kv_w, kv_k_norm,
                                      b_norm[0], b_w_in[0], b_q_norm[0])
    lam_params = jnp.stack([b_lam_q1[0], b_lam_k1[0], b_lam_q2[0], b_lam_k2[0]])
    o2_c = _diff_attn(q2_c, k2_c, v2_c, g2_c, rel_bias, lam_params, b_sub_norm[0])
    return _out_proj(o2_c, h1, b_w_out[0])
```

```python
import functools
import math

import jax
import jax.numpy as jnp
from jax import lax
from jax.experimental import pallas as pl
from jax.experimental.pallas import tpu as pltpu

HEAD_DIM = 64
DIFF_V_DIM = 2 * HEAD_DIM
CHUNK = 64
NUM_BUCKETS = 32
MAX_DISTANCE = 128
RMS_EPS = 1e-6
NEG_INF = -1e30
LOG2E = math.log2(math.e)
QK_SCALE = HEAD_DIM ** -0.5 * LOG2E
LAMBDA_INIT = 0.8 - 0.6 * math.exp(-0.3 * 1)

V7X_LANES = 128
V7X_SUBLANES = 8
V7X_VMEM_LIMIT_BYTES = 56 * 1024 * 1024

ROW_TILE = 512
ATTN_TILE = 512
FOX_HEADS_PER_STEP = 2
DIFF_HEADS_PER_STEP = 1
ONES_ROWS = 16
AUG_DEPTH = 2 * HEAD_DIM
FAR_STEPS_PER_BLOCK = 2

F32 = jnp.float32
BF16 = jnp.bfloat16
_NT = (((1,), (1,)), ((), ()))
_TN = (((0,), (0,)), ((), ()))


def _params(n_axes):
    return pltpu.CompilerParams(
        dimension_semantics=("arbitrary",) * n_axes,
        vmem_limit_bytes=V7X_VMEM_LIMIT_BYTES)


def _head_norm(p_t, g_col, n_heads, scale):
    t = p_t.shape[-1]
    p3 = p_t.reshape(n_heads, HEAD_DIM, t)
    ms = jnp.mean(p3 * p3, axis=1, keepdims=True)
    y = p3 * lax.rsqrt(ms + RMS_EPS) * (g_col * scale)[None]
    return y.reshape(n_heads * HEAD_DIM, t)


def _split3(c):
    hi = c.astype(BF16).astype(F32)
    r = c - hi
    mid = r.astype(BF16).astype(F32)
    lo = (r - mid).astype(BF16).astype(F32)
    return hi, mid, lo


def _silu(g):
    return g / (1.0 + jnp.exp(-g))


def _fox_proj_kernel(x_ref, gn_ref, wq_ref, wk_ref, wv_ref, wg_ref, wf_ref, bf_ref,
                     gq_ref, gk_ref, q_ref, k_ref, v_ref, g_ref, c_ref, carry_ref,
                     *, n_heads):
    @pl.when(pl.program_id(1) == 0)
    def _():
        carry_ref[...] = jnp.zeros_like(carry_ref)

    x = x_ref[0]
    tm = x.shape[0]
    ms = jnp.mean(x * x, axis=-1, keepdims=True)
    u = (x * lax.rsqrt(ms + RMS_EPS) * gn_ref[...]).astype(BF16)

    def proj(w_ref):
        return lax.dot_general(w_ref[...], u, _NT, preferred_element_type=F32)

    z = proj(wf_ref) + bf_ref[...]
    log_f = jnp.minimum(z, 0.0) - jnp.log1p(jnp.exp(-jnp.abs(z)))
    lane = lax.broadcasted_iota(jnp.int32, log_f.shape, 1)
    cs = log_f
    shift = 1
    while shift < tm:
        cs = cs + jnp.where(lane >= shift, pltpu.roll(cs, shift, axis=1), 0.0)
        shift *= 2
    c = cs + carry_ref[:, 0:1]
    c_ref[0, 0] = c * LOG2E
    carry_ref[...] = jnp.broadcast_to(c[:, tm - 1:tm], carry_ref.shape)

    q_ref[0, 0] = _head_norm(proj(wq_ref), gq_ref[...], n_heads, QK_SCALE).astype(BF16)
    k_ref[0, 0] = _head_norm(proj(wk_ref), gk_ref[...], n_heads, 1.0).astype(BF16)
    v_ref[0, 0] = proj(wv_ref).astype(BF16)
    g_ref[0, 0] = proj(wg_ref).astype(BF16)


def _fox_proj(x, a_norm, w_in, b_f, g_q, g_k):
    b, s, d = x.shape
    n_heads = b_f.shape[0]
    width = n_heads * HEAD_DIM
    tm = ROW_TILE
    w_t = w_in.T.astype(BF16)
    wq, wk, wv, wg, wf = (w_t[0:width], w_t[width:2 * width], w_t[2 * width:3 * width],
                          w_t[3 * width:4 * width], w_t[4 * width:])
    const = lambda *_: (0, 0)
    w_spec = pl.BlockSpec((width, d), const)
    col = pl.BlockSpec((HEAD_DIM, 1), const)
    chunked = pl.BlockSpec((1, 1, width, tm), lambda bi, i: (bi, i, 0, 0))
    return pl.pallas_call(
        functools.partial(_fox_proj_kernel, n_heads=n_heads),
        grid=(b, s // tm),
        in_specs=[
            pl.BlockSpec((1, tm, d), lambda bi, i: (bi, i, 0)),
            pl.BlockSpec((1, d), const),
            w_spec, w_spec, w_spec, w_spec,
            pl.BlockSpec((n_heads, d), const),
            pl.BlockSpec((n_heads, 1), const),
            col, col,
        ],
        out_specs=[chunked, chunked, chunked, chunked,
                   pl.BlockSpec((1, 1, n_heads, tm), lambda bi, i: (bi, i, 0, 0))],
        out_shape=[
            jax.ShapeDtypeStruct((b, s // tm, width, tm), BF16),
            jax.ShapeDtypeStruct((b, s // tm, width, tm), BF16),
            jax.ShapeDtypeStruct((b, s // tm, width, tm), BF16),
            jax.ShapeDtypeStruct((b, s // tm, width, tm), BF16),
            jax.ShapeDtypeStruct((b, s // tm, n_heads, tm), F32),
        ],
        scratch_shapes=[pltpu.VMEM((n_heads, V7X_LANES), F32)],
        compiler_params=_params(2),
        name="fox_proj",
    )(x, a_norm.reshape(1, d), wq, wk, wv, wg, wf, b_f.reshape(n_heads, 1),
      g_q.reshape(HEAD_DIM, 1), g_k.reshape(HEAD_DIM, 1))


def _store_scores(score_block, s_ref, mx_ref, idx, diagonal):
    t = s_ref.shape[1]
    if not diagonal:
        s = score_block(0, t, 0, t)
        s_ref[idx] = s
        mx_ref[idx] = jnp.max(s, axis=0, keepdims=True)
        return
    half = t // 2
    top = score_block(0, half, 0, t)
    bottom = score_block(half, half, half, half)
    s_ref[idx, 0:half, :] = top
    s_ref[idx, half:t, 0:half] = jnp.full((half, half), NEG_INF, F32)
    s_ref[idx, half:t, half:t] = bottom
    top_max = jnp.max(top, axis=0, keepdims=True)
    bottom_max = jnp.max(bottom, axis=0, keepdims=True)
    mx_ref[idx] = jnp.concatenate(
        [top_max[:, 0:half], jnp.maximum(top_max[:, half:t], bottom_max)], axis=1)


def _softmax_step(s_ref, mx_ref, m_ref, alpha_ref, p_ref, idx):
    m_prev = m_ref[idx]
    m_new = jnp.maximum(m_prev, mx_ref[idx])
    p_ref[idx] = jnp.exp2(s_ref[idx] - m_new).astype(BF16)
    alpha_ref[idx] = jnp.exp2(m_prev - m_new)
    m_ref[idx] = m_new


def _pv_step(v, p_ref, alpha_ref, acc_ref, idx, diagonal=False):
    if not diagonal:
        pv = jnp.dot(v, p_ref[idx], preferred_element_type=F32)
    else:
        t = p_ref.shape[1]
        half = t // 2
        top = jnp.dot(v[:, 0:half], p_ref[idx, 0:half, :], preferred_element_type=F32)
        bottom = jnp.dot(v[:, half:t], p_ref[idx, half:t, half:t], preferred_element_type=F32)
        pv = jnp.concatenate([top[:, 0:half], top[:, half:t] + bottom], axis=1)
    acc_ref[idx] = alpha_ref[idx] * acc_ref[idx] + pv


def _with_ones_rows(v):
    return jnp.concatenate([v, jnp.ones((ONES_ROWS, v.shape[1]), v.dtype)], axis=0)


def _reset_softmax_state(m_ref, acc_ref):
    m_ref[...] = jnp.full(m_ref.shape, NEG_INF, F32)
    acc_ref[...] = jnp.zeros_like(acc_ref)


def _run_row(n_q, tail, prepare, make_produce, consume, pv, close_tile, open_tile):
    n_tail = len(tail)
    assert n_q > n_tail + 1
    prepare(0)

    def kinds_of(q_tile):
        return ((("far",) * (q_tile + 1)) + tail)[-(q_tile + 1):]

    def first_kind(q_tile):
        return kinds_of(q_tile)[0] if q_tile <= n_tail else "far"

    def query_tile(qi, kinds):
        produce = make_produce(qi)
        successor = min(qi + 1, n_q - 1) if isinstance(qi, int) else jnp.minimum(qi + 1, n_q - 1)
        produce_next = functools.partial(make_produce(successor), 0)

        def step(j, produce_following):
            pv(j - 1)
            consume()
            produce_following()

        def first_step(produce_following):
            if not (isinstance(qi, int) and qi == 0):
                pv(qi - 1, diagonal=True)
                close_tile(qi - 1)
            open_tile()
            consume()
            produce_following()
            prepare(successor)

        def rest(first_tile, produced_kinds, following_first_kind):
            j = first_tile
            for kind in produced_kinds:
                step(j, functools.partial(produce, j + 1, kind))
                j = j + 1
            step(j, functools.partial(produce_next, following_first_kind))

        if kinds is not None:
            if qi == 0:
                produce(0, kinds[0])
                first_step(functools.partial(produce_next, first_kind(1)))
            else:
                first_step(functools.partial(produce, 1, kinds[1]))
                rest(1, kinds[2:], first_kind(qi + 1))
            return

        n_far = qi - n_tail - 1
        remainder = n_far % FAR_STEPS_PER_BLOCK

        def far_steps(first, count):
            for j in range(count):
                step(first + j, functools.partial(produce, first + j + 1, "far"))

        for r in range(FAR_STEPS_PER_BLOCK):
            @pl.when(remainder == r)
            def _(r=r):
                first_step(functools.partial(produce, 1, "far"))
                far_steps(1, r)

        def far_block(i, carry):
            far_steps(1 + remainder + FAR_STEPS_PER_BLOCK * i, FAR_STEPS_PER_BLOCK)
            return carry
        lax.fori_loop(0, n_far // FAR_STEPS_PER_BLOCK, far_block, 0)
        rest(qi - n_tail, tail, "far")

    for qi in range(n_tail + 1):
        query_tile(qi, kinds_of(qi))

    def general_tile(qi, carry):
        query_tile(qi, None)
        return carry
    lax.fori_loop(n_tail + 1, n_q, general_tile, 0)

    pv(n_q - 1, diagonal=True)
    close_tile(n_q - 1)


def _aug_rows(rows, t):
    shape = (V7X_SUBLANES, t)
    row = lax.broadcasted_iota(jnp.int32, shape, 0)
    out = jnp.zeros(shape, F32)
    for r, val in enumerate(rows):
        out = jnp.where(row == r, val, out)
    return out


def _fox_attn_kernel(q_ref, k_ref, v_ref, c_ref, g_ref, o_ref,
                     kaug_ref, vaug_ref, m_ref, alpha_ref, mx_ref, acc_ref, s_ref, p_ref,
                     *, hp, n_chunks):
    t = q_ref.shape[-1]
    pad = jnp.zeros((AUG_DEPTH - HEAD_DIM - V7X_SUBLANES, t), F32)

    def prepare(j):
        for h in range(hp):
            kt = k_ref[0, j, h * HEAD_DIM:(h + 1) * HEAD_DIM, :].astype(F32)
            hi, mid, lo = _split3(c_ref[0, j, 0, h:h + 1, :])
            aug = _aug_rows([-hi, -mid, -lo, 1.0, 1.0, 1.0], t)
            kaug_ref[h, j] = jnp.concatenate([kt, aug, pad], axis=0).T.astype(BF16)
            vaug_ref[h, j] = _with_ones_rows(v_ref[0, j, h * HEAD_DIM:(h + 1) * HEAD_DIM, :])

    def make_produce(qi):
        qa = []
        for h in range(hp):
            qf = q_ref[0, qi, h * HEAD_DIM:(h + 1) * HEAD_DIM, :].astype(F32)
            hi, mid, lo = _split3(c_ref[0, qi, 0, h:h + 1, :])
            aug = _aug_rows([1.0, 1.0, 1.0, hi, mid, lo], t)
            qa.append(jnp.concatenate([qf, aug, pad], axis=0).astype(BF16))

        def produce(j, kind):
            for h in range(hp):
                def score_block(r0, rn, c0, cn, h=h):
                    s = jnp.dot(kaug_ref[h, j, r0:r0 + rn, :], qa[h][:, c0:c0 + cn],
                                preferred_element_type=F32)
                    if kind == "diag":
                        kpos = r0 + lax.broadcasted_iota(jnp.int32, (rn, cn), 0)
                        qpos = c0 + lax.broadcasted_iota(jnp.int32, (rn, cn), 1)
                        s = jnp.where(kpos <= qpos, s, NEG_INF)
                    return s
                _store_scores(score_block, s_ref, mx_ref, h, kind == "diag")
        return produce

    def consume():
        for h in range(hp):
            _softmax_step(s_ref, mx_ref, m_ref, alpha_ref, p_ref, h)

    def pv(j, diagonal=False):
        for h in range(hp):
            _pv_step(vaug_ref[h, j], p_ref, alpha_ref, acc_ref, h, diagonal)

    def close_tile(qi):
        for h in range(hp):
            o = acc_ref[h, 0:HEAD_DIM] * (1.0 / acc_ref[h, HEAD_DIM:HEAD_DIM + 1])
            g = g_ref[0, qi, h * HEAD_DIM:(h + 1) * HEAD_DIM, :].astype(F32)
            o_ref[0, qi, h * HEAD_DIM:(h + 1) * HEAD_DIM, :] = (o * _silu(g)).astype(BF16)

    _run_row(n_chunks, ("diag",), prepare, make_produce, consume, pv, close_tile,
             functools.partial(_reset_softmax_state, m_ref, acc_ref))


def _fox_attn(q_c, k_c, v_c, c_c, g_c):
    b, n_chunks, width, t = q_c.shape
    hp = FOX_HEADS_PER_STEP
    n_heads = width // HEAD_DIM
    rows = hp * HEAD_DIM
    c_c = c_c.reshape(b, n_chunks, n_heads // hp, hp, t)
    full = pl.BlockSpec((1, n_chunks, rows, t), lambda bi, hg: (bi, 0, hg, 0))
    return pl.pallas_call(
        functools.partial(_fox_attn_kernel, hp=hp, n_chunks=n_chunks),
        grid=(b, n_heads // hp),
        in_specs=[
            full, full, full,
            pl.BlockSpec((1, n_chunks, 1, hp, t), lambda bi, hg: (bi, 0, hg, 0, 0)),
            full,
        ],
        out_specs=full,
        out_shape=jax.ShapeDtypeStruct((b, n_chunks, width, t), BF16),
        scratch_shapes=[
            pltpu.VMEM((hp, n_chunks, t, AUG_DEPTH), BF16),
            pltpu.VMEM((hp, n_chunks, HEAD_DIM + ONES_ROWS, t), BF16),
            pltpu.VMEM((hp, 1, t), F32),
            pltpu.VMEM((hp, 1, t), F32),
            pltpu.VMEM((hp, 1, t), F32),
            pltpu.VMEM((hp, HEAD_DIM + ONES_ROWS, t), F32),
            pltpu.VMEM((hp, t, t), F32),
            pltpu.VMEM((hp, t, t), BF16),
        ],
        compiler_params=_params(2),
        name="fox_attn",
    )(q_c, k_c, v_c, c_c, g_c)


def _mid_kernel(o_ref, x_ref, wo_ref, gkv_ref, gb_ref, wkv_ref, wb_ref, gk_ref, gq_ref,
                h_ref, k_ref, v_ref, q_ref, g_ref, *, n_heads):
    half = n_heads * HEAD_DIM
    h1 = x_ref[0] + lax.dot_general(o_ref[0, 0], wo_ref[...], _TN, preferred_element_type=F32)
    h_ref[0] = h1
    ms = jnp.mean(h1 * h1, axis=-1, keepdims=True)
    hn = h1 * lax.rsqrt(ms + RMS_EPS)
    u_kv = (hn * gkv_ref[...]).astype(BF16)
    u_b = (hn * gb_ref[...]).astype(BF16)

    def pack_heads(ref, a, b2):
        for h in range(n_heads):
            sl = slice(h * HEAD_DIM, (h + 1) * HEAD_DIM)
            ref[..., h * 2 * HEAD_DIM:h * 2 * HEAD_DIM + HEAD_DIM, :] = a[sl]
            ref[..., h * 2 * HEAD_DIM + HEAD_DIM:(h + 1) * 2 * HEAD_DIM, :] = b2[sl]

    def proj(w_ref, u, section):
        return lax.dot_general(w_ref[section * half:(section + 1) * half, :], u, _NT,
                               preferred_element_type=F32)

    k1 = _head_norm(proj(wkv_ref, u_kv, 0), gk_ref[0], n_heads, 1.0).astype(BF16)
    k2 = _head_norm(proj(wkv_ref, u_kv, 1), gk_ref[1], n_heads, 1.0).astype(BF16)
    pack_heads(k_ref.at[0, 0], k1, k2)
    for section in (2, 3):
        v_ref[0, 0, (section - 2) * half:(section - 1) * half, :] = proj(wkv_ref, u_kv, section).astype(BF16)

    q1 = _head_norm(proj(wb_ref, u_b, 0), gq_ref[0], n_heads, QK_SCALE).astype(BF16)
    q2 = _head_norm(proj(wb_ref, u_b, 1), gq_ref[1], n_heads, QK_SCALE).astype(BF16)
    pack_heads(q_ref.at[0, 0], q1, q2)
    for section in (2, 3):
        g_ref[0, 0, (section - 2) * half:(section - 1) * half, :] = proj(wb_ref, u_b, section).astype(BF16)


def _mid(o_c, x, w_out, kv_norm, kv_w, kv_k_norm, b_norm, b_w_in, b_q_norm):
    b, s, d = x.shape
    tm = ROW_TILE
    n_heads = kv_w.shape[1] // (4 * HEAD_DIM)
    width = 2 * n_heads * HEAD_DIM
    const2 = lambda *_: (0, 0)
    const3 = lambda *_: (0, 0, 0)
    tok = pl.BlockSpec((1, tm, d), lambda bi, i: (bi, i, 0))
    chunked = pl.BlockSpec((1, 1, width, tm), lambda bi, i: (bi, i, 0, 0))
    return pl.pallas_call(
        functools.partial(_mid_kernel, n_heads=n_heads),
        grid=(b, s // tm),
        in_specs=[
            pl.BlockSpec((1, 1, o_c.shape[2], tm), lambda bi, i: (bi, i, 0, 0)),
            tok,
            pl.BlockSpec(w_out.shape, const2),
            pl.BlockSpec((1, d), const2),
            pl.BlockSpec((1, d), const2),
            pl.BlockSpec((2 * width, d), const2),
            pl.BlockSpec((2 * width, d), const2),
            pl.BlockSpec((2, HEAD_DIM, 1), const3),
            pl.BlockSpec((2, HEAD_DIM, 1), const3),
        ],
        out_specs=[tok, chunked, chunked, chunked, chunked],
        out_shape=[
            jax.ShapeDtypeStruct((b, s, d), F32),
            jax.ShapeDtypeStruct((b, s // tm, width, tm), BF16),
            jax.ShapeDtypeStruct((b, s // tm, width, tm), BF16),
            jax.ShapeDtypeStruct((b, s // tm, width, tm), BF16),
            jax.ShapeDtypeStruct((b, s // tm, width, tm), BF16),
        ],
        compiler_params=_params(2),
        name="mid_proj",
    )(o_c, x, w_out.astype(BF16), kv_norm.reshape(1, d), b_norm.reshape(1, d),
      kv_w.T.astype(BF16), b_w_in.T.astype(BF16),
      kv_k_norm.reshape(2, HEAD_DIM, 1), b_q_norm.reshape(2, HEAD_DIM, 1))


def _diff_attn_kernel(q_ref, k_ref, v_ref, brow_ref, lam_ref, gs_ref, g_ref, o_ref,
                      ktok_ref, vaug_ref, bias_ref, m_ref, alpha_ref, mx_ref, acc_ref, s_ref, p_ref,
                      *, hd, n_chunks):
    t = q_ref.shape[-1]
    dv = DIFF_V_DIM
    streams = [(h, mp) for h in range(hd) for mp in range(2)]

    def prepare(j):
        for h in range(hd):
            ktok_ref[h, j] = k_ref[0, j, h * dv:(h + 1) * dv, :].astype(F32).T.astype(BF16)
            vaug_ref[h, j] = _with_ones_rows(v_ref[0, j, h * dv:(h + 1) * dv, :])

    @pl.when(pl.program_id(1) == 0)
    def _():
        for h, mp in streams:
            for off in range(2):
                rows = jnp.broadcast_to(brow_ref[h, mp, off], (t, 2 * t))
                bias_ref[h, mp, off] = pltpu.roll(rows, 0, 1, stride=1, stride_axis=0)[:, 0:t]

    def make_produce(qi):
        qa = []
        for h in range(hd):
            q = q_ref[0, qi, h * dv:(h + 1) * dv, :]
            row = lax.broadcasted_iota(jnp.int32, q.shape, 0)
            zero = jnp.zeros_like(q)
            qa += [jnp.where(row < HEAD_DIM, q, zero), jnp.where(row >= HEAD_DIM, q, zero)]

        def produce(j, kind):
            for i, (h, mp) in enumerate(streams):
                def score_block(r0, rn, c0, cn, i=i, h=h, mp=mp):
                    s = jnp.dot(ktok_ref[h, j, r0:r0 + rn, :], qa[i][:, c0:c0 + cn],
                                preferred_element_type=F32)
                    if kind == "near":
                        s = s + bias_ref[h, mp, 1, r0:r0 + rn, c0:c0 + cn]
                    elif kind == "diag":
                        kchunk = (r0 + lax.broadcasted_iota(jnp.int32, (rn, cn), 0)) // CHUNK
                        qchunk = (c0 + lax.broadcasted_iota(jnp.int32, (rn, cn), 1)) // CHUNK
                        s = jnp.where(kchunk <= qchunk,
                                      s + bias_ref[h, mp, 0, r0:r0 + rn, c0:c0 + cn], NEG_INF)
                    return s
                _store_scores(score_block, s_ref, mx_ref, i, kind == "diag")
        return produce

    def consume():
        for i in range(len(streams)):
            _softmax_step(s_ref, mx_ref, m_ref, alpha_ref, p_ref, i)

    def pv(j, diagonal=False):
        for i, (h, mp) in enumerate(streams):
            _pv_step(vaug_ref[h, j], p_ref, alpha_ref, acc_ref, i, diagonal)

    def close_tile(qi):
        lam_p = lam_ref[...]
        e1 = jnp.exp(jnp.sum(lam_p[0:1] * lam_p[1:2], axis=1, keepdims=True))
        e2 = jnp.exp(jnp.sum(lam_p[2:3] * lam_p[3:4], axis=1, keepdims=True))
        lam = e1 - e2 + LAMBDA_INIT
        for h in range(hd):
            a1, a2 = acc_ref[2 * h], acc_ref[2 * h + 1]
            o = a1[0:dv] * (1.0 / a1[dv:dv + 1]) - lam * (a2[0:dv] * (1.0 / a2[dv:dv + 1]))
            ms = jnp.mean(o * o, axis=0, keepdims=True)
            y = o * lax.rsqrt(ms + RMS_EPS) * gs_ref[...] * (1.0 - LAMBDA_INIT)
            g = g_ref[0, qi, h * dv:(h + 1) * dv, :].astype(F32)
            o_ref[0, qi, h * dv:(h + 1) * dv, :] = (y * _silu(g)).astype(BF16)

    _run_row(n_chunks, ("near", "diag"), prepare, make_produce, consume, pv, close_tile,
             functools.partial(_reset_softmax_state, m_ref, acc_ref))


def _t5_bucket(rel):
    half = NUM_BUCKETS // 2
    max_exact = half // 2
    ret = jnp.where(rel > 0, half, 0)
    n = jnp.abs(rel)
    n_f = jnp.maximum(n, 1).astype(jnp.float32)
    large = max_exact + (jnp.log(n_f / max_exact) / math.log(MAX_DISTANCE / max_exact)
                         * (half - max_exact)).astype(jnp.int32)
    large = jnp.minimum(large, half - 1)
    return ret + jnp.where(n < max_exact, n, large)


def _bias_rows(rel_bias, t):
    dw = jnp.arange(2 * t)
    d = jnp.where(dw < t, dw, dw - 2 * t)
    rel = jnp.stack([-d, -d - t])
    rows = rel_bias[_t5_bucket(rel)].astype(F32)
    far = rel_bias[_t5_bucket(jnp.int32(-MAX_DISTANCE))].astype(F32)
    return ((rows - far) * LOG2E).transpose(3, 2, 0, 1)[:, :, :, None, :]


def _diff_attn(q_c, k_c, v_c, g_c, rel_bias, lam_params, g_sub):
    b, n_chunks, width, t = q_c.shape
    n_heads = width // DIFF_V_DIM
    hd = DIFF_HEADS_PER_STEP
    assert t >= MAX_DISTANCE
    assert (t // 2) % CHUNK == 0
    rows = hd * DIFF_V_DIM
    n_streams = 2 * hd
    full = pl.BlockSpec((1, n_chunks, rows, t), lambda hg, bi: (bi, 0, hg, 0))
    return pl.pallas_call(
        functools.partial(_diff_attn_kernel, hd=hd, n_chunks=n_chunks),
        grid=(n_heads // hd, b),
        in_specs=[
            full, full, full,
            pl.BlockSpec((hd, 2, 2, 1, 2 * t), lambda hg, bi: (hg, 0, 0, 0, 0)),
            pl.BlockSpec((4, HEAD_DIM), lambda *_: (0, 0)),
            pl.BlockSpec((DIFF_V_DIM, 1), lambda *_: (0, 0)),
            full,
        ],
        out_specs=full,
        out_shape=jax.ShapeDtypeStruct((b, n_chunks, width, t), BF16),
        scratch_shapes=[
            pltpu.VMEM((hd, n_chunks, t, DIFF_V_DIM), BF16),
            pltpu.VMEM((hd, n_chunks, DIFF_V_DIM + ONES_ROWS, t), BF16),
            pltpu.VMEM((hd, 2, 2, t, t), F32),
            pltpu.VMEM((n_streams, 1, t), F32),
            pltpu.VMEM((n_streams, 1, t), F32),
            pltpu.VMEM((n_streams, 1, t), F32),
            pltpu.VMEM((n_streams, DIFF_V_DIM + ONES_ROWS, t), F32),
            pltpu.VMEM((n_streams, t, t), F32),
            pltpu.VMEM((n_streams, t, t), BF16),
        ],
        compiler_params=_params(2),
        name="diff_attn",
    )(q_c, k_c, v_c, _bias_rows(rel_bias, t), lam_params, g_sub.reshape(DIFF_V_DIM, 1), g_c)


def _out_kernel(o_ref, h_ref, w_ref, y_ref):
    y_ref[0] = h_ref[0] + lax.dot_general(o_ref[0, 0], w_ref[...], _TN, preferred_element_type=F32)


def _out_proj(o_c, h1, w_out):
    b, s, d = h1.shape
    tm = ROW_TILE
    tok = pl.BlockSpec((1, tm, d), lambda bi, i: (bi, i, 0))
    return pl.pallas_call(
        _out_kernel,
        grid=(b, s // tm),
        in_specs=[
            pl.BlockSpec((1, 1, o_c.shape[2], tm), lambda bi, i: (bi, i, 0, 0)),
            tok,
            pl.BlockSpec(w_out.shape, lambda *_: (0, 0)),
        ],
        out_specs=tok,
        out_shape=jax.ShapeDtypeStruct((b, s, d), F32),
        compiler_params=_params(2),
        name="out_proj",
    )(o_c, h1, w_out.astype(BF16))


def kernel(x, a_norm, a_w_in, a_b_f, a_q_norm, a_k_norm, a_w_out, kv_norm, kv_w, kv_k_norm,
           rel_bias, b_norm, b_w_in, b_q_norm, b_lam_q1, b_lam_k1, b_lam_q2, b_lam_k2,
           b_sub_norm, b_w_out):
    assert a_norm.shape[0] == 1 and b_norm.shape[0] == 1
    assert x.shape[1] % ATTN_TILE == 0 and ATTN_TILE == ROW_TILE
    q_c, k_c, v_c, g_c, c_c = _fox_proj(x, a_norm[0], a_w_in[0], a_b_f[0], a_q_norm[0], a_k_norm[0])
    o_c = _fox_attn(q_c, k_c, v_c, c_c, g_c)
    h1, k2_c, v2_c, q2_c, g2_c = _mid(o_c, x, a_w_out[0], kv_norm, kv_w, kv_k_norm,
                                      b_norm[0], b_w_in[0], b_q_norm[0])
    lam_params = jnp.stack([b_lam_q1[0], b_lam_k1[0], b_lam_q2[0], b_lam_k2[0]])
    o2_c = _diff_attn(q2_c, k2_c, v2_c, g2_c, rel_bias, lam_params, b_sub_norm[0])
    return _out_proj(o2_c, h1, b_w_out[0])
```
